```python
import math
import jax, jax.numpy as jnp
from jax import lax
import numpy as np

D_MODEL = 1024
BATCH = 8
SEQ = 4096
DEPTH = 2

GRID_W = 64
CTX_LEN = 256
N_EVEN = (DEPTH + 1) // 2
N_ODD = DEPTH // 2
EPS = 1e-6

NA_HEADS = 8
NA_HEAD_DIM = 64
D_A = NA_HEADS * NA_HEAD_DIM
NA_WIN_H_MAX = 8
NA_WIN_W = 16
D_B = 512
CONV_WIDTH = 31
D_IN_AB = 3 * D_A + 2 * D_B
ML_HEADS = 8
ML_QK_DIM = 64
ML_V_DIM = 128
D_CQK = ML_HEADS * ML_QK_DIM
D_CV = ML_HEADS * ML_V_DIM
ML_SHORT_CONV = 5
ML_CHUNK = 128
D_IN_C = 2 * D_CQK + 2 * D_CV + 4 * ML_HEADS
ROPE_BASE = 10000.0
D_FF = ((math.ceil(8 * D_MODEL / 3) + 255) // 256) * 256

kernel_name = 'hybrid_na_conformer_mlstm_dit_block'


def rms_norm(x, g):
    xf = x.astype(jnp.float32)
    y = xf * lax.rsqrt(jnp.mean(xf * xf, axis=-1, keepdims=True) + EPS)
    return (y * g.astype(jnp.float32)).astype(x.dtype)


def layer_norm(x, g, b):
    xf = x.astype(jnp.float32)
    mu = jnp.mean(xf, axis=-1, keepdims=True)
    var = jnp.mean(jnp.square(xf - mu), axis=-1, keepdims=True)
    y = (xf - mu) * lax.rsqrt(var + EPS)
    return (y * g.astype(jnp.float32) + b.astype(jnp.float32)).astype(x.dtype)


def modulate(x, g, shift, scale):
    return rms_norm(x, g) * (1 + scale) + shift


def depthwise_conv(x, w, b):
    width = w.shape[0]
    pad = width // 2
    y = lax.conv_general_dilated(x, w[:, None, :].astype(x.dtype), window_strides=(1,),
                                 padding=[(pad, width - 1 - pad)],
                                 dimension_numbers=('NWC', 'WIO', 'NWC'),
                                 feature_group_count=x.shape[-1])
    return y + b


def rope_1d(xa, pos):
    nf = xa.shape[-1] // 2
    inv = ROPE_BASE ** (-jnp.arange(nf, dtype=jnp.float32) / nf)
    ang = pos.astype(jnp.float32)[:, None] * inv[None, :]
    cos = jnp.cos(ang)[None, :, None, :]
    sin = jnp.sin(ang)[None, :, None, :]
    x1 = xa[..., :nf].astype(jnp.float32)
    x2 = xa[..., nf:].astype(jnp.float32)
    return jnp.concatenate([x1 * cos - x2 * sin, x1 * sin + x2 * cos], axis=-1).astype(xa.dtype)


def axial_rope(x, row_pos, col_pos):
    half = x.shape[-1] // 2
    return jnp.concatenate([rope_1d(x[..., :half], row_pos), rope_1d(x[..., half:], col_pos)], axis=-1)


def swiglu(h, w_gate, w_up, w_down):
    return (jax.nn.silu(h @ w_gate) * (h @ w_up)) @ w_down


def neighbourhood_attention(q, k, v, k_ctx, v_ctx, rpb):
    B, T, H, dh = q.shape
    rows = T // GRID_W
    win_h = min(NA_WIN_H_MAX, rows)
    n_loc = win_h * NA_WIN_W
    qg = q.reshape(B, rows, GRID_W, H, dh)
    kg = k.reshape(B, rows, GRID_W, H, dh)
    vg = v.reshape(B, rows, GRID_W, H, dh)
    col = jnp.arange(GRID_W)
    c0 = jnp.clip(col - NA_WIN_W // 2, 0, GRID_W - NA_WIN_W)
    col_idx = c0[:, None] + jnp.arange(NA_WIN_W)[None, :]
    col_rel = col_idx - col[:, None] + (NA_WIN_W - 1)
    rpb_cols = rpb[:, :, col_rel]

    def row_block(r):
        r0 = jnp.clip(r - win_h // 2, 0, rows - win_h)
        q_r = lax.dynamic_index_in_dim(qg, r, axis=1, keepdims=False)
        k_rows = lax.dynamic_slice_in_dim(kg, r0, win_h, axis=1)
        v_rows = lax.dynamic_slice_in_dim(vg, r0, win_h, axis=1)
        k_win = k_rows[:, :, col_idx]
        v_win = v_rows[:, :, col_idx]
        s_loc = jnp.einsum('bchd,bacwhd->bhcaw', q_r, k_win).astype(jnp.float32)
        row_rel = r0 + jnp.arange(win_h) - r + (NA_WIN_H_MAX - 1)
        bias = jnp.take(rpb_cols, row_rel, axis=1).transpose(0, 2, 1, 3)
        s_loc = (s_loc + bias[None].astype(jnp.float32)).reshape(B, H, GRID_W, n_loc)
        s_ctx = jnp.einsum('bchd,bnhd->bhcn', q_r, k_ctx).astype(jnp.float32)
        p = jax.nn.softmax(jnp.concatenate([s_loc, s_ctx], axis=-1), axis=-1).astype(v.dtype)
        p_loc = p[..., :n_loc].reshape(B, H, GRID_W, win_h, NA_WIN_W)
        p_ctx = p[..., n_loc:]
        return (jnp.einsum('bhcaw,bacwhd->bchd', p_loc, v_win)
                + jnp.einsum('bhcn,bnhd->bchd', p_ctx, v_ctx))

    o = lax.map(row_block, jnp.arange(rows))
    return o.transpose(1, 0, 2, 3, 4).reshape(B, T, H * dh)


def na_conv_mixer(h_lat, h_ctx, w_in, q_g, k_g, rpb, conv_w, conv_b, ln_g, ln_b, w_out, ctx_out):
    def project(h):
        B, T, _ = h.shape
        q, k, v, u, gt = jnp.split(h @ w_in, [D_A, 2 * D_A, 3 * D_A, 3 * D_A + D_B], axis=-1)
        heads = lambda a: a.reshape(B, T, NA_HEADS, NA_HEAD_DIM)
        q = rms_norm(heads(q), q_g) * NA_HEAD_DIM ** -0.5
        k = rms_norm(heads(k), k_g)
        return q, k, heads(v), u * jax.nn.sigmoid(gt)

    def conv_module(glu):
        y = depthwise_conv(glu, conv_w, conv_b)
        return jax.nn.silu(layer_norm(y, ln_g, ln_b))

    qc, kc, vc, glu_c = project(h_ctx)
    ql, kl, vl, glu_l = project(h_lat)
    att_lat = neighbourhood_attention(ql, kl, vl, kc, vc, rpb)
    out_lat = jnp.concatenate([att_lat, conv_module(glu_l)], axis=-1) @ w_out
    out_ctx = None
    if ctx_out:
        B, N = qc.shape[:2]
        s = jnp.einsum('bnhd,bmhd->bhnm', qc, kc).astype(jnp.float32)
        p = jax.nn.softmax(s, axis=-1).astype(vc.dtype)
        att_c = jnp.einsum('bhnm,bmhd->bnhd', p, vc).reshape(B, N, D_A)
        out_ctx = jnp.concatenate([att_c, conv_module(glu_c)], axis=-1) @ w_out
    return out_lat, out_ctx


def mlstm_chunkwise(q, k, v, ig, lf, state, return_h):
    B, H, T, dk = q.shape
    L = min(ML_CHUNK, T)
    nc = T // L

    def chunks(a):
        a = a.astype(jnp.float32).reshape(a.shape[:2] + (nc, L) + a.shape[3:])
        return jnp.moveaxis(a, 2, 0)

    causal = jnp.tril(jnp.ones((L, L), dtype=bool))

    def step(carry, xs):
        S, n, m = carry
        qc, kc, vc, igc, lfc = xs
        b = jnp.cumsum(lfc, axis=-1)
        b_end = b[..., -1]
        w_end = b_end[..., None] - b + igc
        m_new = jnp.maximum(b_end + m, jnp.max(w_end, axis=-1))
        decay = jnp.exp(b_end + m - m_new)
        wk = jnp.exp(w_end - m_new[..., None])
        S_new = decay[..., None, None] * S + jnp.einsum('bhl,bhlk,bhlv->bhkv', wk, kc, vc)
        n_new = decay[..., None] * n + jnp.einsum('bhl,bhlk->bhk', wk, kc)
        if not return_h:
            return (S_new, n_new, m_new), None
        g = b + m[..., None]
        dmat = jnp.where(causal, b[..., :, None] - b[..., None, :] + igc[..., None, :], -jnp.inf)
        m_q = jnp.maximum(g, jnp.max(dmat, axis=-1))
        p = jnp.exp(dmat - m_q[..., None]) * jnp.einsum('bhik,bhjk->bhij', qc, kc)
        inter = jnp.exp(g - m_q)
        num = inter[..., None] * jnp.einsum('bhik,bhkv->bhiv', qc, S) + jnp.einsum('bhij,bhjv->bhiv', p, vc)
        den = inter * jnp.einsum('bhik,bhk->bhi', qc, n) + jnp.sum(p, axis=-1)
        h = num / jnp.maximum(jnp.abs(den), jnp.exp(-m_q))[..., None]
        return (S_new, n_new, m_new), h

    carry, h = lax.scan(step, state, (chunks(q), chunks(k), chunks(v), chunks(ig), chunks(lf)))
    if return_h:
        h = jnp.moveaxis(h, 0, 2).reshape(B, H, T, -1)
    return h, carry


def mlstm_mixer(h_lat, h_ctx, w_in, conv_w, conv_b, gate_b, norm_g, w_out, row_pos, col_pos, ctx_out):
    def project(h, use_rope):
        B, T, _ = h.shape
        qk, v, o, gates = jnp.split(h @ w_in, [2 * D_CQK, 2 * D_CQK + D_CV, 2 * D_CQK + 2 * D_CV], axis=-1)
        qk = jax.nn.silu(depthwise_conv(qk, conv_w, conv_b))
        q, k = jnp.split(qk, 2, axis=-1)
        q = q.reshape(B, T, ML_HEADS, ML_QK_DIM) * ML_QK_DIM ** -0.5
        k = k.reshape(B, T, ML_HEADS, ML_QK_DIM)
        if use_rope:
            q = axial_rope(q, row_pos, col_pos)
            k = axial_rope(k, row_pos, col_pos)
        v = v.reshape(B, T, ML_HEADS, ML_V_DIM)
        heads = lambda a: jnp.transpose(a, (0, 2, 1, 3))
        g = jnp.transpose(gates.astype(jnp.float32) + gate_b.astype(jnp.float32), (0, 2, 1))
        ig_f, ig_b, fg_f, fg_b = jnp.split(g, 4, axis=1)
        return (heads(q), heads(k), heads(v), o, ig_f, ig_b,
                jax.nn.log_sigmoid(fg_f), jax.nn.log_sigmoid(fg_b))

    rev = lambda a: jnp.flip(a, axis=2)
    qc, kc, vc, oc, igf_c, igb_c, lff_c, lfb_c = project(h_ctx, False)
    ql, kl, vl, ol, igf_l, igb_l, lff_l, lfb_l = project(h_lat, True)
    B, H = ql.shape[:2]
    zero = (jnp.zeros((B, H, ML_QK_DIM, ML_V_DIM), jnp.float32),
            jnp.zeros((B, H, ML_QK_DIM), jnp.float32),
            jnp.zeros((B, H), jnp.float32))
    hcf, st_f = mlstm_chunkwise(qc, kc, vc, igf_c, lff_c, zero, ctx_out)
    hcb, st_b = mlstm_chunkwise(rev(qc), rev(kc), rev(vc), rev(igb_c), rev(lfb_c), zero, ctx_out)
    hlf, _ = mlstm_chunkwise(ql, kl, vl, igf_l, lff_l, st_f, True)
    hlb, _ = mlstm_chunkwise(rev(ql), rev(kl), rev(vl), rev(igb_l), rev(lfb_l), st_b, True)

    def finish(h, o):
        Bh, Hh, T, dv = h.shape
        h = rms_norm(jnp.transpose(h, (0, 2, 1, 3)), norm_g.reshape(Hh, dv)).reshape(Bh, T, Hh * dv)
        return (h.astype(o.dtype) * jax.nn.sigmoid(o)) @ w_out

    out_lat = finish(hlf + rev(hlb), ol)
    out_ctx = finish(hcf + rev(hcb), oc) if ctx_out else None
    return out_lat, out_ctx


def setup_inputs(seed: int = 0) -> dict:
    key = jax.random.key(seed)
    ks = iter(jax.random.split(key, 40))
    nrm = lambda shape, s: jax.random.normal(next(ks), shape, jnp.float32) * s
    D = D_MODEL
    inp = {}
    inp['x'] = nrm((BATCH, SEQ, D), 1.0)
    inp['c'] = nrm((BATCH, D), 1.0)
    inp['ctx'] = nrm((BATCH, CTX_LEN, D), 1.0)
    inp['c_ctx'] = nrm((D,), 1.0)
    inp['ada_w'] = nrm((DEPTH, D, 6 * D), 0.5 * D ** -0.5)
    inp['ada_b'] = nrm((DEPTH, 6 * D), 0.02)
    inp['norm_mix_g'] = 1.0 + nrm((DEPTH, D), 0.02)
    inp['norm_ffn_g'] = 1.0 + nrm((DEPTH, D), 0.02)
    inp['ffn_w_gate'] = nrm((DEPTH, D, D_FF), D ** -0.5)
    inp['ffn_w_up'] = nrm((DEPTH, D, D_FF), D ** -0.5)
    inp['ffn_w_down'] = nrm((DEPTH, D_FF, D), D_FF ** -0.5)
    inp['ab_w_in'] = nrm((N_EVEN, D, D_IN_AB), D ** -0.5)
    inp['na_q_norm_g'] = 1.0 + nrm((N_EVEN, NA_HEAD_DIM), 0.02)
    inp['na_k_norm_g'] = 1.0 + nrm((N_EVEN, NA_HEAD_DIM), 0.02)
    inp['na_rpb'] = nrm((N_EVEN, NA_HEADS, 2 * NA_WIN_H_MAX - 1, 2 * NA_WIN_W - 1), 0.1)
    inp['conv_w'] = nrm((N_EVEN, CONV_WIDTH, D_B), CONV_WIDTH ** -0.5)
    inp['conv_b'] = nrm((N_EVEN, D_B), 0.02)
    inp['conv_ln_g'] = 1.0 + nrm((N_EVEN, D_B), 0.02)
    inp['conv_ln_b'] = nrm((N_EVEN, D_B), 0.02)
    inp['ab_w_out'] = nrm((N_EVEN, D_A + D_B, D), (D_A + D_B) ** -0.5)
    inp['ml_w_in'] = nrm((N_ODD, D, D_IN_C), D ** -0.5)
    inp['ml_conv_w'] = nrm((N_ODD, ML_SHORT_CONV, 2 * D_CQK), ML_SHORT_CONV ** -0.5)
    inp['ml_conv_b'] = nrm((N_ODD, 2 * D_CQK), 0.02)
    inp['ml_gate_b'] = jnp.concatenate([nrm((N_ODD, 2 * ML_HEADS), 0.1),
                                        3.0 + nrm((N_ODD, 2 * ML_HEADS), 0.5)], axis=-1)
    inp['ml_norm_g'] = 1.0 + nrm((N_ODD, D_CV), 0.02)
    inp['ml_w_out'] = nrm((N_ODD, D_CV, D), D_CV ** -0.5)
    return inp


def reference(x, c, ctx, c_ctx, ada_w, ada_b, norm_mix_g, norm_ffn_g, ffn_w_gate, ffn_w_up, ffn_w_down,
              ab_w_in, na_q_norm_g, na_k_norm_g, na_rpb, conv_w, conv_b, conv_ln_g, conv_ln_b, ab_w_out,
              ml_w_in, ml_conv_w, ml_conv_b, ml_gate_b, ml_norm_g, ml_w_out):
    T = x.shape[1]
    t = jnp.arange(T)
    row_pos = t // GRID_W
    col_pos = t % GRID_W
    s_lat = jax.nn.silu(c)
    s_ctx = jax.nn.silu(c_ctx)
    for l in range(DEPTH):
        last = l == DEPTH - 1
        j = l // 2
        mod_lat = (s_lat @ ada_w[l] + ada_b[l])[:, None, :]
        mod_ctx = s_ctx @ ada_w[l] + ada_b[l]
        sh1, sc1, g1, sh2, sc2, g2 = jnp.split(mod_lat, 6, axis=-1)
        csh1, csc1, cg1, csh2, csc2, cg2 = jnp.split(mod_ctx, 6, axis=-1)
        h_lat = modulate(x, norm_mix_g[l], sh1, sc1)
        h_ctx = modulate(ctx, norm_mix_g[l], csh1, csc1)
        if l % 2 == 0:
            out_lat, out_ctx = na_conv_mixer(h_lat, h_ctx, ab_w_in[j], na_q_norm_g[j], na_k_norm_g[j], na_rpb[j],
                                             conv_w[j], conv_b[j], conv_ln_g[j], conv_ln_b[j], ab_w_out[j],
                                             not last)
        else:
            out_lat, out_ctx = mlstm_mixer(h_lat, h_ctx, ml_w_in[j], ml_conv_w[j], ml_conv_b[j], ml_gate_b[j],
                                           ml_norm_g[j], ml_w_out[j], row_pos, col_pos, not last)
        x = x + g1 * out_lat
        x = x + g2 * swiglu(modulate(x, norm_ffn_g[l], sh2, sc2), ffn_w_gate[l], ffn_w_up[l], ffn_w_down[l])
        if not last:
            ctx = ctx + cg1 * out_ctx
            ctx = ctx + cg2 * swiglu(modulate(ctx, norm_ffn_g[l], csh2, csc2),
                                     ffn_w_gate[l], ffn_w_up[l], ffn_w_down[l])
    return x
```

```python
import functools
import math

import jax
import jax.numpy as jnp
from jax import lax
from jax.experimental import pallas as pl
from jax.experimental.pallas import tpu as pltpu

F32 = jnp.float32
BF16 = jnp.bfloat16

EPS = 1e-6
GRID_W = 64
NA_HEADS = 8
NA_HEAD_DIM = 64
D_A = NA_HEADS * NA_HEAD_DIM
NA_WIN_H = 8
NA_WIN_W = 16
D_B = 512
CONV_WIDTH = 31
ML_HEADS = 8
ML_QK_DIM = 64
ML_V_DIM = 128
D_CQK = ML_HEADS * ML_QK_DIM
D_CV = ML_HEADS * ML_V_DIM
ML_SHORT_CONV = 5
ML_CHUNK = 128
ROPE_BASE = 10000.0

LANES = 128
HALO = 16
MOD_ROWS = 16
VMEM_LIMIT = 56 * 1024 * 1024
NEG_INF = float("-inf")


def _cparams(sem):
    return pltpu.CompilerParams(dimension_semantics=sem, vmem_limit_bytes=VMEM_LIMIT)


def _const_spec(shape):
    nd = len(shape)
    return pl.BlockSpec(shape, lambda *_: (0,) * nd, pipeline_mode=pl.Buffered(1))


def _sigmoid(x):
    return 1.0 / (1.0 + jnp.exp(-x))


def _silu(x):
    return x * _sigmoid(x)


def _modulate(x, g, shift, scale):
    ms = jnp.mean(x * x, axis=-1, keepdims=True)
    y = x * lax.rsqrt(ms + EPS) * g
    return y * (1.0 + scale) + shift


def _dot(a, b):
    return jnp.dot(a, b, preferred_element_type=F32)


def _dot_nt(a, b):
    return lax.dot_general(a, b, (((1,), (1,)), ((), ())), preferred_element_type=F32)


def _lo_lanes():
    return lax.broadcasted_iota(jnp.int32, (1, LANES), 1) < (LANES // 2)


def _ada_kernel(s_ref, w_ref, b_ref, o_ref):
    s = _silu(s_ref[...]).astype(BF16)
    o_ref[0] = _dot(s, w_ref[0].astype(BF16)) + b_ref[0]


def _ada_mod(s_rows, ada_w, ada_b):
    depth, d, n = ada_w.shape
    tn = n // 4
    return pl.pallas_call(
        _ada_kernel,
        grid=(depth, n // tn),
        in_specs=[pl.BlockSpec((MOD_ROWS, d), lambda l, j: (0, 0)),
                  pl.BlockSpec((1, d, tn), lambda l, j: (l, 0, j)),
                  pl.BlockSpec((1, 1, tn), lambda l, j: (l, 0, j))],
        out_specs=pl.BlockSpec((1, MOD_ROWS, tn), lambda l, j: (l, 0, j)),
        out_shape=jax.ShapeDtypeStruct((depth, MOD_ROWS, n), F32),
        compiler_params=_cparams(("arbitrary", "arbitrary")),
        name="ada_mod",
    )(s_rows, ada_w, ada_b.reshape(depth, 1, n))


def _mod_spec(d, sec, row_fn):
    return pl.BlockSpec((1, 1, d), lambda *idx: (row_fn(*idx), 0, sec))


def _ab_in_kernel(x_ref, sh_ref, sc_ref, g_ref, w_ref, qg_ref, kg_ref, q_ref, k_ref, v_ref, glu_ref):
    h = _modulate(x_ref[...], g_ref[...], sh_ref[0], sc_ref[0]).astype(BF16)
    lo = _lo_lanes()

    def head_norm(r, gain_ref, out_ref):
        for p in range(D_A // LANES):
            sl = slice(p * LANES, (p + 1) * LANES)
            xp = r[:, sl]
            sq = xp * xp
            s_all = jnp.sum(sq, axis=-1, keepdims=True)
            s_lo = jnp.sum(jnp.where(lo, sq, 0.0), axis=-1, keepdims=True)
            ms = jnp.where(lo, s_lo, s_all - s_lo) * (1.0 / NA_HEAD_DIM)
            out_ref[:, sl] = (xp * lax.rsqrt(ms + EPS) * gain_ref[:, sl]).astype(out_ref.dtype)

    head_norm(_dot(h, w_ref[:, 0:D_A]), qg_ref, q_ref)
    head_norm(_dot(h, w_ref[:, D_A:2 * D_A]), kg_ref, k_ref)
    v_ref[...] = _dot(h, w_ref[:, 2 * D_A:3 * D_A]).astype(v_ref.dtype)
    u = _dot(h, w_ref[:, 3 * D_A:3 * D_A + D_B])
    gt = _dot(h, w_ref[:, 3 * D_A + D_B:3 * D_A + 2 * D_B])
    glu_ref[...] = u * _sigmoid(gt)


def _ab_in(x2, mod3, row_fn, norm_g, w_in, qg, kg, tm):
    m, d = x2.shape
    tok = lambda c: pl.BlockSpec((tm, c), lambda i: (i, 0))
    return pl.pallas_call(
        _ab_in_kernel,
        grid=(m // tm,),
        in_specs=[tok(d), _mod_spec(d, 0, row_fn), _mod_spec(d, 1, row_fn), _const_spec((1, d)),
                  _const_spec(w_in.shape), _const_spec((1, D_A)), _const_spec((1, D_A))],
        out_specs=[tok(D_A), tok(D_A), tok(D_A), tok(D_B)],
        out_shape=[jax.ShapeDtypeStruct((m, D_A), BF16), jax.ShapeDtypeStruct((m, D_A), BF16),
                   jax.ShapeDtypeStruct((m, D_A), BF16), jax.ShapeDtypeStruct((m, D_B), F32)],
        compiler_params=_cparams(("arbitrary",)),
        name="ab_in",
    )(x2, mod3, mod3, norm_g, w_in, qg, kg)


def _softmax_pv(scores, values):
    m = functools.reduce(jnp.maximum, [jnp.max(s, axis=-1, keepdims=True) for s in scores])
    ps = [jnp.exp(s - m) for s in scores]
    l = functools.reduce(jnp.add, [jnp.sum(p, axis=-1, keepdims=True) for p in ps])
    o = functools.reduce(jnp.add, [_dot(p.astype(BF16), v) for p, v in zip(ps, values)])
    return o / l


def _na_kernel(q_ref, k_ref, v_ref, kc_ref, vc_ref, tb_ref, o_ref, *, rows):
    r = pl.program_id(1)
    r0 = jnp.clip(r - NA_WIN_H // 2, 0, rows - NA_WIN_H)
    start = pl.multiple_of(r0 * GRID_W, GRID_W)
    nk = NA_WIN_H * GRID_W
    lo = _lo_lanes()
    for p in range(D_A // LANES):
        sl = slice(p * LANES, (p + 1) * LANES)
        qp = q_ref[:, sl]
        kp = k_ref[pl.ds(start, nk), sl]
        vp = v_ref[pl.ds(start, nk), sl]
        kcp = kc_ref[:, sl]
        vcp = vc_ref[:, sl]
        outs = []
        for e in range(2):
            qm = jnp.where(lo if e == 0 else jnp.logical_not(lo), qp, jnp.zeros_like(qp))
            s_loc = _dot_nt(qm, kp) + tb_ref[0, 2 * p + e]
            s_ctx = _dot_nt(qm, kcp)
            outs.append(_softmax_pv([s_loc, s_ctx], [vp, vcp]))
        o_ref[:, sl] = jnp.where(lo, outs[0], outs[1]).astype(o_ref.dtype)


def _na_attention(q, k, v, kc, vc, table, batch, t, n):
    rows = t // GRID_W
    assert rows >= NA_WIN_H and t % GRID_W == 0
    rho = lambda r: jnp.clip(r - NA_WIN_H // 2, 0, rows - NA_WIN_H) - r + (NA_WIN_H - 1)
    return pl.pallas_call(
        functools.partial(_na_kernel, rows=rows),
        grid=(batch, rows),
        in_specs=[pl.BlockSpec((GRID_W, D_A), lambda b, r: (b * rows + r, 0)),
                  pl.BlockSpec((t, D_A), lambda b, r: (b, 0)),
                  pl.BlockSpec((t, D_A), lambda b, r: (b, 0)),
                  pl.BlockSpec((n, D_A), lambda b, r: (b, 0)),
                  pl.BlockSpec((n, D_A), lambda b, r: (b, 0)),
                  pl.BlockSpec((1, NA_HEADS, GRID_W, NA_WIN_H * GRID_W), lambda b, r: (rho(r), 0, 0, 0))],
        out_specs=pl.BlockSpec((GRID_W, D_A), lambda b, r: (b * rows + r, 0)),
        out_shape=jax.ShapeDtypeStruct((batch * t, D_A), BF16),
        compiler_params=_cparams(("arbitrary", "arbitrary")),
        name="na_attention",
    )(q, k, v, kc, vc, table)


def _ctx_attn_kernel(q_ref, k_ref, v_ref, o_ref):
    lo = _lo_lanes()
    for p in range(D_A // LANES):
        sl = slice(p * LANES, (p + 1) * LANES)
        qp = q_ref[:, sl]
        kp = k_ref[:, sl]
        vp = v_ref[:, sl]
        outs = []
        for e in range(2):
            qm = jnp.where(lo if e == 0 else jnp.logical_not(lo), qp, jnp.zeros_like(qp))
            outs.append(_softmax_pv([_dot_nt(qm, kp)], [vp]))
        o_ref[:, sl] = jnp.where(lo, outs[0], outs[1]).astype(o_ref.dtype)


def _ctx_attention(q, k, v, batch, n):
    spec = pl.BlockSpec((n, D_A), lambda b: (b, 0))
    return pl.pallas_call(
        _ctx_attn_kernel,
        grid=(batch,),
        in_specs=[spec, spec, spec],
        out_specs=spec,
        out_shape=jax.ShapeDtypeStruct((batch * n, D_A), BF16),
        compiler_params=_cparams(("arbitrary",)),
        name="ctx_attention",
    )(q, k, v)


def _na_bias_table(rpb, rows):
    col = jnp.arange(GRID_W)
    c0 = jnp.clip(col - NA_WIN_W // 2, 0, GRID_W - NA_WIN_W)
    kc = jnp.arange(GRID_W)
    inside = (kc[None, :] >= c0[:, None]) & (kc[None, :] < c0[:, None] + NA_WIN_W)
    col_rel = jnp.clip(kc[None, :] - col[:, None] + (NA_WIN_W - 1), 0, 2 * NA_WIN_W - 2)
    band = jnp.where(inside[None, None], rpb[:, :, col_rel], NEG_INF)
    row_rel = jnp.arange(NA_WIN_H)[:, None] + jnp.arange(NA_WIN_H)[None, :]
    tab = band[:, row_rel]
    tab = tab.transpose(1, 0, 3, 2, 4)
    return tab.reshape(NA_WIN_H, NA_HEADS, GRID_W, NA_WIN_H * GRID_W).astype(F32)


def _conv_kernel(prev_ref, x_ref, next_ref, cw_ref, cb_ref, lg_ref, lb_ref, o_ref, xs_ref, *, tt, nt, rc):
    i = pl.program_id(1)
    xs_ref[0:HALO] = jnp.where(i > 0, prev_ref[...], 0.0)
    xs_ref[HALO:HALO + tt] = x_ref[...]
    xs_ref[HALO + tt:2 * HALO + tt] = jnp.where(i < nt - 1, next_ref[...], 0.0)
    pad = CONV_WIDTH // 2
    for c in range(tt // rc):
        acc = jnp.broadcast_to(cb_ref[...], (rc, D_B))
        for w in range(CONV_WIDTH):
            off = HALO - pad + w + c * rc
            acc = acc + xs_ref[off:off + rc, :] * cw_ref[w:w + 1, :]
        mu = jnp.mean(acc, axis=-1, keepdims=True)
        xc = acc - mu
        var = jnp.mean(xc * xc, axis=-1, keepdims=True)
        y = xc * lax.rsqrt(var + EPS) * lg_ref[...] + lb_ref[...]
        o_ref[c * rc:(c + 1) * rc, :] = _silu(y).astype(o_ref.dtype)


def _conv_module(glu, cw, cb, lg, lb, batch, t):
    tt = min(t, 128)
    nt = t // tt
    hb = tt // HALO
    nhb = t // HALO
    return pl.pallas_call(
        functools.partial(_conv_kernel, tt=tt, nt=nt, rc=32),
        grid=(batch, nt),
        in_specs=[pl.BlockSpec((HALO, D_B), lambda b, i: (b * nhb + jnp.maximum(i * hb - 1, 0), 0)),
                  pl.BlockSpec((tt, D_B), lambda b, i: (b * nt + i, 0)),
                  pl.BlockSpec((HALO, D_B), lambda b, i: (b * nhb + jnp.minimum((i + 1) * hb, nhb - 1), 0)),
                  _const_spec((CONV_WIDTH, D_B)), _const_spec((1, D_B)), _const_spec((1, D_B)),
                  _const_spec((1, D_B))],
        out_specs=pl.BlockSpec((tt, D_B), lambda b, i: (b * nt + i, 0)),
        out_shape=jax.ShapeDtypeStruct((batch * t, D_B), BF16),
        scratch_shapes=[pltpu.VMEM((tt + 2 * HALO, D_B), F32)],
        compiler_params=_cparams(("arbitrary", "arbitrary")),
        name="conv_module",
    )(glu, glu, glu, cw, cb, lg, lb)


def _mix_ffn_kernel(x_ref, a1_ref, a2_ref, g1_ref, sh_ref, sc_ref, g2_ref, ng_ref,
                    wo_ref, wg_ref, wu_ref, wd_ref, o_ref, acc_ref, *, tf):
    ka = a1_ref.shape[1]
    mix = _dot(a1_ref[...], wo_ref[0:ka, :]) + _dot(a2_ref[...], wo_ref[ka:2 * ka, :])
    x1 = x_ref[...] + g1_ref[0] * mix
    h = _modulate(x1, ng_ref[...], sh_ref[0], sc_ref[0]).astype(BF16)
    d_ff = wg_ref.shape[1]
    for c in range(d_ff // tf):
        sl = slice(c * tf, (c + 1) * tf)
        act = (_silu(_dot(h, wg_ref[:, sl])) * _dot(h, wu_ref[:, sl])).astype(BF16)
        part = _dot(act, wd_ref[sl, :])
        if c == 0:
            acc_ref[...] = part
        else:
            acc_ref[...] += part
    o_ref[...] = x1 + g2_ref[0] * acc_ref[...]


def _mix_ffn(x2, a, a_cols, mod3, row_fn, norm_g, w_out, w_gate, w_up, w_down, tm):
    m, d = x2.shape
    a1, a2 = a
    ka = w_out.shape[0] // 2
    tok = lambda c, j=0: pl.BlockSpec((tm, c), lambda i: (i, j))
    return pl.pallas_call(
        functools.partial(_mix_ffn_kernel, tf=256),
        grid=(m // tm,),
        in_specs=[tok(d), tok(ka, a_cols[0]), tok(ka, a_cols[1]),
                  _mod_spec(d, 2, row_fn), _mod_spec(d, 3, row_fn), _mod_spec(d, 4, row_fn),
                  _mod_spec(d, 5, row_fn), _const_spec((1, d)),
                  _const_spec(w_out.shape), _const_spec(w_gate.shape), _const_spec(w_up.shape),
                  _const_spec(w_down.shape)],
        out_specs=tok(d),
        out_shape=jax.ShapeDtypeStruct((m, d), F32),
        scratch_shapes=[pltpu.VMEM((tm, d), F32)],
        compiler_params=_cparams(("arbitrary",)),
        name="mix_ffn",
    )(x2, a1, a2, mod3, mod3, mod3, mod3, norm_g, w_out, w_gate, w_up, w_down)


def _ml_in_kernel(xp_ref, x_ref, xn_ref, sh_ref, sc_ref, g_ref, w_ref, cw_ref, cb_ref, gb_ref, cos_ref, sin_ref,
                  q_ref, k_ref, v_ref, sg_ref, gates_ref, hs_ref, r_ref, *, tm, nt):
    i = pl.program_id(1)
    g, sh, sc = g_ref[...], sh_ref[0], sc_ref[0]
    hs_ref[0:HALO] = jnp.where(i > 0, _modulate(xp_ref[...], g, sh, sc), 0.0).astype(BF16)
    hs_ref[HALO:HALO + tm] = _modulate(x_ref[...], g, sh, sc).astype(BF16)
    hs_ref[HALO + tm:2 * HALO + tm] = jnp.where(i < nt - 1, _modulate(xn_ref[...], g, sh, sc), 0.0).astype(BF16)
    nqk = 2 * D_CQK
    r_ref[...] = _dot(hs_ref[...], w_ref[:, 0:nqk])
    pad = ML_SHORT_CONV // 2
    acc = jnp.broadcast_to(cb_ref[...], (tm, nqk))
    for w in range(ML_SHORT_CONV):
        off = HALO - pad + w
        acc = acc + r_ref[off:off + tm, :] * cw_ref[w:w + 1, :]
    qk = _silu(acc)
    cos = cos_ref[...]
    sin = sin_ref[...]
    lane = lax.broadcasted_iota(jnp.int32, (1, LANES), 1)
    first = (lane & 16) == 0
    for j in range(nqk // LANES):
        xg = qk[:, j * LANES:(j + 1) * LANES]
        sw = jnp.where(first, pltpu.roll(xg, LANES - 16, 1), pltpu.roll(xg, 16, 1))
        y = xg * cos + sw * sin
        if j < D_CQK // LANES:
            q_ref[:, j * LANES:(j + 1) * LANES] = (y * ML_QK_DIM ** -0.5).astype(q_ref.dtype)
        else:
            jj = j - D_CQK // LANES
            k_ref[:, jj * LANES:(jj + 1) * LANES] = y.astype(k_ref.dtype)
    hm = hs_ref[HALO:HALO + tm]
    v_ref[...] = _dot(hm, w_ref[:, nqk:nqk + D_CV]).astype(v_ref.dtype)
    sg_ref[...] = _sigmoid(_dot(hm, w_ref[:, nqk + D_CV:nqk + 2 * D_CV])).astype(sg_ref.dtype)
    gt = _dot(hm, w_ref[:, nqk + 2 * D_CV:nqk + 2 * D_CV + LANES]) + gb_ref[...]
    log_sig = jnp.minimum(gt, 0.0) - jnp.log(1.0 + jnp.exp(-jnp.abs(gt)))
    gates_ref[...] = jnp.where(lane < 2 * ML_HEADS, gt, log_sig)


def _ml_in(x2, mod3, row_fn, norm_g, w_in, cw, cb, gb, cos_t, sin_t, batch, t, tm):
    m, d = x2.shape
    nt = t // tm
    hb = tm // HALO
    nhb = t // HALO
    nqk = 2 * D_CQK
    tok = lambda c: pl.BlockSpec((tm, c), lambda b, i: (b * nt + i, 0))
    rf = lambda b, i: row_fn(b)
    return pl.pallas_call(
        functools.partial(_ml_in_kernel, tm=tm, nt=nt),
        grid=(batch, nt),
        in_specs=[pl.BlockSpec((HALO, d), lambda b, i: (b * nhb + jnp.maximum(i * hb - 1, 0), 0)),
                  tok(d),
                  pl.BlockSpec((HALO, d), lambda b, i: (b * nhb + jnp.minimum((i + 1) * hb, nhb - 1), 0)),
                  _mod_spec(d, 0, rf), _mod_spec(d, 1, rf), _const_spec((1, d)),
                  _const_spec(w_in.shape), _const_spec((ML_SHORT_CONV, nqk)), _const_spec((1, nqk)),
                  _const_spec((1, LANES)),
                  pl.BlockSpec((tm, LANES), lambda b, i: (i, 0)),
                  pl.BlockSpec((tm, LANES), lambda b, i: (i, 0))],
        out_specs=[tok(D_CQK), tok(D_CQK), tok(D_CV), tok(D_CV), tok(LANES)],
        out_shape=[jax.ShapeDtypeStruct((m, D_CQK), BF16), jax.ShapeDtypeStruct((m, D_CQK), BF16),
                   jax.ShapeDtypeStruct((m, D_CV), BF16), jax.ShapeDtypeStruct((m, D_CV), BF16),
                   jax.ShapeDtypeStruct((m, LANES), F32)],
        scratch_shapes=[pltpu.VMEM((tm + 2 * HALO, d), BF16), pltpu.VMEM((tm + 2 * HALO, nqk), F32)],
        compiler_params=_cparams(("arbitrary", "arbitrary")),
        name="ml_in",
    )(x2, x2, x2, mod3, mod3, norm_g, w_in, cw, cb, gb, cos_t, sin_t)


def _rope_tables(t, use_rope):
    lane = jnp.arange(LANES)
    in_head = lane % ML_QK_DIM
    nf = ML_QK_DIM // 4
    inv = ROPE_BASE ** (-jnp.arange(nf, dtype=F32) / nf)
    freq = inv[in_head % nf]
    sign = jnp.where((in_head // nf) % 2 == 0, -1.0, 1.0).astype(F32)
    if not use_rope:
        return jnp.ones((t, LANES), F32), jnp.zeros((t, LANES), F32)
    tok = jnp.arange(t)
    pos = jnp.where((in_head // (2 * nf))[None, :] == 0, (tok // GRID_W)[:, None], (tok % GRID_W)[:, None])
    ang = pos.astype(F32) * freq[None, :]
    return jnp.cos(ang), jnp.sin(ang) * sign[None, :]


def _gate_prep(gates, backward):
    L = gates.shape[0]
    ri = lax.broadcasted_iota(jnp.int32, (L, L), 0)
    ci = lax.broadcasted_iota(jnp.int32, (L, L), 1)
    tri = ((ci >= ri) if backward else (ci <= ri)).astype(F32)
    cum = jnp.dot(tri, gates, preferred_element_type=F32, precision=lax.Precision.HIGHEST)
    return gates.T, cum, cum.T, (ci >= ri) if backward else (ci <= ri)


def _dir_terms(gates, gates_t, cum, cum_t, h, backward):
    L = gates.shape[0]
    ig_lane = (ML_HEADS if backward else 0) + h
    lf_lane = (3 * ML_HEADS if backward else 2 * ML_HEADS) + h
    b_col = cum[:, lf_lane:lf_lane + 1]
    b_row = cum_t[lf_lane:lf_lane + 1, :]
    ig_col = gates[:, ig_lane:ig_lane + 1]
    ig_row = gates_t[ig_lane:ig_lane + 1, :]
    end = 0 if backward else L - 1
    b_end = cum[end:end + 1, lf_lane:lf_lane + 1]
    return b_col, b_row, ig_col, ig_row, b_end


def _state_update(s_pair, n_pair, ms, k_pair, v_ref_cols, terms, lo):
    rows_lo = lax.broadcasted_iota(jnp.int32, (LANES, 1), 0) < (LANES // 2)
    upd = None
    nsum = None
    decays = []
    m_news = []
    for e in range(2):
        b_col, _, ig_col, _, b_end = terms[e]
        w_end = b_end - b_col + ig_col
        m_new = jnp.maximum(b_end + ms[e], jnp.max(w_end, axis=0, keepdims=True))
        decays.append(jnp.exp(b_end + ms[e] - m_new))
        m_news.append(m_new)
        wk = jnp.exp(w_end - m_new)
        kw = jnp.where(lo if e == 0 else jnp.logical_not(lo), k_pair.astype(F32), 0.0) * wk
        part = _dot(kw.T.astype(BF16), v_ref_cols[e])
        upd = part if upd is None else upd + part
        ks = jnp.sum(kw, axis=0, keepdims=True)
        nsum = ks if nsum is None else nsum + ks
    s_new = jnp.where(rows_lo, decays[0], decays[1]) * s_pair + upd
    n_new = jnp.where(lo, decays[0], decays[1]) * n_pair + nsum
    return s_new, n_new, m_news


def _ml_state_kernel(k_ref, v_ref, gates_ref, s0_ref, a0_ref, spre_ref, apre_ref, sfin_ref, afin_ref, *, backward):
    c = pl.program_id(1)

    @pl.when(c == 0)
    def _():
        sfin_ref[...] = s0_ref[...]
        afin_ref[...] = a0_ref[...]

    spre_ref[0, 0] = sfin_ref[0]
    apre_ref[0, 0] = afin_ref[0]
    gates = gates_ref[...]
    gates_t, cum, cum_t, _ = _gate_prep(gates, backward)
    lo = _lo_lanes()
    for p in range(ML_HEADS // 2):
        sl = slice(p * LANES, (p + 1) * LANES)
        terms = [_dir_terms(gates, gates_t, cum, cum_t, 2 * p + e, backward) for e in range(2)]
        ms = [afin_ref[0, ML_HEADS + 2 * p + e:ML_HEADS + 2 * p + e + 1, 0:1] for e in range(2)]
        vs = [v_ref[:, (2 * p + e) * ML_V_DIM:(2 * p + e + 1) * ML_V_DIM] for e in range(2)]
        s_new, n_new, m_new = _state_update(sfin_ref[0, p], afin_ref[0, p:p + 1, :], ms, k_ref[:, sl], vs, terms, lo)
        sfin_ref[0, p] = s_new
        afin_ref[0, p:p + 1, :] = n_new
        for e in range(2):
            row = ML_HEADS + 2 * p + e
            afin_ref[0, row:row + 1, :] = jnp.broadcast_to(m_new[e], (1, LANES))


def _ml_state_scan(k, v, gates, s0, a0, batch, t, backward):
    L = min(ML_CHUNK, t)
    nc = t // L
    np_ = ML_HEADS // 2
    ci = (lambda c: nc - 1 - c) if backward else (lambda c: c)
    tok = lambda cols: pl.BlockSpec((L, cols), lambda b, c: (b * nc + ci(c), 0))
    st = pl.BlockSpec((1, np_, LANES, ML_V_DIM), lambda b, c: (b, 0, 0, 0))
    ax = pl.BlockSpec((1, 2 * ML_HEADS, LANES), lambda b, c: (b, 0, 0))
    return pl.pallas_call(
        functools.partial(_ml_state_kernel, backward=backward),
        grid=(batch, nc),
        in_specs=[tok(D_CQK), tok(D_CV), tok(LANES), st, ax],
        out_specs=[pl.BlockSpec((1, 1, np_, LANES, ML_V_DIM), lambda b, c: (b, ci(c), 0, 0, 0)),
                   pl.BlockSpec((1, 1, 2 * ML_HEADS, LANES), lambda b, c: (b, ci(c), 0, 0)), st, ax],
        out_shape=[jax.ShapeDtypeStruct((batch, nc, np_, LANES, ML_V_DIM), F32),
                   jax.ShapeDtypeStruct((batch, nc, 2 * ML_HEADS, LANES), F32),
                   jax.ShapeDtypeStruct((batch, np_, LANES, ML_V_DIM), F32),
                   jax.ShapeDtypeStruct((batch, 2 * ML_HEADS, LANES), F32)],
        compiler_params=_cparams(("arbitrary", "arbitrary")),
        name="ml_state_bwd" if backward else "ml_state_fwd",
    )(k, v, gates, s0, a0)


def _chunk_out(qm, k_pair, qk, v_h, s_pair, n_pair, m, terms, mask):
    b_col, b_row, _, ig_row, _ = terms
    g = b_col + m
    dmat = jnp.where(mask, b_col - b_row + ig_row, NEG_INF)
    m_q = jnp.maximum(g, jnp.max(dmat, axis=-1, keepdims=True))
    p = jnp.exp(dmat - m_q) * qk
    inter = jnp.exp(g - m_q)
    num = inter * _dot(qm, s_pair.astype(BF16)) + _dot(p.astype(BF16), v_h)
    qn = jnp.sum(qm.astype(F32) * n_pair, axis=-1, keepdims=True)
    den = inter * qn + jnp.sum(p, axis=-1, keepdims=True)
    return num / jnp.maximum(jnp.abs(den), jnp.exp(-m_q))


def _ml_out_kernel(q_ref, k_ref, v_ref, gates_ref, sg_ref, sb_ref, ab_ref, s0_ref, a0_ref, ng_ref, o_ref,
                   s_ref, a_ref):
    c = pl.program_id(1)

    @pl.when(c == 0)
    def _():
        s_ref[...] = s0_ref[0]
        a_ref[...] = a0_ref[0]

    gates = gates_ref[...]
    gt_f, cum_f, cumt_f, mask_f = _gate_prep(gates, False)
    _, cum_b, cumt_b, mask_b = _gate_prep(gates, True)
    lo = _lo_lanes()
    for p in range(ML_HEADS // 2):
        sl = slice(p * LANES, (p + 1) * LANES)
        qp = q_ref[:, sl]
        kp = k_ref[:, sl]
        s_f = s_ref[p]
        n_f = a_ref[p:p + 1, :]
        s_b = sb_ref[0, 0, p]
        n_b = ab_ref[0, 0, p:p + 1, :]
        terms_f, ms_f, vs = [], [], []
        for e in range(2):
            h = 2 * p + e
            hs = slice(h * ML_V_DIM, (h + 1) * ML_V_DIM)
            v_h = v_ref[:, hs]
            vs.append(v_h)
            tf = _dir_terms(gates, gt_f, cum_f, cumt_f, h, False)
            tb = _dir_terms(gates, gt_f, cum_b, cumt_b, h, True)
            m_f = a_ref[ML_HEADS + h:ML_HEADS + h + 1, 0:1]
            m_b = ab_ref[0, 0, ML_HEADS + h:ML_HEADS + h + 1, 0:1]
            terms_f.append(tf)
            ms_f.append(m_f)
            qm = jnp.where(lo if e == 0 else jnp.logical_not(lo), qp, jnp.zeros_like(qp))
            qk = _dot_nt(qm, kp)
            hsum = (_chunk_out(qm, kp, qk, v_h, s_f, n_f, m_f, tf, mask_f)
                    + _chunk_out(qm, kp, qk, v_h, s_b, n_b, m_b, tb, mask_b))
            ms = jnp.mean(hsum * hsum, axis=-1, keepdims=True)
            y = hsum * lax.rsqrt(ms + EPS) * ng_ref[:, hs]
            o_ref[:, hs] = (y.astype(BF16) * sg_ref[:, hs]).astype(o_ref.dtype)
        s_new, n_new, m_new = _state_update(s_f, n_f, ms_f, kp, vs, terms_f, lo)
        s_ref[p] = s_new
        a_ref[p:p + 1, :] = n_new
        for e in range(2):
            row = ML_HEADS + 2 * p + e
            a_ref[row:row + 1, :] = jnp.broadcast_to(m_new[e], (1, LANES))


def _ml_out(q, k, v, gates, sg, s_bwd, a_bwd, s0, a0, norm_g, batch, t):
    L = min(ML_CHUNK, t)
    nc = t // L
    np_ = ML_HEADS // 2
    tok = lambda cols: pl.BlockSpec((L, cols), lambda b, c: (b * nc + c, 0))
    return pl.pallas_call(
        _ml_out_kernel,
        grid=(batch, nc),
        in_specs=[tok(D_CQK), tok(D_CQK), tok(D_CV), tok(LANES), tok(D_CV),
                  pl.BlockSpec((1, 1, np_, LANES, ML_V_DIM), lambda b, c: (b, c, 0, 0, 0)),
                  pl.BlockSpec((1, 1, 2 * ML_HEADS, LANES), lambda b, c: (b, c, 0, 0)),
                  pl.BlockSpec((1, np_, LANES, ML_V_DIM), lambda b, c: (b, 0, 0, 0)),
                  pl.BlockSpec((1, 2 * ML_HEADS, LANES), lambda b, c: (b, 0, 0)),
                  _const_spec((1, D_CV))],
        out_specs=tok(D_CV),
        out_shape=jax.ShapeDtypeStruct((batch * t, D_CV), BF16),
        scratch_shapes=[pltpu.VMEM((np_, LANES, ML_V_DIM), F32), pltpu.VMEM((2 * ML_HEADS, LANES), F32)],
        compiler_params=_cparams(("arbitrary", "arbitrary")),
        name="ml_out",
    )(q, k, v, gates, sg, s_bwd, a_bwd, s0, a0, norm_g)


def _even_layer(x2, ctx2, mod3, lat_row, ctx_row, batch, t, n, norm_mix_g, norm_ffn_g, ffn_w, w_in, qg, kg,
                table, cw, cb, lg, lb, w_out, ctx_out):
    tm = min(512, t)
    tmc = min(512, batch * n)
    ql, kl, vl, glu_l = _ab_in(x2, mod3, lat_row(t // tm), norm_mix_g, w_in, qg, kg, tm)
    qc, kc, vc, glu_c = _ab_in(ctx2, mod3, ctx_row, norm_mix_g, w_in, qg, kg, tmc)
    att_l = _na_attention(ql, kl, vl, kc, vc, table, batch, t, n)
    conv_l = _conv_module(glu_l, cw, cb, lg, lb, batch, t)
    x2 = _mix_ffn(x2, (att_l, conv_l), (0, 0), mod3, lat_row(t // tm), norm_ffn_g, w_out, *ffn_w, tm)
    if ctx_out:
        att_c = _ctx_attention(qc, kc, vc, batch, n)
        conv_c = _conv_module(glu_c, cw, cb, lg, lb, batch, n)
        ctx2 = _mix_ffn(ctx2, (att_c, conv_c), (0, 0), mod3, ctx_row, norm_ffn_g, w_out, *ffn_w, tmc)
    return x2, ctx2


def _odd_layer(x2, ctx2, mod3, lat_row, ctx_row, batch, t, n, norm_mix_g, norm_ffn_g, ffn_w, w_in, cw, cb, gb,
               norm_g, w_out):
    tm = min(512, t)
    tmc = min(512, n)
    cos_l, sin_l = _rope_tables(t, True)
    cos_c, sin_c = _rope_tables(n, False)
    ql, kl, vl, sgl, gl = _ml_in(x2, mod3, lambda b: b, norm_mix_g, w_in, cw, cb, gb, cos_l, sin_l, batch, t, tm)
    _, kc, vc, _, gc = _ml_in(ctx2, mod3, lambda b: ctx_row(b), norm_mix_g, w_in, cw, cb, gb, cos_c, sin_c,
                              batch, n, tmc)
    np_ = ML_HEADS // 2
    s_zero = jnp.zeros((batch, np_, LANES, ML_V_DIM), F32)
    a_zero = jnp.zeros((batch, 2 * ML_HEADS, LANES), F32)
    _, _, sf, af = _ml_state_scan(kc, vc, gc, s_zero, a_zero, batch, n, False)
    _, _, sb, ab = _ml_state_scan(kc, vc, gc, s_zero, a_zero, batch, n, True)
    s_pre, a_pre, _, _ = _ml_state_scan(kl, vl, gl, sb, ab, batch, t, True)
    gated = _ml_out(ql, kl, vl, gl, sgl, s_pre, a_pre, sf, af, norm_g, batch, t)
    return _mix_ffn(x2, (gated, gated), (0, 1), mod3, lat_row(t // tm), norm_ffn_g, w_out, *ffn_w, tm)


def kernel(x, c, ctx, c_ctx, ada_w, ada_b, norm_mix_g, norm_ffn_g, ffn_w_gate, ffn_w_up, ffn_w_down, ab_w_in,
           na_q_norm_g, na_k_norm_g, na_rpb, conv_w, conv_b, conv_ln_g, conv_ln_b, ab_w_out, ml_w_in, ml_conv_w,
           ml_conv_b, ml_gate_b, ml_norm_g, ml_w_out):
    batch, t, d = x.shape
    n = ctx.shape[1]
    depth = ada_w.shape[0]
    assert batch + 1 <= MOD_ROWS and depth % 2 == 0, "odd layers are only implemented as the last-layer form"
    s_rows = jnp.zeros((MOD_ROWS, d), F32).at[:batch].set(c).at[batch].set(c_ctx)
    mod = _ada_mod(s_rows, ada_w, ada_b)
    x2 = x.reshape(batch * t, d)
    ctx2 = ctx.reshape(batch * n, d)
    lat_row = lambda tiles: (lambda i: i // tiles)
    ctx_row = lambda *_: batch
    rows = t // GRID_W
    for l in range(depth):
        j = l // 2
        last = l == depth - 1
        mod3 = mod[l].reshape(MOD_ROWS, 1, 6 * d)
        ffn_w = (ffn_w_gate[l].astype(BF16), ffn_w_up[l].astype(BF16), ffn_w_down[l].astype(BF16))
        nmg = norm_mix_g[l].reshape(1, d)
        nfg = norm_ffn_g[l].reshape(1, d)
        if l % 2 == 0:
            qg = (jnp.tile(na_q_norm_g[j], NA_HEADS) * NA_HEAD_DIM ** -0.5).reshape(1, D_A)
            kg = jnp.tile(na_k_norm_g[j], NA_HEADS).reshape(1, D_A)
            x2, ctx2 = _even_layer(x2, ctx2, mod3, lat_row, ctx_row, batch, t, n, nmg, nfg, ffn_w,
                                   ab_w_in[j].astype(BF16), qg, kg, _na_bias_table(na_rpb[j], rows),
                                   conv_w[j], conv_b[j].reshape(1, D_B), conv_ln_g[j].reshape(1, D_B),
                                   conv_ln_b[j].reshape(1, D_B), ab_w_out[j].astype(BF16), not last)
        else:
            assert last
            nqk = 2 * D_CQK
            w_in = jnp.pad(ml_w_in[j], ((0, 0), (0, LANES - 4 * ML_HEADS))).astype(BF16)
            gb = jnp.pad(ml_gate_b[j], (0, LANES - 4 * ML_HEADS)).reshape(1, LANES)
            x2 = _odd_layer(x2, ctx2, mod3, lat_row, ctx_row, batch, t, n, nmg, nfg, ffn_w, w_in,
                            ml_conv_w[j], ml_conv_b[j].reshape(1, nqk), gb, ml_norm_g[j].reshape(1, D_CV),
                            ml_w_out[j].astype(BF16))
    return x2.reshape(batch, t, d)
```

```python
import functools
import math

import jax
import jax.numpy as jnp
from jax import lax
from jax.experimental import pallas as pl
from jax.experimental.pallas import tpu as pltpu

F32 = jnp.float32
BF16 = jnp.bfloat16

EPS = 1e-6
GRID_W = 64
NA_HEADS = 8
NA_HEAD_DIM = 64
D_A = NA_HEADS * NA_HEAD_DIM
NA_WIN_H = 8
NA_WIN_W = 16
D_B = 512
CONV_WIDTH = 31
ML_HEADS = 8
ML_QK_DIM = 64
ML_V_DIM = 128
D_CQK = ML_HEADS * ML_QK_DIM
D_CV = ML_HEADS * ML_V_DIM
ML_SHORT_CONV = 5
ML_CHUNK = 128
ROPE_BASE = 10000.0

LANES = 128
SUBLANES = 8
HALO = 16
VT_ROWS = ML_V_DIM + HALO
MOD_ROWS = 16
VMEM_LIMIT = 56 * 1024 * 1024
NEG_INF = float("-inf")


def _cparams(sem):
    return pltpu.CompilerParams(dimension_semantics=sem, vmem_limit_bytes=VMEM_LIMIT)


def _const_spec(shape):
    nd = len(shape)
    return pl.BlockSpec(shape, lambda *_: (0,) * nd, pipeline_mode=pl.Buffered(1))


def _sigmoid(x):
    return 1.0 / (1.0 + jnp.exp(-x))


def _silu(x):
    return x * _sigmoid(x)


def _modulate(x, g, shift, scale):
    ms = jnp.mean(x * x, axis=-1, keepdims=True)
    y = x * lax.rsqrt(ms + EPS) * g
    return y * (1.0 + scale) + shift


def _dot(a, b):
    return jnp.dot(a, b, preferred_element_type=F32)


def _dot_nt(a, b):
    return lax.dot_general(a, b, (((1,), (1,)), ((), ())), preferred_element_type=F32)


def _lo_lanes():
    return lax.broadcasted_iota(jnp.int32, (1, LANES), 1) < (LANES // 2)


def _ada_kernel(s_ref, w_ref, b_ref, o_ref):
    s = _silu(s_ref[...]).astype(BF16)
    o_ref[0] = _dot(s, w_ref[0].astype(BF16)) + b_ref[0]


def _ada_mod(s_rows, ada_w, ada_b):
    depth, d, n = ada_w.shape
    tn = n // 4
    return pl.pallas_call(
        _ada_kernel,
        grid=(depth, n // tn),
        in_specs=[pl.BlockSpec((MOD_ROWS, d), lambda l, j: (0, 0)),
                  pl.BlockSpec((1, d, tn), lambda l, j: (l, 0, j)),
                  pl.BlockSpec((1, 1, tn), lambda l, j: (l, 0, j))],
        out_specs=pl.BlockSpec((1, MOD_ROWS, tn), lambda l, j: (l, 0, j)),
        out_shape=jax.ShapeDtypeStruct((depth, MOD_ROWS, n), F32),
        compiler_params=_cparams(("arbitrary", "arbitrary")),
        name="ada_mod",
    )(s_rows, ada_w, ada_b.reshape(depth, 1, n))


def _mod_spec(d, sec, row_fn):
    return pl.BlockSpec((1, 1, d), lambda *idx: (row_fn(*idx), 0, sec))


def _ab_in_kernel(x_ref, sh_ref, sc_ref, g_ref, w_ref, qg_ref, kg_ref, q_ref, k_ref, v_ref, glu_ref):
    h = _modulate(x_ref[...], g_ref[...], sh_ref[0], sc_ref[0]).astype(BF16)
    lo = _lo_lanes()

    def head_norm(r, gain_ref, out_ref):
        for p in range(D_A // LANES):
            sl = slice(p * LANES, (p + 1) * LANES)
            xp = r[:, sl]
            sq = xp * xp
            s_all = jnp.sum(sq, axis=-1, keepdims=True)
            s_lo = jnp.sum(jnp.where(lo, sq, 0.0), axis=-1, keepdims=True)
            ms = jnp.where(lo, s_lo, s_all - s_lo) * (1.0 / NA_HEAD_DIM)
            out_ref[:, sl] = (xp * lax.rsqrt(ms + EPS) * gain_ref[:, sl]).astype(out_ref.dtype)

    head_norm(_dot(h, w_ref[:, 0:D_A]), qg_ref, q_ref)
    head_norm(_dot(h, w_ref[:, D_A:2 * D_A]), kg_ref, k_ref)
    v_ref[...] = _dot(h, w_ref[:, 2 * D_A:3 * D_A]).astype(v_ref.dtype)
    u = _dot(h, w_ref[:, 3 * D_A:3 * D_A + D_B])
    gt = _dot(h, w_ref[:, 3 * D_A + D_B:3 * D_A + 2 * D_B])
    glu_ref[...] = u * _sigmoid(gt)


def _ab_in(x2, mod3, row_fn, norm_g, w_in, qg, kg, tm):
    m, d = x2.shape
    tok = lambda c: pl.BlockSpec((tm, c), lambda i: (i, 0))
    return pl.pallas_call(
        _ab_in_kernel,
        grid=(m // tm,),
        in_specs=[tok(d), _mod_spec(d, 0, row_fn), _mod_spec(d, 1, row_fn), _const_spec((1, d)),
                  _const_spec(w_in.shape), _const_spec((1, D_A)), _const_spec((1, D_A))],
        out_specs=[tok(D_A), tok(D_A), tok(D_A), tok(D_B)],
        out_shape=[jax.ShapeDtypeStruct((m, D_A), BF16), jax.ShapeDtypeStruct((m, D_A), BF16),
                   jax.ShapeDtypeStruct((m, D_A), BF16), jax.ShapeDtypeStruct((m, D_B), F32)],
        compiler_params=_cparams(("arbitrary",)),
        name="ab_in",
    )(x2, mod3, mod3, norm_g, w_in, qg, kg)


def _softmax_pv(scores, values):
    m = functools.reduce(jnp.maximum, [jnp.max(s, axis=-1, keepdims=True) for s in scores])
    ps = [jnp.exp(s - m) for s in scores]
    l = functools.reduce(jnp.add, [jnp.sum(p, axis=-1, keepdims=True) for p in ps])
    o = functools.reduce(jnp.add, [_dot(p.astype(BF16), v) for p, v in zip(ps, values)])
    return o / l


def _pair_rows(qp, lo):
    zero = jnp.zeros_like(qp)
    return jnp.concatenate([jnp.where(lo, qp, zero), jnp.where(lo, zero, qp)], axis=0)


def _na_kernel(q_ref, k_ref, v_ref, kc_ref, vc_ref, *rest, rows, rps):
    tb_refs, o_ref = rest[:rps], rest[rps]
    nk = NA_WIN_H * GRID_W
    lo = _lo_lanes()
    npair = D_A // LANES
    sls = [slice(p * LANES, (p + 1) * LANES) for p in range(npair)]
    starts = []
    for i in range(rps):
        r0 = jnp.clip(pl.program_id(1) * rps + i - NA_WIN_H // 2, 0, rows - NA_WIN_H)
        starts.append(pl.multiple_of(r0 * GRID_W, GRID_W))
    scores = []
    for i in range(rps):
        for p in range(npair):
            q2 = _pair_rows(q_ref[i * GRID_W:(i + 1) * GRID_W, sls[p]], lo)
            scores.append((_dot_nt(q2, k_ref[pl.ds(starts[i], nk), sls[p]]) + tb_refs[i][0, p],
                           _dot_nt(q2, kc_ref[:, sls[p]])))
    probs = []
    for s_loc, s_ctx in scores:
        m = jnp.maximum(jnp.max(s_loc, axis=-1, keepdims=True), jnp.max(s_ctx, axis=-1, keepdims=True))
        p_loc = jnp.exp(s_loc - m)
        p_ctx = jnp.exp(s_ctx - m)
        l = jnp.sum(p_loc, axis=-1, keepdims=True) + jnp.sum(p_ctx, axis=-1, keepdims=True)
        probs.append((p_loc.astype(BF16), p_ctx.astype(BF16), l))
    for i in range(rps):
        for p in range(npair):
            p_loc, p_ctx, l = probs[i * npair + p]
            o2 = (_dot(p_loc, v_ref[pl.ds(starts[i], nk), sls[p]]) + _dot(p_ctx, vc_ref[:, sls[p]])) / l
            o_ref[i * GRID_W:(i + 1) * GRID_W, sls[p]] = jnp.where(
                lo, o2[0:GRID_W], o2[GRID_W:2 * GRID_W]).astype(o_ref.dtype)


def _na_attention(q, k, v, kc, vc, table, batch, t, n, rps):
    rows = t // GRID_W
    assert rows >= NA_WIN_H and t % GRID_W == 0 and rows % rps == 0
    steps = rows // rps

    def table_spec(i):
        def index(b, s):
            r = s * rps + i
            return (jnp.clip(r - NA_WIN_H // 2, 0, rows - NA_WIN_H) - r + (NA_WIN_H - 1), 0, 0, 0)
        return pl.BlockSpec((1, NA_HEADS // 2, 2 * GRID_W, NA_WIN_H * GRID_W), index)

    return pl.pallas_call(
        functools.partial(_na_kernel, rows=rows, rps=rps),
        grid=(batch, steps),
        in_specs=[pl.BlockSpec((rps * GRID_W, D_A), lambda b, s: (b * steps + s, 0)),
                  pl.BlockSpec((t, D_A), lambda b, s: (b, 0)),
                  pl.BlockSpec((t, D_A), lambda b, s: (b, 0)),
                  pl.BlockSpec((n, D_A), lambda b, s: (b, 0)),
                  pl.BlockSpec((n, D_A), lambda b, s: (b, 0))] + [table_spec(i) for i in range(rps)],
        out_specs=pl.BlockSpec((rps * GRID_W, D_A), lambda b, s: (b * steps + s, 0)),
        out_shape=jax.ShapeDtypeStruct((batch * t, D_A), BF16),
        compiler_params=_cparams(("arbitrary", "arbitrary")),
        name="na_attention",
    )(q, k, v, kc, vc, *([table] * rps))


def _ctx_attn_kernel(q_ref, k_ref, v_ref, o_ref):
    lo = _lo_lanes()
    for p in range(D_A // LANES):
        sl = slice(p * LANES, (p + 1) * LANES)
        qp = q_ref[:, sl]
        kp = k_ref[:, sl]
        vp = v_ref[:, sl]
        outs = []
        for e in range(2):
            qm = jnp.where(lo if e == 0 else jnp.logical_not(lo), qp, jnp.zeros_like(qp))
            outs.append(_softmax_pv([_dot_nt(qm, kp)], [vp]))
        o_ref[:, sl] = jnp.where(lo, outs[0], outs[1]).astype(o_ref.dtype)


def _ctx_attention(q, k, v, batch, n):
    spec = pl.BlockSpec((n, D_A), lambda b: (b, 0))
    return pl.pallas_call(
        _ctx_attn_kernel,
        grid=(batch,),
        in_specs=[spec, spec, spec],
        out_specs=spec,
        out_shape=jax.ShapeDtypeStruct((batch * n, D_A), BF16),
        compiler_params=_cparams(("arbitrary",)),
        name="ctx_attention",
    )(q, k, v)


def _na_bias_table(rpb, rows):
    col = jnp.arange(GRID_W)
    c0 = jnp.clip(col - NA_WIN_W // 2, 0, GRID_W - NA_WIN_W)
    kc = jnp.arange(GRID_W)
    inside = (kc[None, :] >= c0[:, None]) & (kc[None, :] < c0[:, None] + NA_WIN_W)
    col_rel = jnp.clip(kc[None, :] - col[:, None] + (NA_WIN_W - 1), 0, 2 * NA_WIN_W - 2)
    band = jnp.where(inside[None, None], rpb[:, :, col_rel], NEG_INF)
    row_rel = jnp.arange(NA_WIN_H)[:, None] + jnp.arange(NA_WIN_H)[None, :]
    tab = band[:, row_rel]
    tab = tab.transpose(1, 0, 3, 2, 4)
    return tab.reshape(NA_WIN_H, NA_HEADS // 2, 2 * GRID_W, NA_WIN_H * GRID_W).astype(F32)


def _conv_kernel(prev_ref, x_ref, next_ref, cw_ref, cb_ref, lg_ref, lb_ref, o_ref, xs_ref, *, tt, nt, rc):
    i = pl.program_id(1)
    xs_ref[0, 0:HALO] = jnp.where(i > 0, prev_ref[...], 0.0)
    xs_ref[0, HALO:HALO + tt] = x_ref[...]
    xs_ref[0, HALO + tt:2 * HALO + tt] = jnp.where(i < nt - 1, next_ref[...], 0.0)
    span = tt + 2 * HALO - SUBLANES
    for s in range(1, SUBLANES):
        xs_ref[s, 0:span] = xs_ref[0, s:s + span]
    first = HALO - CONV_WIDTH // 2

    def chunk(c, carry):
        base = pl.multiple_of(c * rc, rc)
        acc = jnp.broadcast_to(cb_ref[...], (rc, D_B))
        for w in range(CONV_WIDTH):
            s, a = (first + w) % SUBLANES, (first + w) // SUBLANES
            acc = acc + xs_ref[s, pl.ds(base + a * SUBLANES, rc), :] * cw_ref[w:w + 1, :]
        mu = jnp.mean(acc, axis=-1, keepdims=True)
        xc = acc - mu
        var = jnp.mean(xc * xc, axis=-1, keepdims=True)
        y = xc * lax.rsqrt(var + EPS) * lg_ref[...] + lb_ref[...]
        o_ref[pl.ds(base, rc), :] = _silu(y).astype(o_ref.dtype)
        return carry

    lax.fori_loop(0, tt // rc, chunk, 0)


def _conv_module(glu, cw, cb, lg, lb, batch, t):
    tt = min(t, 512)
    nt = t // tt
    hb = tt // HALO
    nhb = t // HALO
    return pl.pallas_call(
        functools.partial(_conv_kernel, tt=tt, nt=nt, rc=32),
        grid=(batch, nt),
        in_specs=[pl.BlockSpec((HALO, D_B), lambda b, i: (b * nhb + jnp.maximum(i * hb - 1, 0), 0)),
                  pl.BlockSpec((tt, D_B), lambda b, i: (b * nt + i, 0)),
                  pl.BlockSpec((HALO, D_B), lambda b, i: (b * nhb + jnp.minimum((i + 1) * hb, nhb - 1), 0)),
                  _const_spec((CONV_WIDTH, D_B)), _const_spec((1, D_B)), _const_spec((1, D_B)),
                  _const_spec((1, D_B))],
        out_specs=pl.BlockSpec((tt, D_B), lambda b, i: (b * nt + i, 0)),
        out_shape=jax.ShapeDtypeStruct((batch * t, D_B), BF16),
        scratch_shapes=[pltpu.VMEM((SUBLANES, tt + 2 * HALO, D_B), F32)],
        compiler_params=_cparams(("arbitrary", "arbitrary")),
        name="conv_module",
    )(glu, glu, glu, cw, cb, lg, lb)


def _mix_ffn_kernel(x_ref, a1_ref, a2_ref, g1_ref, sh_ref, sc_ref, g2_ref, ng_ref,
                    wo_ref, wg_ref, wu_ref, wd_ref, o_ref, acc_ref, *, tf):
    ka = a1_ref.shape[1]
    mix = _dot(a1_ref[...], wo_ref[0:ka, :]) + _dot(a2_ref[...], wo_ref[ka:2 * ka, :])
    x1 = x_ref[...] + g1_ref[0] * mix
    h = _modulate(x1, ng_ref[...], sh_ref[0], sc_ref[0]).astype(BF16)
    d_ff = wg_ref.shape[1]
    for c in range(d_ff // tf):
        sl = slice(c * tf, (c + 1) * tf)
        act = (_silu(_dot(h, wg_ref[:, sl])) * _dot(h, wu_ref[:, sl])).astype(BF16)
        part = _dot(act, wd_ref[sl, :])
        if c == 0:
            acc_ref[...] = part
        else:
            acc_ref[...] += part
    o_ref[...] = x1 + g2_ref[0] * acc_ref[...]


def _mix_ffn(x2, a, a_cols, mod3, row_fn, norm_g, w_out, w_gate, w_up, w_down, tm):
    m, d = x2.shape
    a1, a2 = a
    ka = w_out.shape[0] // 2
    tok = lambda c, j=0: pl.BlockSpec((tm, c), lambda i: (i, j))
    return pl.pallas_call(
        functools.partial(_mix_ffn_kernel, tf=256),
        grid=(m // tm,),
        in_specs=[tok(d), tok(ka, a_cols[0]), tok(ka, a_cols[1]),
                  _mod_spec(d, 2, row_fn), _mod_spec(d, 3, row_fn), _mod_spec(d, 4, row_fn),
                  _mod_spec(d, 5, row_fn), _const_spec((1, d)),
                  _const_spec(w_out.shape), _const_spec(w_gate.shape), _const_spec(w_up.shape),
                  _const_spec(w_down.shape)],
        out_specs=tok(d),
        out_shape=jax.ShapeDtypeStruct((m, d), F32),
        scratch_shapes=[pltpu.VMEM((tm, d), F32)],
        compiler_params=_cparams(("arbitrary",)),
        name="mix_ffn",
    )(x2, a1, a2, mod3, mod3, mod3, mod3, norm_g, w_out, w_gate, w_up, w_down)


def _ml_in_kernel(xp_ref, x_ref, xn_ref, sh_ref, sc_ref, g_ref, w_ref, wvt_ref, cw_ref, cb_ref, gb_ref, cos_ref,
                  sin_ref, q_ref, k_ref, vt_ref, sg_ref, gates_ref, gatest_ref, hs_ref, r_ref, *, tm, nt):
    i = pl.program_id(1)
    g, sh, sc = g_ref[...], sh_ref[0], sc_ref[0]
    hs_ref[0:HALO] = jnp.where(i > 0, _modulate(xp_ref[...], g, sh, sc), 0.0).astype(BF16)
    hs_ref[HALO:HALO + tm] = _modulate(x_ref[...], g, sh, sc).astype(BF16)
    hs_ref[HALO + tm:2 * HALO + tm] = jnp.where(i < nt - 1, _modulate(xn_ref[...], g, sh, sc), 0.0).astype(BF16)
    nqk = 2 * D_CQK
    r_ref[...] = _dot(hs_ref[...], w_ref[:, 0:nqk])
    pad = ML_SHORT_CONV // 2
    acc = jnp.broadcast_to(cb_ref[...], (tm, nqk))
    for w in range(ML_SHORT_CONV):
        off = HALO - pad + w
        acc = acc + r_ref[off:off + tm, :] * cw_ref[w:w + 1, :]
    qk = _silu(acc)
    cos = cos_ref[...]
    sin = sin_ref[...]
    lane = lax.broadcasted_iota(jnp.int32, (1, LANES), 1)
    first = (lane & 16) == 0
    for j in range(nqk // LANES):
        xg = qk[:, j * LANES:(j + 1) * LANES]
        sw = jnp.where(first, pltpu.roll(xg, LANES - 16, 1), pltpu.roll(xg, 16, 1))
        y = xg * cos + sw * sin
        if j < D_CQK // LANES:
            q_ref[:, j * LANES:(j + 1) * LANES] = (y * ML_QK_DIM ** -0.5).astype(q_ref.dtype)
        else:
            jj = j - D_CQK // LANES
            k_ref[:, jj * LANES:(jj + 1) * LANES] = y.astype(k_ref.dtype)
    hm = hs_ref[HALO:HALO + tm]
    vt = _dot_nt(wvt_ref[...], hm).astype(vt_ref.dtype)
    ones_rows = (lax.broadcasted_iota(jnp.int32, (VT_ROWS - ML_V_DIM, tm), 0) == 0).astype(vt_ref.dtype)
    for h in range(ML_HEADS):
        vt_ref[h * VT_ROWS:h * VT_ROWS + ML_V_DIM, :] = vt[h * ML_V_DIM:(h + 1) * ML_V_DIM]
        vt_ref[h * VT_ROWS + ML_V_DIM:(h + 1) * VT_ROWS, :] = ones_rows
    sg_ref[...] = _sigmoid(_dot(hm, w_ref[:, nqk:nqk + D_CV])).astype(sg_ref.dtype)
    gt = _dot(hm, w_ref[:, nqk + D_CV:nqk + D_CV + LANES]) + gb_ref[...]
    log_sig = jnp.minimum(gt, 0.0) - jnp.log(1.0 + jnp.exp(-jnp.abs(gt)))
    gates = jnp.where(lane < 2 * ML_HEADS, gt, log_sig)
    gates_ref[...] = gates
    gatest_ref[...] = gates.T


def _ml_in(x2, mod3, row_fn, norm_g, w_in, w_vt, cw, cb, gb, cos_t, sin_t, batch, t, tm):
    m, d = x2.shape
    nt = t // tm
    hb = tm // HALO
    nhb = t // HALO
    nqk = 2 * D_CQK
    tok = lambda c: pl.BlockSpec((tm, c), lambda b, i: (b * nt + i, 0))
    rf = lambda b, i: row_fn(b)
    return pl.pallas_call(
        functools.partial(_ml_in_kernel, tm=tm, nt=nt),
        grid=(batch, nt),
        in_specs=[pl.BlockSpec((HALO, d), lambda b, i: (b * nhb + jnp.maximum(i * hb - 1, 0), 0)),
                  tok(d),
                  pl.BlockSpec((HALO, d), lambda b, i: (b * nhb + jnp.minimum((i + 1) * hb, nhb - 1), 0)),
                  _mod_spec(d, 0, rf), _mod_spec(d, 1, rf), _const_spec((1, d)),
                  _const_spec(w_in.shape), _const_spec(w_vt.shape), _const_spec((ML_SHORT_CONV, nqk)),
                  _const_spec((1, nqk)), _const_spec((1, LANES)),
                  pl.BlockSpec((tm, LANES), lambda b, i: (i, 0)),
                  pl.BlockSpec((tm, LANES), lambda b, i: (i, 0))],
        out_specs=[tok(D_CQK), tok(D_CQK), pl.BlockSpec((ML_HEADS * VT_ROWS, tm), lambda b, i: (b, i)),
                   tok(D_CV), tok(LANES), pl.BlockSpec((LANES, tm), lambda b, i: (b, i))],
        out_shape=[jax.ShapeDtypeStruct((m, D_CQK), BF16), jax.ShapeDtypeStruct((m, D_CQK), BF16),
                   jax.ShapeDtypeStruct((batch * ML_HEADS * VT_ROWS, t), BF16),
                   jax.ShapeDtypeStruct((m, D_CV), BF16), jax.ShapeDtypeStruct((m, LANES), F32),
                   jax.ShapeDtypeStruct((batch * LANES, t), F32)],
        scratch_shapes=[pltpu.VMEM((tm + 2 * HALO, d), BF16), pltpu.VMEM((tm + 2 * HALO, nqk), F32)],
        compiler_params=_cparams(("arbitrary", "arbitrary")),
        name="ml_in",
    )(x2, x2, x2, mod3, mod3, norm_g, w_in, w_vt, cw, cb, gb, cos_t, sin_t)


def _rope_tables(t, use_rope):
    lane = jnp.arange(LANES)
    in_head = lane % ML_QK_DIM
    nf = ML_QK_DIM // 4
    inv = ROPE_BASE ** (-jnp.arange(nf, dtype=F32) / nf)
    freq = inv[in_head % nf]
    sign = jnp.where((in_head // nf) % 2 == 0, -1.0, 1.0).astype(F32)
    if not use_rope:
        return jnp.ones((t, LANES), F32), jnp.zeros((t, LANES), F32)
    tok = jnp.arange(t)
    pos = jnp.where((in_head // (2 * nf))[None, :] == 0, (tok // GRID_W)[:, None], (tok % GRID_W)[:, None])
    ang = pos.astype(F32) * freq[None, :]
    return jnp.cos(ang), jnp.sin(ang) * sign[None, :]


def _cum_rows(gt, backward):
    L = gt.shape[1]
    ri = lax.broadcasted_iota(jnp.int32, (L, L), 0)
    ci = lax.broadcasted_iota(jnp.int32, (L, L), 1)
    mat = ((ri >= ci) if backward else (ri <= ci)).astype(F32)
    return jnp.dot(gt, mat, preferred_element_type=F32, precision=lax.Precision.HIGHEST)


def _cum_cols(g, backward):
    L = g.shape[0]
    ri = lax.broadcasted_iota(jnp.int32, (L, L), 0)
    ci = lax.broadcasted_iota(jnp.int32, (L, L), 1)
    mat = ((ci >= ri) if backward else (ci <= ri)).astype(F32)
    return jnp.dot(mat, g, preferred_element_type=F32, precision=lax.Precision.HIGHEST)


def _gate_rows(gt, cumt, h, backward):
    L = gt.shape[1]
    ig = (ML_HEADS if backward else 0) + h
    lf = (3 * ML_HEADS if backward else 2 * ML_HEADS) + h
    end = 0 if backward else L - 1
    return gt[ig:ig + 1, :], cumt[lf:lf + 1, :], cumt[lf:lf + 1, end:end + 1]


def _state_step(s_pair, k_pair, vts, rows, ms, lo):
    upd, decays, m_news = None, [], []
    for e in range(2):
        ig_row, b_row, b_end = rows[e]
        w_end = b_end - b_row + ig_row
        m_new = jnp.maximum(b_end + ms[e], jnp.max(w_end, axis=1, keepdims=True))
        decays.append(jnp.exp(b_end + ms[e] - m_new))
        m_news.append(m_new)
        wk = jnp.exp(w_end - m_new)
        lhs = (vts[e].astype(F32) * wk).astype(BF16)
        zero = jnp.zeros_like(k_pair)
        km = jnp.where(lo, k_pair, zero) if e == 0 else jnp.where(lo, zero, k_pair)
        part = _dot(lhs, km)
        upd = part if upd is None else upd + part
    return jnp.where(lo, decays[0], decays[1]) * s_pair + upd, m_news


def _ml_state_kernel(k_ref, vt_ref, gt_ref, s0_ref, m0_ref, spre_ref, mpre_ref, sfin_ref, mfin_ref, *,
                     backward, cps):
    @pl.when(pl.program_id(1) == 0)
    def _():
        sfin_ref[...] = s0_ref[...]
        mfin_ref[...] = m0_ref[...]

    L = ML_CHUNK
    lo = _lo_lanes()
    order = list(range(cps - 1, -1, -1)) if backward else list(range(cps))
    cumts = {u: _cum_rows(gt_ref[:, u * L:(u + 1) * L], backward) for u in order}
    for u in order:
        cs = slice(u * L, (u + 1) * L)
        spre_ref[0, u] = sfin_ref[0]
        mpre_ref[0, u] = mfin_ref[0]
        gt = gt_ref[:, cs]
        for p in range(ML_HEADS // 2):
            hs = (2 * p, 2 * p + 1)
            rows = [_gate_rows(gt, cumts[u], h, backward) for h in hs]
            ms = [mfin_ref[0, h:h + 1, 0:1] for h in hs]
            vts = [vt_ref[h * VT_ROWS:(h + 1) * VT_ROWS, cs] for h in hs]
            s_new, m_new = _state_step(sfin_ref[0, p], k_ref[cs, p * LANES:(p + 1) * LANES], vts, rows, ms, lo)
            sfin_ref[0, p] = s_new
            for e in range(2):
                mfin_ref[0, hs[e]:hs[e] + 1, :] = jnp.broadcast_to(m_new[e], (1, LANES))


def _ml_state_scan(k, vt, gt, s0, m0, batch, t, backward, cps):
    L = ML_CHUNK
    nc = t // L
    cps = min(cps, nc)
    assert t % L == 0 and nc % cps == 0
    steps = nc // cps
    np_ = ML_HEADS // 2
    si = (lambda s: steps - 1 - s) if backward else (lambda s: s)
    st = pl.BlockSpec((1, np_, VT_ROWS, LANES), lambda b, s: (b, 0, 0, 0))
    mx = pl.BlockSpec((1, ML_HEADS, LANES), lambda b, s: (b, 0, 0))
    return pl.pallas_call(
        functools.partial(_ml_state_kernel, backward=backward, cps=cps),
        grid=(batch, steps),
        in_specs=[pl.BlockSpec((cps * L, D_CQK), lambda b, s: (b * steps + si(s), 0)),
                  pl.BlockSpec((ML_HEADS * VT_ROWS, cps * L), lambda b, s: (b, si(s))),
                  pl.BlockSpec((LANES, cps * L), lambda b, s: (b, si(s))), st, mx],
        out_specs=[pl.BlockSpec((1, cps, np_, VT_ROWS, LANES), lambda b, s: (b, si(s), 0, 0, 0)),
                   pl.BlockSpec((1, cps, ML_HEADS, LANES), lambda b, s: (b, si(s), 0, 0)), st, mx],
        out_shape=[jax.ShapeDtypeStruct((batch, nc, np_, VT_ROWS, LANES), F32),
                   jax.ShapeDtypeStruct((batch, nc, ML_HEADS, LANES), F32),
                   jax.ShapeDtypeStruct((batch, np_, VT_ROWS, LANES), F32),
                   jax.ShapeDtypeStruct((batch, ML_HEADS, LANES), F32)],
        compiler_params=_cparams(("arbitrary", "arbitrary")),
        name="ml_state_bwd" if backward else "ml_state_fwd",
    )(k, vt, gt, s0, m0)


def _ml_out_kernel(q_ref, k_ref, vt_ref, g_ref, gt_ref, sg_ref, sb_ref, mb_ref, s0_ref, m0_ref, ngb_ref, o_ref,
                   s_ref, m_ref, *, cps):
    @pl.when(pl.program_id(1) == 0)
    def _():
        s_ref[...] = s0_ref[0]
        m_ref[...] = m0_ref[0]

    L = ML_CHUNK
    lo = _lo_lanes()
    ri = lax.broadcasted_iota(jnp.int32, (L, L), 0)
    ci = lax.broadcasted_iota(jnp.int32, (L, L), 1)
    masks = (ri <= ci, ri >= ci)
    npair = ML_HEADS // 2
    for u in range(cps):
        cs = slice(u * L, (u + 1) * L)
        g = g_ref[cs, :]
        gt = gt_ref[:, cs]
        cum = (_cum_cols(g, False), _cum_cols(g, True))
        cumt = (_cum_rows(gt, False), _cum_rows(gt, True))
        prods = []
        for p in range(npair):
            sl = slice(p * LANES, (p + 1) * LANES)
            stack = jnp.concatenate([k_ref[cs, sl], s_ref[p].astype(BF16), sb_ref[0, u, p].astype(BF16)], axis=0)
            qp = q_ref[cs, sl]
            zero = jnp.zeros_like(qp)
            prods.append(_dot_nt(stack, jnp.where(lo, qp, zero)))
            prods.append(_dot_nt(stack, jnp.where(lo, zero, qp)))
        pts, inters, mqs = [], [], []
        for h in range(ML_HEADS):
            for d in range(2):
                ig_row, b_row, _ = _gate_rows(gt, cumt[d], h, d == 1)
                ig = (ML_HEADS if d else 0) + h
                lf = (3 * ML_HEADS if d else 2 * ML_HEADS) + h
                r_col = g[:, ig:ig + 1] - cum[d][:, lf:lf + 1]
                m = mb_ref[0, u, h:h + 1, 0:1] if d else m_ref[h:h + 1, 0:1]
                g_row = b_row + m
                dt = jnp.where(masks[d], b_row + r_col, NEG_INF)
                m_q = jnp.maximum(g_row, jnp.max(dt, axis=0, keepdims=True))
                pts.append((jnp.exp(dt - m_q) * prods[h][0:L]).astype(BF16))
                inters.append(jnp.exp(g_row - m_q))
                mqs.append(m_q)
        nums = [_dot(vt_ref[(i // 2) * VT_ROWS:(i // 2 + 1) * VT_ROWS, cs], pts[i]) for i in range(2 * ML_HEADS)]
        for h in range(ML_HEADS):
            hsum = None
            for d in range(2):
                i = 2 * h + d
                tot = inters[i] * prods[h][L + d * VT_ROWS:L + (d + 1) * VT_ROWS] + nums[i]
                den = tot[ML_V_DIM:ML_V_DIM + 1]
                hd = tot[0:ML_V_DIM] / jnp.maximum(jnp.abs(den), jnp.exp(-mqs[i]))
                hsum = hd if hsum is None else hsum + hd
            hs = slice(h * ML_V_DIM, (h + 1) * ML_V_DIM)
            ms = jnp.mean(hsum * hsum, axis=0, keepdims=True)
            y = hsum * lax.rsqrt(ms + EPS) * ngb_ref[hs, :]
            o_ref[cs, hs] = (y.T.astype(BF16) * sg_ref[cs, hs]).astype(o_ref.dtype)
        for p in range(npair):
            hp = (2 * p, 2 * p + 1)
            rows = [_gate_rows(gt, cumt[0], h, False) for h in hp]
            ms = [m_ref[h:h + 1, 0:1] for h in hp]
            vts = [vt_ref[h * VT_ROWS:(h + 1) * VT_ROWS, cs] for h in hp]
            s_new, m_new = _state_step(s_ref[p], k_ref[cs, p * LANES:(p + 1) * LANES], vts, rows, ms, lo)
            s_ref[p] = s_new
            for e in range(2):
                m_ref[hp[e]:hp[e] + 1, :] = jnp.broadcast_to(m_new[e], (1, LANES))


def _ml_out(q, k, vt, g, gt, sg, s_bwd, m_bwd, s0, m0, norm_gb, batch, t, cps):
    L = ML_CHUNK
    nc = t // L
    assert t % L == 0 and nc % cps == 0
    steps = nc // cps
    np_ = ML_HEADS // 2
    tok = lambda cols: pl.BlockSpec((cps * L, cols), lambda b, s: (b * steps + s, 0))
    return pl.pallas_call(
        functools.partial(_ml_out_kernel, cps=cps),
        grid=(batch, steps),
        in_specs=[tok(D_CQK), tok(D_CQK),
                  pl.BlockSpec((ML_HEADS * VT_ROWS, cps * L), lambda b, s: (b, s)),
                  tok(LANES), pl.BlockSpec((LANES, cps * L), lambda b, s: (b, s)), tok(D_CV),
                  pl.BlockSpec((1, cps, np_, VT_ROWS, LANES), lambda b, s: (b, s, 0, 0, 0)),
                  pl.BlockSpec((1, cps, ML_HEADS, LANES), lambda b, s: (b, s, 0, 0)),
                  pl.BlockSpec((1, np_, VT_ROWS, LANES), lambda b, s: (b, 0, 0, 0)),
                  pl.BlockSpec((1, ML_HEADS, LANES), lambda b, s: (b, 0, 0)),
                  _const_spec((D_CV, LANES))],
        out_specs=tok(D_CV),
        out_shape=jax.ShapeDtypeStruct((batch * t, D_CV), BF16),
        scratch_shapes=[pltpu.VMEM((np_, VT_ROWS, LANES), F32), pltpu.VMEM((ML_HEADS, LANES), F32)],
        compiler_params=_cparams(("arbitrary", "arbitrary")),
        name="ml_out",
    )(q, k, vt, g, gt, sg, s_bwd, m_bwd, s0, m0, norm_gb)


def _even_layer(x2, ctx2, mod3, lat_row, ctx_row, batch, t, n, norm_mix_g, norm_ffn_g, ffn_w, w_in, qg, kg,
                table, cw, cb, lg, lb, w_out, ctx_out):
    tm = min(512, t)
    tmc = min(512, batch * n)
    ql, kl, vl, glu_l = _ab_in(x2, mod3, lat_row(t // tm), norm_mix_g, w_in, qg, kg, tm)
    qc, kc, vc, glu_c = _ab_in(ctx2, mod3, ctx_row, norm_mix_g, w_in, qg, kg, tmc)
    att_l = _na_attention(ql, kl, vl, kc, vc, table, batch, t, n, rps=2)
    conv_l = _conv_module(glu_l, cw, cb, lg, lb, batch, t)
    x2 = _mix_ffn(x2, (att_l, conv_l), (0, 0), mod3, lat_row(t // tm), norm_ffn_g, w_out, *ffn_w, tm)
    if ctx_out:
        att_c = _ctx_attention(qc, kc, vc, batch, n)
        conv_c = _conv_module(glu_c, cw, cb, lg, lb, batch, n)
        ctx2 = _mix_ffn(ctx2, (att_c, conv_c), (0, 0), mod3, ctx_row, norm_ffn_g, w_out, *ffn_w, tmc)
    return x2, ctx2


def _odd_layer(x2, ctx2, mod3, lat_row, ctx_row, batch, t, n, norm_mix_g, norm_ffn_g, ffn_w, w_in, w_vt, cw, cb, gb,
               norm_gb, w_out):
    tm = min(512, t)
    tmc = min(512, n)
    cos_l, sin_l = _rope_tables(t, True)
    cos_c, sin_c = _rope_tables(n, False)
    ql, kl, vtl, sgl, gl, gtl = _ml_in(x2, mod3, lambda b: b, norm_mix_g, w_in, w_vt, cw, cb, gb, cos_l, sin_l,
                                       batch, t, tm)
    _, kc, vtc, _, _, gtc = _ml_in(ctx2, mod3, lambda b: ctx_row(b), norm_mix_g, w_in, w_vt, cw, cb, gb, cos_c,
                                   sin_c, batch, n, tmc)
    s_zero = jnp.zeros((batch, ML_HEADS // 2, VT_ROWS, LANES), F32)
    m_zero = jnp.zeros((batch, ML_HEADS, LANES), F32)
    _, _, sf, mf = _ml_state_scan(kc, vtc, gtc, s_zero, m_zero, batch, n, False, cps=2)
    _, _, sb, mb = _ml_state_scan(kc, vtc, gtc, s_zero, m_zero, batch, n, True, cps=2)
    s_pre, m_pre, _, _ = _ml_state_scan(kl, vtl, gtl, sb, mb, batch, t, True, cps=4)
    gated = _ml_out(ql, kl, vtl, gl, gtl, sgl, s_pre, m_pre, sf, mf, norm_gb, batch, t, cps=1)
    return _mix_ffn(x2, (gated, gated), (0, 1), mod3, lat_row(t // tm), norm_ffn_g, w_out, *ffn_w, tm)


def kernel(x, c, ctx, c_ctx, ada_w, ada_b, norm_mix_g, norm_ffn_g, ffn_w_gate, ffn_w_up, ffn_w_down, ab_w_in,
           na_q_norm_g, na_k_norm_g, na_rpb, conv_w, conv_b, conv_ln_g, conv_ln_b, ab_w_out, ml_w_in, ml_conv_w,
           ml_conv_b, ml_gate_b, ml_norm_g, ml_w_out):
    batch, t, d = x.shape
    n = ctx.shape[1]
    depth = ada_w.shape[0]
    assert batch + 1 <= MOD_ROWS and depth % 2 == 0, "odd layers are only implemented as the last-layer form"
    s_rows = jnp.zeros((MOD_ROWS, d), F32).at[:batch].set(c).at[batch].set(c_ctx)
    mod = _ada_mod(s_rows, ada_w, ada_b)
    x2 = x.reshape(batch * t, d)
    ctx2 = ctx.reshape(batch * n, d)
    lat_row = lambda tiles: (lambda i: i // tiles)
    ctx_row = lambda *_: batch
    rows = t // GRID_W
    for l in range(depth):
        j = l // 2
        last = l == depth - 1
        mod3 = mod[l].reshape(MOD_ROWS, 1, 6 * d)
        ffn_w = (ffn_w_gate[l].astype(BF16), ffn_w_up[l].astype(BF16), ffn_w_down[l].astype(BF16))
        nmg = norm_mix_g[l].reshape(1, d)
        nfg = norm_ffn_g[l].reshape(1, d)
        if l % 2 == 0:
            qg = (jnp.tile(na_q_norm_g[j], NA_HEADS) * NA_HEAD_DIM ** -0.5).reshape(1, D_A)
            kg = jnp.tile(na_k_norm_g[j], NA_HEADS).reshape(1, D_A)
            x2, ctx2 = _even_layer(x2, ctx2, mod3, lat_row, ctx_row, batch, t, n, nmg, nfg, ffn_w,
                                   ab_w_in[j].astype(BF16), qg, kg, _na_bias_table(na_rpb[j], rows),
                                   conv_w[j], conv_b[j].reshape(1, D_B), conv_ln_g[j].reshape(1, D_B),
                                   conv_ln_b[j].reshape(1, D_B), ab_w_out[j].astype(BF16), not last)
        else:
            assert last
            nqk = 2 * D_CQK
            w = ml_w_in[j]
            w_gates = jnp.pad(w[:, nqk + 2 * D_CV:], ((0, 0), (0, LANES - 4 * ML_HEADS)))
            w_in = jnp.concatenate([w[:, :nqk], w[:, nqk + D_CV:nqk + 2 * D_CV], w_gates], axis=1).astype(BF16)
            w_vt = w[:, nqk:nqk + D_CV].T.astype(BF16)
            gb = jnp.pad(ml_gate_b[j], (0, LANES - 4 * ML_HEADS)).reshape(1, LANES)
            norm_gb = jnp.broadcast_to(ml_norm_g[j][:, None], (D_CV, LANES))
            x2 = _odd_layer(x2, ctx2, mod3, lat_row, ctx_row, batch, t, n, nmg, nfg, ffn_w, w_in, w_vt,
                            ml_conv_w[j], ml_conv_b[j].reshape(1, nqk), gb, norm_gb, ml_w_out[j].astype(BF16))
    return x2.reshape(batch, t, d)
```

```python
import functools
import math

import jax
import jax.numpy as jnp
from jax import lax
from jax.experimental import pallas as pl
from jax.experimental.pallas import tpu as pltpu

F32 = jnp.float32
BF16 = jnp.bfloat16

EPS = 1e-6
GRID_W = 64
NA_HEADS = 8
NA_HEAD_DIM = 64
D_A = NA_HEADS * NA_HEAD_DIM
NA_WIN_H = 8
NA_WIN_W = 16
D_B = 512
CONV_WIDTH = 31
ML_HEADS = 8
ML_QK_DIM = 64
ML_V_DIM = 128
D_CQK = ML_HEADS * ML_QK_DIM
D_CV = ML_HEADS * ML_V_DIM
ML_SHORT_CONV = 5
ML_CHUNK = 128
ROPE_BASE = 10000.0

LANES = 128
SUBLANES = 8
HALO = 16
VT_ROWS = ML_V_DIM + HALO
MOD_ROWS = 16
VMEM_LIMIT = 56 * 1024 * 1024
NEG_INF = float("-inf")


def _cparams(sem):
    return pltpu.CompilerParams(dimension_semantics=sem, vmem_limit_bytes=VMEM_LIMIT)


def _const_spec(shape):
    nd = len(shape)
    return pl.BlockSpec(shape, lambda *_: (0,) * nd, pipeline_mode=pl.Buffered(1))


def _sigmoid(x):
    return 1.0 / (1.0 + jnp.exp(-x))


def _silu(x):
    return x * _sigmoid(x)


def _modulate(x, g, shift, scale):
    ms = jnp.mean(x * x, axis=-1, keepdims=True)
    y = x * lax.rsqrt(ms + EPS) * g
    return y * (1.0 + scale) + shift


def _dot(a, b):
    return jnp.dot(a, b, preferred_element_type=F32)


def _dot_nt(a, b):
    return lax.dot_general(a, b, (((1,), (1,)), ((), ())), preferred_element_type=F32)


def _lo_lanes():
    return lax.broadcasted_iota(jnp.int32, (1, LANES), 1) < (LANES // 2)


def _ada_kernel(s_ref, w_ref, b_ref, o_ref):
    s = _silu(s_ref[...]).astype(BF16)
    o_ref[0] = _dot(s, w_ref[0].astype(BF16)) + b_ref[0]


def _ada_mod(s_rows, ada_w, ada_b):
    depth, d, n = ada_w.shape
    tn = n // 4
    return pl.pallas_call(
        _ada_kernel,
        grid=(depth, n // tn),
        in_specs=[pl.BlockSpec((MOD_ROWS, d), lambda l, j: (0, 0)),
                  pl.BlockSpec((1, d, tn), lambda l, j: (l, 0, j)),
                  pl.BlockSpec((1, 1, tn), lambda l, j: (l, 0, j))],
        out_specs=pl.BlockSpec((1, MOD_ROWS, tn), lambda l, j: (l, 0, j)),
        out_shape=jax.ShapeDtypeStruct((depth, MOD_ROWS, n), F32),
        compiler_params=_cparams(("arbitrary", "arbitrary")),
        name="ada_mod",
    )(s_rows, ada_w, ada_b.reshape(depth, 1, n))


def _mod_spec(d, sec, row_fn):
    return pl.BlockSpec((1, 1, d), lambda *idx: (row_fn(*idx), 0, sec))


def _ab_in_kernel(x_ref, sh_ref, sc_ref, g_ref, w_ref, qg_ref, kg_ref, q_ref, k_ref, v_ref, glu_ref):
    h = _modulate(x_ref[...], g_ref[...], sh_ref[0], sc_ref[0]).astype(BF16)
    lo = _lo_lanes()

    def head_norm(r, gain_ref, out_ref):
        for p in range(D_A // LANES):
            sl = slice(p * LANES, (p + 1) * LANES)
            xp = r[:, sl]
            sq = xp * xp
            s_all = jnp.sum(sq, axis=-1, keepdims=True)
            s_lo = jnp.sum(jnp.where(lo, sq, 0.0), axis=-1, keepdims=True)
            ms = jnp.where(lo, s_lo, s_all - s_lo) * (1.0 / NA_HEAD_DIM)
            out_ref[:, sl] = (xp * lax.rsqrt(ms + EPS) * gain_ref[:, sl]).astype(out_ref.dtype)

    head_norm(_dot(h, w_ref[:, 0:D_A]), qg_ref, q_ref)
    head_norm(_dot(h, w_ref[:, D_A:2 * D_A]), kg_ref, k_ref)
    v_ref[...] = _dot(h, w_ref[:, 2 * D_A:3 * D_A]).astype(v_ref.dtype)
    u = _dot(h, w_ref[:, 3 * D_A:3 * D_A + D_B])
    gt = _dot(h, w_ref[:, 3 * D_A + D_B:3 * D_A + 2 * D_B])
    glu_ref[...] = u * _sigmoid(gt)


def _ab_in(x2, mod3, row_fn, norm_g, w_in, qg, kg, tm):
    m, d = x2.shape
    tok = lambda c: pl.BlockSpec((tm, c), lambda i: (i, 0))
    return pl.pallas_call(
        _ab_in_kernel,
        grid=(m // tm,),
        in_specs=[tok(d), _mod_spec(d, 0, row_fn), _mod_spec(d, 1, row_fn), _const_spec((1, d)),
                  _const_spec(w_in.shape), _const_spec((1, D_A)), _const_spec((1, D_A))],
        out_specs=[tok(D_A), tok(D_A), tok(D_A), tok(D_B)],
        out_shape=[jax.ShapeDtypeStruct((m, D_A), BF16), jax.ShapeDtypeStruct((m, D_A), BF16),
                   jax.ShapeDtypeStruct((m, D_A), BF16), jax.ShapeDtypeStruct((m, D_B), F32)],
        compiler_params=_cparams(("arbitrary",)),
        name="ab_in",
    )(x2, mod3, mod3, norm_g, w_in, qg, kg)


def _softmax_pv(scores, values):
    m = functools.reduce(jnp.maximum, [jnp.max(s, axis=-1, keepdims=True) for s in scores])
    ps = [jnp.exp(s - m) for s in scores]
    l = functools.reduce(jnp.add, [jnp.sum(p, axis=-1, keepdims=True) for p in ps])
    o = functools.reduce(jnp.add, [_dot(p.astype(BF16), v) for p, v in zip(ps, values)])
    return o / l


def _pair_rows(qp, lo):
    zero = jnp.zeros_like(qp)
    return jnp.concatenate([jnp.where(lo, qp, zero), jnp.where(lo, zero, qp)], axis=0)


def _na_kernel(q_ref, k_ref, v_ref, kc_ref, vc_ref, *rest, rows, rps):
    tb_refs, o_ref = rest[:rps], rest[rps]
    nk = NA_WIN_H * GRID_W
    lo = _lo_lanes()
    npair = D_A // LANES
    sls = [slice(p * LANES, (p + 1) * LANES) for p in range(npair)]
    starts = []
    for i in range(rps):
        r0 = jnp.clip(pl.program_id(1) * rps + i - NA_WIN_H // 2, 0, rows - NA_WIN_H)
        starts.append(pl.multiple_of(r0 * GRID_W, GRID_W))
    scores = []
    for i in range(rps):
        for p in range(npair):
            q2 = _pair_rows(q_ref[i * GRID_W:(i + 1) * GRID_W, sls[p]], lo)
            scores.append((_dot_nt(q2, k_ref[pl.ds(starts[i], nk), sls[p]]) + tb_refs[i][0, p],
                           _dot_nt(q2, kc_ref[:, sls[p]])))
    probs = []
    for s_loc, s_ctx in scores:
        m = jnp.maximum(jnp.max(s_loc, axis=-1, keepdims=True), jnp.max(s_ctx, axis=-1, keepdims=True))
        p_loc = jnp.exp(s_loc - m)
        p_ctx = jnp.exp(s_ctx - m)
        l = jnp.sum(p_loc, axis=-1, keepdims=True) + jnp.sum(p_ctx, axis=-1, keepdims=True)
        probs.append((p_loc.astype(BF16), p_ctx.astype(BF16), l))
    for i in range(rps):
        for p in range(npair):
            p_loc, p_ctx, l = probs[i * npair + p]
            o2 = (_dot(p_loc, v_ref[pl.ds(starts[i], nk), sls[p]]) + _dot(p_ctx, vc_ref[:, sls[p]])) / l
            o_ref[i * GRID_W:(i + 1) * GRID_W, sls[p]] = jnp.where(
                lo, o2[0:GRID_W], o2[GRID_W:2 * GRID_W]).astype(o_ref.dtype)


def _na_attention(q, k, v, kc, vc, table, batch, t, n, rps):
    rows = t // GRID_W
    assert rows >= NA_WIN_H and t % GRID_W == 0 and rows % rps == 0
    steps = rows // rps

    def table_spec(i):
        def index(b, s):
            r = s * rps + i
            return (jnp.clip(r - NA_WIN_H // 2, 0, rows - NA_WIN_H) - r + (NA_WIN_H - 1), 0, 0, 0)
        return pl.BlockSpec((1, NA_HEADS // 2, 2 * GRID_W, NA_WIN_H * GRID_W), index)

    return pl.pallas_call(
        functools.partial(_na_kernel, rows=rows, rps=rps),
        grid=(batch, steps),
        in_specs=[pl.BlockSpec((rps * GRID_W, D_A), lambda b, s: (b * steps + s, 0)),
                  pl.BlockSpec((t, D_A), lambda b, s: (b, 0)),
                  pl.BlockSpec((t, D_A), lambda b, s: (b, 0)),
                  pl.BlockSpec((n, D_A), lambda b, s: (b, 0)),
                  pl.BlockSpec((n, D_A), lambda b, s: (b, 0))] + [table_spec(i) for i in range(rps)],
        out_specs=pl.BlockSpec((rps * GRID_W, D_A), lambda b, s: (b * steps + s, 0)),
        out_shape=jax.ShapeDtypeStruct((batch * t, D_A), BF16),
        compiler_params=_cparams(("arbitrary", "arbitrary")),
        name="na_attention",
    )(q, k, v, kc, vc, *([table] * rps))


def _ctx_attn_kernel(q_ref, k_ref, v_ref, o_ref):
    lo = _lo_lanes()
    for p in range(D_A // LANES):
        sl = slice(p * LANES, (p + 1) * LANES)
        qp = q_ref[:, sl]
        kp = k_ref[:, sl]
        vp = v_ref[:, sl]
        outs = []
        for e in range(2):
            qm = jnp.where(lo if e == 0 else jnp.logical_not(lo), qp, jnp.zeros_like(qp))
            outs.append(_softmax_pv([_dot_nt(qm, kp)], [vp]))
        o_ref[:, sl] = jnp.where(lo, outs[0], outs[1]).astype(o_ref.dtype)


def _ctx_attention(q, k, v, batch, n):
    spec = pl.BlockSpec((n, D_A), lambda b: (b, 0))
    return pl.pallas_call(
        _ctx_attn_kernel,
        grid=(batch,),
        in_specs=[spec, spec, spec],
        out_specs=spec,
        out_shape=jax.ShapeDtypeStruct((batch * n, D_A), BF16),
        compiler_params=_cparams(("arbitrary",)),
        name="ctx_attention",
    )(q, k, v)


def _na_bias_table(rpb, rows):
    col = jnp.arange(GRID_W)
    c0 = jnp.clip(col - NA_WIN_W // 2, 0, GRID_W - NA_WIN_W)
    kc = jnp.arange(GRID_W)
    inside = (kc[None, :] >= c0[:, None]) & (kc[None, :] < c0[:, None] + NA_WIN_W)
    col_rel = jnp.clip(kc[None, :] - col[:, None] + (NA_WIN_W - 1), 0, 2 * NA_WIN_W - 2)
    band = jnp.where(inside[None, None], rpb[:, :, col_rel], NEG_INF)
    row_rel = jnp.arange(NA_WIN_H)[:, None] + jnp.arange(NA_WIN_H)[None, :]
    tab = band[:, row_rel]
    tab = tab.transpose(1, 0, 3, 2, 4)
    return tab.reshape(NA_WIN_H, NA_HEADS // 2, 2 * GRID_W, NA_WIN_H * GRID_W).astype(F32)


def _conv_kernel(prev_ref, x_ref, next_ref, cw_ref, cb_ref, lg_ref, lb_ref, o_ref, xs_ref, *, tt, nt, rc):
    i = pl.program_id(1)
    xs_ref[0, 0:HALO] = jnp.where(i > 0, prev_ref[...], 0.0)
    xs_ref[0, HALO:HALO + tt] = x_ref[...]
    xs_ref[0, HALO + tt:2 * HALO + tt] = jnp.where(i < nt - 1, next_ref[...], 0.0)
    span = tt + 2 * HALO - SUBLANES
    for s in range(1, SUBLANES):
        xs_ref[s, 0:span] = xs_ref[0, s:s + span]
    first = HALO - CONV_WIDTH // 2

    def chunk(c, carry):
        base = pl.multiple_of(c * rc, rc)
        acc = jnp.broadcast_to(cb_ref[...], (rc, D_B))
        for w in range(CONV_WIDTH):
            s, a = (first + w) % SUBLANES, (first + w) // SUBLANES
            acc = acc + xs_ref[s, pl.ds(base + a * SUBLANES, rc), :] * cw_ref[w:w + 1, :]
        mu = jnp.mean(acc, axis=-1, keepdims=True)
        xc = acc - mu
        var = jnp.mean(xc * xc, axis=-1, keepdims=True)
        y = xc * lax.rsqrt(var + EPS) * lg_ref[...] + lb_ref[...]
        o_ref[pl.ds(base, rc), :] = _silu(y).astype(o_ref.dtype)
        return carry

    lax.fori_loop(0, tt // rc, chunk, 0, unroll=4)


def _conv_module(glu, cw, cb, lg, lb, batch, t):
    tt = min(t, 512)
    nt = t // tt
    hb = tt // HALO
    nhb = t // HALO
    return pl.pallas_call(
        functools.partial(_conv_kernel, tt=tt, nt=nt, rc=32),
        grid=(batch, nt),
        in_specs=[pl.BlockSpec((HALO, D_B), lambda b, i: (b * nhb + jnp.maximum(i * hb - 1, 0), 0)),
                  pl.BlockSpec((tt, D_B), lambda b, i: (b * nt + i, 0)),
                  pl.BlockSpec((HALO, D_B), lambda b, i: (b * nhb + jnp.minimum((i + 1) * hb, nhb - 1), 0)),
                  _const_spec((CONV_WIDTH, D_B)), _const_spec((1, D_B)), _const_spec((1, D_B)),
                  _const_spec((1, D_B))],
        out_specs=pl.BlockSpec((tt, D_B), lambda b, i: (b * nt + i, 0)),
        out_shape=jax.ShapeDtypeStruct((batch * t, D_B), BF16),
        scratch_shapes=[pltpu.VMEM((SUBLANES, tt + 2 * HALO, D_B), F32)],
        compiler_params=_cparams(("arbitrary", "arbitrary")),
        name="conv_module",
    )(glu, glu, glu, cw, cb, lg, lb)


def _mix_ffn_kernel(x_ref, a1_ref, a2_ref, g1_ref, sh_ref, sc_ref, g2_ref, ng_ref,
                    wo_ref, wg_ref, wu_ref, wd_ref, o_ref, acc_ref, *, tf):
    ka = a1_ref.shape[1]
    mix = _dot(a1_ref[...], wo_ref[0:ka, :]) + _dot(a2_ref[...], wo_ref[ka:2 * ka, :])
    x1 = x_ref[...] + g1_ref[0] * mix
    h = _modulate(x1, ng_ref[...], sh_ref[0], sc_ref[0]).astype(BF16)
    d_ff = wg_ref.shape[1]
    for c in range(d_ff // tf):
        sl = slice(c * tf, (c + 1) * tf)
        act = (_silu(_dot(h, wg_ref[:, sl])) * _dot(h, wu_ref[:, sl])).astype(BF16)
        part = _dot(act, wd_ref[sl, :])
        if c == 0:
            acc_ref[...] = part
        else:
            acc_ref[...] += part
    o_ref[...] = x1 + g2_ref[0] * acc_ref[...]


def _mix_ffn(x2, a, a_cols, mod3, row_fn, norm_g, w_out, w_gate, w_up, w_down, tm):
    m, d = x2.shape
    a1, a2 = a
    ka = w_out.shape[0] // 2
    tok = lambda c, j=0: pl.BlockSpec((tm, c), lambda i: (i, j))
    return pl.pallas_call(
        functools.partial(_mix_ffn_kernel, tf=256),
        grid=(m // tm,),
        in_specs=[tok(d), tok(ka, a_cols[0]), tok(ka, a_cols[1]),
                  _mod_spec(d, 2, row_fn), _mod_spec(d, 3, row_fn), _mod_spec(d, 4, row_fn),
                  _mod_spec(d, 5, row_fn), _const_spec((1, d)),
                  _const_spec(w_out.shape), _const_spec(w_gate.shape), _const_spec(w_up.shape),
                  _const_spec(w_down.shape)],
        out_specs=tok(d),
        out_shape=jax.ShapeDtypeStruct((m, d), F32),
        scratch_shapes=[pltpu.VMEM((tm, d), F32)],
        compiler_params=_cparams(("arbitrary",)),
        name="mix_ffn",
    )(x2, a1, a2, mod3, mod3, mod3, mod3, norm_g, w_out, w_gate, w_up, w_down)


def _ml_in_kernel(xp_ref, x_ref, xn_ref, sh_ref, sc_ref, g_ref, w_ref, wvt_ref, cw_ref, cb_ref, gb_ref, cos_ref,
                  sin_ref, q_ref, k_ref, vt_ref, sg_ref, gatest_ref, hs_ref, r_ref, *, tm, nt):
    i = pl.program_id(1)
    g, sh, sc = g_ref[...], sh_ref[0], sc_ref[0]
    hs_ref[0:HALO] = jnp.where(i > 0, _modulate(xp_ref[...], g, sh, sc), 0.0).astype(BF16)
    hs_ref[HALO:HALO + tm] = _modulate(x_ref[...], g, sh, sc).astype(BF16)
    hs_ref[HALO + tm:2 * HALO + tm] = jnp.where(i < nt - 1, _modulate(xn_ref[...], g, sh, sc), 0.0).astype(BF16)
    nqk = 2 * D_CQK
    lane = lax.broadcasted_iota(jnp.int32, (1, LANES), 1)
    r_ref[...] = _dot(hs_ref[...], w_ref[:, 0:nqk])
    hm = hs_ref[HALO:HALO + tm]
    vt = _dot_nt(wvt_ref[...], hm).astype(vt_ref.dtype)
    ones_rows = (lax.broadcasted_iota(jnp.int32, (VT_ROWS - ML_V_DIM, tm), 0) == 0).astype(vt_ref.dtype)
    for h in range(ML_HEADS):
        vt_ref[h * VT_ROWS:h * VT_ROWS + ML_V_DIM, :] = vt[h * ML_V_DIM:(h + 1) * ML_V_DIM]
        vt_ref[h * VT_ROWS + ML_V_DIM:(h + 1) * VT_ROWS, :] = ones_rows
    o_pre = _dot(hm, w_ref[:, nqk:nqk + D_CV])
    gt = _dot(hm, w_ref[:, nqk + D_CV:nqk + D_CV + LANES]) + gb_ref[...]
    sg_ref[...] = _sigmoid(o_pre).astype(sg_ref.dtype)
    log_sig = jnp.minimum(gt, 0.0) - jnp.log(1.0 + jnp.exp(-jnp.abs(gt)))
    gates = jnp.where(lane < 2 * ML_HEADS, gt, log_sig)
    gatest_ref[...] = gates.T
    pad = ML_SHORT_CONV // 2
    rows = tm + 2 * HALO
    qk_parts = []
    for j in range(nqk // LANES):
        rj = r_ref[:, j * LANES:(j + 1) * LANES]
        acc = jnp.broadcast_to(cb_ref[:, j * LANES:(j + 1) * LANES], (tm, LANES))
        for w in range(ML_SHORT_CONV):
            shifted = rj if w == pad else pltpu.roll(rj, (pad - w) % rows, 0)
            acc = acc + shifted[HALO:HALO + tm] * cw_ref[w:w + 1, j * LANES:(j + 1) * LANES]
        qk_parts.append(_silu(acc))
    cos = cos_ref[...]
    sin = sin_ref[...]
    first = (lane & 16) == 0
    for j in range(nqk // LANES):
        xg = qk_parts[j]
        sw = jnp.where(first, pltpu.roll(xg, LANES - 16, 1), pltpu.roll(xg, 16, 1))
        y = xg * cos + sw * sin
        if j < D_CQK // LANES:
            q_ref[:, j * LANES:(j + 1) * LANES] = (y * ML_QK_DIM ** -0.5).astype(q_ref.dtype)
        else:
            jj = j - D_CQK // LANES
            k_ref[:, jj * LANES:(jj + 1) * LANES] = y.astype(k_ref.dtype)


def _ml_in(x2, mod3, row_fn, norm_g, w_in, w_vt, cw, cb, gb, cos_t, sin_t, batch, t, tm):
    m, d = x2.shape
    nt = t // tm
    hb = tm // HALO
    nhb = t // HALO
    nqk = 2 * D_CQK
    tok = lambda c: pl.BlockSpec((tm, c), lambda b, i: (b * nt + i, 0))
    rf = lambda b, i: row_fn(b)
    return pl.pallas_call(
        functools.partial(_ml_in_kernel, tm=tm, nt=nt),
        grid=(batch, nt),
        in_specs=[pl.BlockSpec((HALO, d), lambda b, i: (b * nhb + jnp.maximum(i * hb - 1, 0), 0)),
                  tok(d),
                  pl.BlockSpec((HALO, d), lambda b, i: (b * nhb + jnp.minimum((i + 1) * hb, nhb - 1), 0)),
                  _mod_spec(d, 0, rf), _mod_spec(d, 1, rf), _const_spec((1, d)),
                  _const_spec(w_in.shape), _const_spec(w_vt.shape), _const_spec((ML_SHORT_CONV, nqk)),
                  _const_spec((1, nqk)), _const_spec((1, LANES)),
                  pl.BlockSpec((tm, LANES), lambda b, i: (i, 0)),
                  pl.BlockSpec((tm, LANES), lambda b, i: (i, 0))],
        out_specs=[tok(D_CQK), tok(D_CQK), pl.BlockSpec((ML_HEADS * VT_ROWS, tm), lambda b, i: (b, i)),
                   tok(D_CV), pl.BlockSpec((LANES, tm), lambda b, i: (b, i))],
        out_shape=[jax.ShapeDtypeStruct((m, D_CQK), BF16), jax.ShapeDtypeStruct((m, D_CQK), BF16),
                   jax.ShapeDtypeStruct((batch * ML_HEADS * VT_ROWS, t), BF16),
                   jax.ShapeDtypeStruct((m, D_CV), BF16),
                   jax.ShapeDtypeStruct((batch * LANES, t), F32)],
        scratch_shapes=[pltpu.VMEM((tm + 2 * HALO, d), BF16), pltpu.VMEM((tm + 2 * HALO, nqk), F32)],
        compiler_params=_cparams(("arbitrary", "arbitrary")),
        name="ml_in",
    )(x2, x2, x2, mod3, mod3, norm_g, w_in, w_vt, cw, cb, gb, cos_t, sin_t)


def _rope_tables(t, use_rope):
    lane = jnp.arange(LANES)
    in_head = lane % ML_QK_DIM
    nf = ML_QK_DIM // 4
    inv = ROPE_BASE ** (-jnp.arange(nf, dtype=F32) / nf)
    freq = inv[in_head % nf]
    sign = jnp.where((in_head // nf) % 2 == 0, -1.0, 1.0).astype(F32)
    if not use_rope:
        return jnp.ones((t, LANES), F32), jnp.zeros((t, LANES), F32)
    tok = jnp.arange(t)
    pos = jnp.where((in_head // (2 * nf))[None, :] == 0, (tok // GRID_W)[:, None], (tok % GRID_W)[:, None])
    ang = pos.astype(F32) * freq[None, :]
    return jnp.cos(ang), jnp.sin(ang) * sign[None, :]


N_GATES = 4 * ML_HEADS


def _cum_rows(gt, backward):
    L = gt.shape[1]
    x = gt[0:N_GATES]
    hi = x.astype(BF16)
    r1 = x - hi.astype(F32)
    mid = r1.astype(BF16)
    lo = (r1 - mid.astype(F32)).astype(BF16)
    ri = lax.broadcasted_iota(jnp.int32, (L, L), 0)
    ci = lax.broadcasted_iota(jnp.int32, (L, L), 1)
    mat = ((ri >= ci) if backward else (ri <= ci)).astype(F32).astype(BF16)
    out = _dot(jnp.concatenate([hi, mid, lo], axis=0), mat)
    return out[0:N_GATES] + out[N_GATES:2 * N_GATES] + out[2 * N_GATES:3 * N_GATES]


def _gate_rows(gt, cumt, h, backward):
    L = gt.shape[1]
    ig = (ML_HEADS if backward else 0) + h
    lf = (3 * ML_HEADS if backward else 2 * ML_HEADS) + h
    end = 0 if backward else L - 1
    return gt[ig:ig + 1, :], cumt[lf:lf + 1, :], cumt[lf:lf + 1, end:end + 1]


def _state_step(s_pair, k_pair, vts, rows, ms, lo):
    upd, decays, m_news = None, [], []
    for e in range(2):
        ig_row, b_row, b_end = rows[e]
        w_end = b_end - b_row + ig_row
        m_new = jnp.maximum(b_end + ms[e], jnp.max(w_end, axis=1, keepdims=True))
        decays.append(jnp.exp(b_end + ms[e] - m_new))
        m_news.append(m_new)
        wk = jnp.exp(w_end - m_new)
        lhs = (vts[e].astype(F32) * wk).astype(BF16)
        zero = jnp.zeros_like(k_pair)
        km = jnp.where(lo, k_pair, zero) if e == 0 else jnp.where(lo, zero, k_pair)
        part = _dot(lhs, km)
        upd = part if upd is None else upd + part
    return jnp.where(lo, decays[0], decays[1]) * s_pair + upd, m_news


def _ml_state_kernel(k_ref, vt_ref, gt_ref, s0_ref, m0_ref, spre_ref, mpre_ref, sfin_ref, mfin_ref, *,
                     backward, cps):
    @pl.when(pl.program_id(1) == 0)
    def _():
        sfin_ref[...] = s0_ref[...]
        mfin_ref[...] = m0_ref[...]

    L = ML_CHUNK
    lo = _lo_lanes()
    order = list(range(cps - 1, -1, -1)) if backward else list(range(cps))
    cumts = {u: _cum_rows(gt_ref[:, u * L:(u + 1) * L], backward) for u in order}
    for u in order:
        cs = slice(u * L, (u + 1) * L)
        spre_ref[0, u] = sfin_ref[0]
        mpre_ref[0, u] = mfin_ref[0]
        gt = gt_ref[:, cs]
        for p in range(ML_HEADS // 2):
            hs = (2 * p, 2 * p + 1)
            rows = [_gate_rows(gt, cumts[u], h, backward) for h in hs]
            ms = [mfin_ref[0, h:h + 1, 0:1] for h in hs]
            vts = [vt_ref[h * VT_ROWS:(h + 1) * VT_ROWS, cs] for h in hs]
            s_new, m_new = _state_step(sfin_ref[0, p], k_ref[cs, p * LANES:(p + 1) * LANES], vts, rows, ms, lo)
            sfin_ref[0, p] = s_new
            for e in range(2):
                mfin_ref[0, hs[e]:hs[e] + 1, :] = jnp.broadcast_to(m_new[e], (1, LANES))


def _ml_state_scan(k, vt, gt, s0, m0, batch, t, backward, cps):
    L = ML_CHUNK
    nc = t // L
    cps = min(cps, nc)
    assert t % L == 0 and nc % cps == 0
    steps = nc // cps
    np_ = ML_HEADS // 2
    si = (lambda s: steps - 1 - s) if backward else (lambda s: s)
    st = pl.BlockSpec((1, np_, VT_ROWS, LANES), lambda b, s: (b, 0, 0, 0))
    mx = pl.BlockSpec((1, ML_HEADS, LANES), lambda b, s: (b, 0, 0))
    return pl.pallas_call(
        functools.partial(_ml_state_kernel, backward=backward, cps=cps),
        grid=(batch, steps),
        in_specs=[pl.BlockSpec((cps * L, D_CQK), lambda b, s: (b * steps + si(s), 0)),
                  pl.BlockSpec((ML_HEADS * VT_ROWS, cps * L), lambda b, s: (b, si(s))),
                  pl.BlockSpec((LANES, cps * L), lambda b, s: (b, si(s))), st, mx],
        out_specs=[pl.BlockSpec((1, cps, np_, VT_ROWS, LANES), lambda b, s: (b, si(s), 0, 0, 0)),
                   pl.BlockSpec((1, cps, ML_HEADS, LANES), lambda b, s: (b, si(s), 0, 0)), st, mx],
        out_shape=[jax.ShapeDtypeStruct((batch, nc, np_, VT_ROWS, LANES), F32),
                   jax.ShapeDtypeStruct((batch, nc, ML_HEADS, LANES), F32),
                   jax.ShapeDtypeStruct((batch, np_, VT_ROWS, LANES), F32),
                   jax.ShapeDtypeStruct((batch, ML_HEADS, LANES), F32)],
        compiler_params=_cparams(("arbitrary", "arbitrary")),
        name="ml_state_bwd" if backward else "ml_state_fwd",
    )(k, vt, gt, s0, m0)


def _ml_out_kernel(q_ref, k_ref, vt_ref, gt_ref, sg_ref, sb_ref, mb_ref, s0_ref, m0_ref, ngb_ref, o_ref,
                   s_ref, m_ref, *, cps):
    @pl.when(pl.program_id(1) == 0)
    def _():
        s_ref[...] = s0_ref[0]
        m_ref[...] = m0_ref[0]

    L = ML_CHUNK
    lo = _lo_lanes()
    ri = lax.broadcasted_iota(jnp.int32, (L, L), 0)
    ci = lax.broadcasted_iota(jnp.int32, (L, L), 1)
    masks = (ri <= ci, ri >= ci)
    npair = ML_HEADS // 2
    for u in range(cps):
        cs = slice(u * L, (u + 1) * L)
        gt = gt_ref[:, cs]
        cumt = (_cum_rows(gt, False), _cum_rows(gt, True))
        r_rows = jnp.concatenate([gt[0:ML_HEADS] - cumt[0][2 * ML_HEADS:3 * ML_HEADS],
                                  gt[ML_HEADS:2 * ML_HEADS] - cumt[1][3 * ML_HEADS:4 * ML_HEADS],
                                  jnp.zeros((LANES - 2 * ML_HEADS, L), F32)], axis=0)
        r_cols = r_rows.T
        prods = []
        for p in range(npair):
            sl = slice(p * LANES, (p + 1) * LANES)
            stack = jnp.concatenate([k_ref[cs, sl], s_ref[p].astype(BF16), sb_ref[0, u, p].astype(BF16)], axis=0)
            prods.append(_dot_nt(stack, _pair_rows(q_ref[cs, sl], lo)))
        pts, inters, mqs = [], [], []
        for h in range(ML_HEADS):
            qs = slice((h % 2) * L, (h % 2 + 1) * L)
            for d in range(2):
                ig_row, b_row, _ = _gate_rows(gt, cumt[d], h, d == 1)
                r_col = r_cols[:, d * ML_HEADS + h:d * ML_HEADS + h + 1]
                m = mb_ref[0, u, h:h + 1, 0:1] if d else m_ref[h:h + 1, 0:1]
                g_row = b_row + m
                dt = jnp.where(masks[d], b_row + r_col, NEG_INF)
                m_q = jnp.maximum(g_row, jnp.max(dt, axis=0, keepdims=True))
                pts.append((jnp.exp(dt - m_q) * prods[h // 2][0:L, qs]).astype(BF16))
                inters.append(jnp.exp(g_row - m_q))
                mqs.append(m_q)
        nums = [_dot(vt_ref[h * VT_ROWS:(h + 1) * VT_ROWS, cs], jnp.concatenate(pts[2 * h:2 * h + 2], axis=1))
                for h in range(ML_HEADS)]
        for h in range(ML_HEADS):
            qs = slice((h % 2) * L, (h % 2 + 1) * L)
            hsum = None
            for d in range(2):
                i = 2 * h + d
                tot = (inters[i] * prods[h // 2][L + d * VT_ROWS:L + (d + 1) * VT_ROWS, qs]
                       + nums[h][:, d * L:(d + 1) * L])
                den = tot[ML_V_DIM:ML_V_DIM + 1]
                hd = tot[0:ML_V_DIM] / jnp.maximum(jnp.abs(den), jnp.exp(-mqs[i]))
                hsum = hd if hsum is None else hsum + hd
            hs = slice(h * ML_V_DIM, (h + 1) * ML_V_DIM)
            ms = jnp.mean(hsum * hsum, axis=0, keepdims=True)
            y = hsum * lax.rsqrt(ms + EPS) * ngb_ref[hs, :]
            o_ref[cs, hs] = (y.T.astype(BF16) * sg_ref[cs, hs]).astype(o_ref.dtype)
        for p in range(npair):
            hp = (2 * p, 2 * p + 1)
            rows = [_gate_rows(gt, cumt[0], h, False) for h in hp]
            ms = [m_ref[h:h + 1, 0:1] for h in hp]
            vts = [vt_ref[h * VT_ROWS:(h + 1) * VT_ROWS, cs] for h in hp]
            s_new, m_new = _state_step(s_ref[p], k_ref[cs, p * LANES:(p + 1) * LANES], vts, rows, ms, lo)
            s_ref[p] = s_new
            for e in range(2):
                m_ref[hp[e]:hp[e] + 1, :] = jnp.broadcast_to(m_new[e], (1, LANES))


def _ml_out(q, k, vt, gt, sg, s_bwd, m_bwd, s0, m0, norm_gb, batch, t, cps):
    L = ML_CHUNK
    nc = t // L
    assert t % L == 0 and nc % cps == 0
    steps = nc // cps
    np_ = ML_HEADS // 2
    tok = lambda cols: pl.BlockSpec((cps * L, cols), lambda b, s: (b * steps + s, 0))
    return pl.pallas_call(
        functools.partial(_ml_out_kernel, cps=cps),
        grid=(batch, steps),
        in_specs=[tok(D_CQK), tok(D_CQK),
                  pl.BlockSpec((ML_HEADS * VT_ROWS, cps * L), lambda b, s: (b, s)),
                  pl.BlockSpec((LANES, cps * L), lambda b, s: (b, s)), tok(D_CV),
                  pl.BlockSpec((1, cps, np_, VT_ROWS, LANES), lambda b, s: (b, s, 0, 0, 0)),
                  pl.BlockSpec((1, cps, ML_HEADS, LANES), lambda b, s: (b, s, 0, 0)),
                  pl.BlockSpec((1, np_, VT_ROWS, LANES), lambda b, s: (b, 0, 0, 0)),
                  pl.BlockSpec((1, ML_HEADS, LANES), lambda b, s: (b, 0, 0)),
                  _const_spec((D_CV, LANES))],
        out_specs=tok(D_CV),
        out_shape=jax.ShapeDtypeStruct((batch * t, D_CV), BF16),
        scratch_shapes=[pltpu.VMEM((np_, VT_ROWS, LANES), F32), pltpu.VMEM((ML_HEADS, LANES), F32)],
        compiler_params=_cparams(("arbitrary", "arbitrary")),
        name="ml_out",
    )(q, k, vt, gt, sg, s_bwd, m_bwd, s0, m0, norm_gb)


def _even_layer(x2, ctx2, mod3, lat_row, ctx_row, batch, t, n, norm_mix_g, norm_ffn_g, ffn_w, w_in, qg, kg,
                table, cw, cb, lg, lb, w_out, ctx_out):
    tm = min(512, t)
    tmc = min(512, batch * n)
    ql, kl, vl, glu_l = _ab_in(x2, mod3, lat_row(t // tm), norm_mix_g, w_in, qg, kg, tm)
    qc, kc, vc, glu_c = _ab_in(ctx2, mod3, ctx_row, norm_mix_g, w_in, qg, kg, tmc)
    att_l = _na_attention(ql, kl, vl, kc, vc, table, batch, t, n, rps=4)
    conv_l = _conv_module(glu_l, cw, cb, lg, lb, batch, t)
    x2 = _mix_ffn(x2, (att_l, conv_l), (0, 0), mod3, lat_row(t // tm), norm_ffn_g, w_out, *ffn_w, tm)
    if ctx_out:
        att_c = _ctx_attention(qc, kc, vc, batch, n)
        conv_c = _conv_module(glu_c, cw, cb, lg, lb, batch, n)
        ctx2 = _mix_ffn(ctx2, (att_c, conv_c), (0, 0), mod3, ctx_row, norm_ffn_g, w_out, *ffn_w, tmc)
    return x2, ctx2


def _odd_layer(x2, ctx2, mod3, lat_row, ctx_row, batch, t, n, norm_mix_g, norm_ffn_g, ffn_w, w_in, w_vt, cw, cb, gb,
               norm_gb, w_out):
    tm = min(512, t)
    tmc = min(512, n)
    cos_l, sin_l = _rope_tables(t, True)
    cos_c, sin_c = _rope_tables(n, False)
    ql, kl, vtl, sgl, gtl = _ml_in(x2, mod3, lambda b: b, norm_mix_g, w_in, w_vt, cw, cb, gb, cos_l, sin_l,
                                       batch, t, tm)
    _, kc, vtc, _, gtc = _ml_in(ctx2, mod3, lambda b: ctx_row(b), norm_mix_g, w_in, w_vt, cw, cb, gb, cos_c,
                                   sin_c, batch, n, tmc)
    s_zero = jnp.zeros((batch, ML_HEADS // 2, VT_ROWS, LANES), F32)
    m_zero = jnp.zeros((batch, ML_HEADS, LANES), F32)
    _, _, sf, mf = _ml_state_scan(kc, vtc, gtc, s_zero, m_zero, batch, n, False, cps=2)
    _, _, sb, mb = _ml_state_scan(kc, vtc, gtc, s_zero, m_zero, batch, n, True, cps=2)
    s_pre, m_pre, _, _ = _ml_state_scan(kl, vtl, gtl, sb, mb, batch, t, True, cps=4)
    gated = _ml_out(ql, kl, vtl, gtl, sgl, s_pre, m_pre, sf, mf, norm_gb, batch, t, cps=2)
    return _mix_ffn(x2, (gated, gated), (0, 1), mod3, lat_row(t // tm), norm_ffn_g, w_out, *ffn_w, tm)


def kernel(x, c, ctx, c_ctx, ada_w, ada_b, norm_mix_g, norm_ffn_g, ffn_w_gate, ffn_w_up, ffn_w_down, ab_w_in,
           na_q_norm_g, na_k_norm_g, na_rpb, conv_w, conv_b, conv_ln_g, conv_ln_b, ab_w_out, ml_w_in, ml_conv_w,
           ml_conv_b, ml_gate_b, ml_norm_g, ml_w_out):
    batch, t, d = x.shape
    n = ctx.shape[1]
    depth = ada_w.shape[0]
    assert batch + 1 <= MOD_ROWS and depth % 2 == 0, "odd layers are only implemented as the last-layer form"
    s_rows = jnp.zeros((MOD_ROWS, d), F32).at[:batch].set(c).at[batch].set(c_ctx)
    mod = _ada_mod(s_rows, ada_w, ada_b)
    x2 = x.reshape(batch * t, d)
    ctx2 = ctx.reshape(batch * n, d)
    lat_row = lambda tiles: (lambda i: i // tiles)
    ctx_row = lambda *_: batch
    rows = t // GRID_W
    for l in range(depth):
        j = l // 2
        last = l == depth - 1
        mod3 = mod[l].reshape(MOD_ROWS, 1, 6 * d)
        ffn_w = (ffn_w_gate[l].astype(BF16), ffn_w_up[l].astype(BF16), ffn_w_down[l].astype(BF16))
        nmg = norm_mix_g[l].reshape(1, d)
        nfg = norm_ffn_g[l].reshape(1, d)
        if l % 2 == 0:
            qg = (jnp.tile(na_q_norm_g[j], NA_HEADS) * NA_HEAD_DIM ** -0.5).reshape(1, D_A)
            kg = jnp.tile(na_k_norm_g[j], NA_HEADS).reshape(1, D_A)
            x2, ctx2 = _even_layer(x2, ctx2, mod3, lat_row, ctx_row, batch, t, n, nmg, nfg, ffn_w,
                                   ab_w_in[j].astype(BF16), qg, kg, _na_bias_table(na_rpb[j], rows),
                                   conv_w[j], conv_b[j].reshape(1, D_B), conv_ln_g[j].reshape(1, D_B),
                                   conv_ln_b[j].reshape(1, D_B), ab_w_out[j].astype(BF16), not last)
        else:
            assert last
            nqk = 2 * D_CQK
            w = ml_w_in[j]
            w_gates = jnp.pad(w[:, nqk + 2 * D_CV:], ((0, 0), (0, LANES - 4 * ML_HEADS)))
            w_in = jnp.concatenate([w[:, :nqk], w[:, nqk + D_CV:nqk + 2 * D_CV], w_gates], axis=1).astype(BF16)
            w_vt = w[:, nqk:nqk + D_CV].T.astype(BF16)
            gb = jnp.pad(ml_gate_b[j], (0, LANES - 4 * ML_HEADS)).reshape(1, LANES)
            norm_gb = jnp.broadcast_to(ml_norm_g[j][:, None], (D_CV, LANES))
            x2 = _odd_layer(x2, ctx2, mod3, lat_row, ctx_row, batch, t, n, nmg, nfg, ffn_w, w_in, w_vt,
                            ml_conv_w[j], ml_conv_b[j].reshape(1, nqk), gb, norm_gb, ml_w_out[j].astype(BF16))
    return x2.reshape(batch, t, d)
```

```python
import functools
import math

import jax
import jax.numpy as jnp
import numpy as np
from jax import lax
from jax.experimental import pallas as pl
from jax.experimental.pallas import tpu as pltpu

F32 = jnp.float32
BF16 = jnp.bfloat16

EPS = 1e-6
GRID_W = 64
NA_HEADS = 8
NA_HEAD_DIM = 64
D_A = NA_HEADS * NA_HEAD_DIM
NA_WIN_H = 8
NA_WIN_W = 16
D_B = 512
CONV_WIDTH = 31
ML_HEADS = 8
ML_QK_DIM = 64
ML_V_DIM = 128
D_CQK = ML_HEADS * ML_QK_DIM
D_CV = ML_HEADS * ML_V_DIM
ML_SHORT_CONV = 5
ML_CHUNK = 128
ROPE_BASE = 10000.0

LANES = 128
SUBLANES = 8
HALO = 16
VT_ROWS = ML_V_DIM + HALO
MOD_ROWS = 16
VMEM_LIMIT = 56 * 1024 * 1024
NEG_INF = float("-inf")
LOG2E = math.log2(math.e)


def _cparams(sem):
    return pltpu.CompilerParams(dimension_semantics=sem, vmem_limit_bytes=VMEM_LIMIT)


def _const_spec(shape):
    nd = len(shape)
    return pl.BlockSpec(shape, lambda *_: (0,) * nd, pipeline_mode=pl.Buffered(1))


def _sigmoid(x):
    return 1.0 / (1.0 + jnp.exp(-x))


def _silu(x):
    return x * _sigmoid(x)


def _modulate(x, g, shift, scale):
    ms = jnp.mean(x * x, axis=-1, keepdims=True)
    y = x * lax.rsqrt(ms + EPS) * g
    return y * (1.0 + scale) + shift


def _dot(a, b):
    return jnp.dot(a, b, preferred_element_type=F32)


def _dot_nt(a, b):
    return lax.dot_general(a, b, (((1,), (1,)), ((), ())), preferred_element_type=F32)


def _lo_lanes():
    return lax.broadcasted_iota(jnp.int32, (1, LANES), 1) < (LANES // 2)


def _ada_kernel(s_ref, w_ref, b_ref, o_ref):
    s = _silu(s_ref[...]).astype(BF16)
    o_ref[0] = _dot(s, w_ref[0].astype(BF16)) + b_ref[0]


def _ada_mod(s_rows, ada_w, ada_b):
    depth, d, n = ada_w.shape
    tn = n // 4
    return pl.pallas_call(
        _ada_kernel,
        grid=(depth, n // tn),
        in_specs=[pl.BlockSpec((MOD_ROWS, d), lambda l, j: (0, 0)),
                  pl.BlockSpec((1, d, tn), lambda l, j: (l, 0, j)),
                  pl.BlockSpec((1, 1, tn), lambda l, j: (l, 0, j))],
        out_specs=pl.BlockSpec((1, MOD_ROWS, tn), lambda l, j: (l, 0, j)),
        out_shape=jax.ShapeDtypeStruct((depth, MOD_ROWS, n), F32),
        compiler_params=_cparams(("arbitrary", "arbitrary")),
        name="ada_mod",
    )(s_rows, ada_w, ada_b.reshape(depth, 1, n))


def _mod_spec(d, sec, row_fn):
    return pl.BlockSpec((1, 1, d), lambda *idx: (row_fn(*idx), 0, sec))


def _ab_in_kernel(x_ref, sh_ref, sc_ref, g_ref, w_ref, qg_ref, kg_ref, q_ref, k_ref, v_ref, glu_ref):
    h = _modulate(x_ref[...], g_ref[...], sh_ref[0], sc_ref[0]).astype(BF16)
    lo = _lo_lanes()

    def head_norm(r, gain_ref, out_ref):
        for p in range(D_A // LANES):
            sl = slice(p * LANES, (p + 1) * LANES)
            xp = r[:, sl]
            sq = xp * xp
            s_all = jnp.sum(sq, axis=-1, keepdims=True)
            s_lo = jnp.sum(jnp.where(lo, sq, 0.0), axis=-1, keepdims=True)
            ms = jnp.where(lo, s_lo, s_all - s_lo) * (1.0 / NA_HEAD_DIM)
            out_ref[:, sl] = (xp * lax.rsqrt(ms + EPS) * gain_ref[:, sl]).astype(out_ref.dtype)

    head_norm(_dot(h, w_ref[:, 0:D_A]), qg_ref, q_ref)
    head_norm(_dot(h, w_ref[:, D_A:2 * D_A]), kg_ref, k_ref)
    v_ref[...] = _dot(h, w_ref[:, 2 * D_A:3 * D_A]).astype(v_ref.dtype)
    u = _dot(h, w_ref[:, 3 * D_A:3 * D_A + D_B])
    gt = _dot(h, w_ref[:, 3 * D_A + D_B:3 * D_A + 2 * D_B])
    glu_ref[...] = u * _sigmoid(gt)


def _ab_in(x2, mod3, row_fn, norm_g, w_in, qg, kg, tm):
    m, d = x2.shape
    tok = lambda c: pl.BlockSpec((tm, c), lambda i: (i, 0))
    return pl.pallas_call(
        _ab_in_kernel,
        grid=(m // tm,),
        in_specs=[tok(d), _mod_spec(d, 0, row_fn), _mod_spec(d, 1, row_fn), _const_spec((1, d)),
                  _const_spec(w_in.shape), _const_spec((1, D_A)), _const_spec((1, D_A))],
        out_specs=[tok(D_A), tok(D_A), tok(D_A), tok(D_B)],
        out_shape=[jax.ShapeDtypeStruct((m, D_A), BF16), jax.ShapeDtypeStruct((m, D_A), BF16),
                   jax.ShapeDtypeStruct((m, D_A), BF16), jax.ShapeDtypeStruct((m, D_B), F32)],
        compiler_params=_cparams(("arbitrary",)),
        name="ab_in",
    )(x2, mod3, mod3, norm_g, w_in, qg, kg)


def _softmax_pv(scores, values):
    m = functools.reduce(jnp.maximum, [jnp.max(s, axis=-1, keepdims=True) for s in scores])
    ps = [jnp.exp2(s - m) for s in scores]
    l = functools.reduce(jnp.add, [jnp.sum(p, axis=-1, keepdims=True) for p in ps])
    o = functools.reduce(jnp.add, [_dot(p.astype(BF16), v) for p, v in zip(ps, values)])
    return o / l


def _pair_rows(qp, lo):
    zero = jnp.zeros_like(qp)
    return jnp.concatenate([jnp.where(lo, qp, zero), jnp.where(lo, zero, qp)], axis=0)


def _na_kernel(q_ref, k_ref, v_ref, kc_ref, vc_ref, tb_ref, o_ref, *, rows, rps):
    nk = NA_WIN_H * GRID_W
    lo = _lo_lanes()
    npair = D_A // LANES
    sls = [slice(p * LANES, (p + 1) * LANES) for p in range(npair)]
    starts, rhos = [], []
    for i in range(rps):
        r = pl.program_id(1) * rps + i
        r0 = jnp.clip(r - NA_WIN_H // 2, 0, rows - NA_WIN_H)
        starts.append(pl.multiple_of(r0 * GRID_W, GRID_W))
        rhos.append(r0 - r + (NA_WIN_H - 1))
    scores = []
    for i in range(rps):
        for p in range(npair):
            q2 = _pair_rows(q_ref[i * GRID_W:(i + 1) * GRID_W, sls[p]], lo)
            bias = jnp.concatenate([tb_ref[rhos[i] + 2 * a, p] for a in range(NA_WIN_H // 2)], axis=1)
            scores.append((_dot_nt(q2, k_ref[pl.ds(starts[i], nk), sls[p]]) + bias,
                           _dot_nt(q2, kc_ref[:, sls[p]])))
    probs = []
    for s_loc, s_ctx in scores:
        m = jnp.maximum(jnp.max(s_loc, axis=-1, keepdims=True), jnp.max(s_ctx, axis=-1, keepdims=True))
        p_loc = jnp.exp2(s_loc - m)
        p_ctx = jnp.exp2(s_ctx - m)
        l = jnp.sum(p_loc, axis=-1, keepdims=True) + jnp.sum(p_ctx, axis=-1, keepdims=True)
        probs.append((p_loc.astype(BF16), p_ctx.astype(BF16), l))
    for i in range(rps):
        for p in range(npair):
            p_loc, p_ctx, l = probs[i * npair + p]
            o2 = (_dot(p_loc, v_ref[pl.ds(starts[i], nk), sls[p]]) + _dot(p_ctx, vc_ref[:, sls[p]])) / l
            o_ref[i * GRID_W:(i + 1) * GRID_W, sls[p]] = jnp.where(
                lo, o2[0:GRID_W], o2[GRID_W:2 * GRID_W]).astype(o_ref.dtype)


def _na_attention(q, k, v, kc, vc, table, batch, t, n, rps):
    rows = t // GRID_W
    assert rows >= NA_WIN_H and t % GRID_W == 0 and rows % rps == 0
    steps = rows // rps
    return pl.pallas_call(
        functools.partial(_na_kernel, rows=rows, rps=rps),
        grid=(batch, steps),
        in_specs=[pl.BlockSpec((rps * GRID_W, D_A), lambda b, s: (b * steps + s, 0)),
                  pl.BlockSpec((t, D_A), lambda b, s: (b, 0)),
                  pl.BlockSpec((t, D_A), lambda b, s: (b, 0)),
                  pl.BlockSpec((n, D_A), lambda b, s: (b, 0)),
                  pl.BlockSpec((n, D_A), lambda b, s: (b, 0)), _const_spec(table.shape)],
        out_specs=pl.BlockSpec((rps * GRID_W, D_A), lambda b, s: (b * steps + s, 0)),
        out_shape=jax.ShapeDtypeStruct((batch * t, D_A), BF16),
        compiler_params=_cparams(("arbitrary", "arbitrary")),
        name="na_attention",
    )(q, k, v, kc, vc, table)


def _ctx_attn_kernel(q_ref, k_ref, v_ref, o_ref):
    lo = _lo_lanes()
    for p in range(D_A // LANES):
        sl = slice(p * LANES, (p + 1) * LANES)
        qp = q_ref[:, sl]
        kp = k_ref[:, sl]
        vp = v_ref[:, sl]
        outs = []
        for e in range(2):
            qm = jnp.where(lo if e == 0 else jnp.logical_not(lo), qp, jnp.zeros_like(qp))
            outs.append(_softmax_pv([_dot_nt(qm, kp)], [vp]))
        o_ref[:, sl] = jnp.where(lo, outs[0], outs[1]).astype(o_ref.dtype)


def _ctx_attention(q, k, v, batch, n):
    spec = pl.BlockSpec((n, D_A), lambda b: (b, 0))
    return pl.pallas_call(
        _ctx_attn_kernel,
        grid=(batch,),
        in_specs=[spec, spec, spec],
        out_specs=spec,
        out_shape=jax.ShapeDtypeStruct((batch * n, D_A), BF16),
        compiler_params=_cparams(("arbitrary",)),
        name="ctx_attention",
    )(q, k, v)


def _na_bias_table(rpb):
    n_rel = 2 * NA_WIN_W - 1
    lead = GRID_W - NA_WIN_W
    ext = jnp.pad(rpb.astype(F32), ((0, 0), (0, 0), (lead, 2 * GRID_W - 1 - lead - n_rel)), constant_values=NEG_INF)
    band = jnp.stack([ext[:, :, GRID_W - 1 - c:2 * GRID_W - 1 - c] for c in range(GRID_W)], axis=2)
    col = np.arange(GRID_W)
    c0 = np.clip(col - NA_WIN_W // 2, 0, GRID_W - NA_WIN_W)
    inside = (col[None, :] >= c0[:, None]) & (col[None, :] < c0[:, None] + NA_WIN_W)
    band = jnp.where(inside[None, None], band, NEG_INF)
    two = jnp.concatenate([band[:, :-1], band[:, 1:]], axis=-1)
    two = two.reshape(NA_HEADS // 2, 2, 2 * NA_WIN_H - 2, GRID_W, 2 * GRID_W).transpose(2, 0, 1, 3, 4)
    return two.reshape(2 * NA_WIN_H - 2, NA_HEADS // 2, 2 * GRID_W, 2 * GRID_W)


def _conv_kernel(prev_ref, x_ref, next_ref, cw_ref, cb_ref, lg_ref, lb_ref, o_ref, xs_ref, *, tt, nt, rc):
    i = pl.program_id(1)
    xs_ref[0, 0:HALO] = jnp.where(i > 0, prev_ref[...], 0.0)
    xs_ref[0, HALO:HALO + tt] = x_ref[...]
    xs_ref[0, HALO + tt:2 * HALO + tt] = jnp.where(i < nt - 1, next_ref[...], 0.0)
    span = tt + 2 * HALO - SUBLANES
    for s in range(1, SUBLANES):
        xs_ref[s, 0:span] = xs_ref[0, s:s + span]
    first = HALO - CONV_WIDTH // 2

    def chunk(c, carry):
        base = pl.multiple_of(c * rc, rc)
        acc = jnp.broadcast_to(cb_ref[...], (rc, D_B))
        for w in range(CONV_WIDTH):
            s, a = (first + w) % SUBLANES, (first + w) // SUBLANES
            acc = acc + xs_ref[s, pl.ds(base + a * SUBLANES, rc), :] * cw_ref[w:w + 1, :]
        mu = jnp.mean(acc, axis=-1, keepdims=True)
        xc = acc - mu
        var = jnp.mean(xc * xc, axis=-1, keepdims=True)
        y = xc * lax.rsqrt(var + EPS) * lg_ref[...] + lb_ref[...]
        o_ref[pl.ds(base, rc), :] = _silu(y).astype(o_ref.dtype)
        return carry

    lax.fori_loop(0, tt // rc, chunk, 0, unroll=4)


def _conv_module(glu, cw, cb, lg, lb, batch, t):
    tt = min(t, 512)
    nt = t // tt
    hb = tt // HALO
    nhb = t // HALO
    return pl.pallas_call(
        functools.partial(_conv_kernel, tt=tt, nt=nt, rc=32),
        grid=(batch, nt),
        in_specs=[pl.BlockSpec((HALO, D_B), lambda b, i: (b * nhb + jnp.maximum(i * hb - 1, 0), 0)),
                  pl.BlockSpec((tt, D_B), lambda b, i: (b * nt + i, 0)),
                  pl.BlockSpec((HALO, D_B), lambda b, i: (b * nhb + jnp.minimum((i + 1) * hb, nhb - 1), 0)),
                  _const_spec((CONV_WIDTH, D_B)), _const_spec((1, D_B)), _const_spec((1, D_B)),
                  _const_spec((1, D_B))],
        out_specs=pl.BlockSpec((tt, D_B), lambda b, i: (b * nt + i, 0)),
        out_shape=jax.ShapeDtypeStruct((batch * t, D_B), BF16),
        scratch_shapes=[pltpu.VMEM((SUBLANES, tt + 2 * HALO, D_B), F32)],
        compiler_params=_cparams(("arbitrary", "arbitrary")),
        name="conv_module",
    )(glu, glu, glu, cw, cb, lg, lb)


def _mix_ffn_kernel(x_ref, a1_ref, a2_ref, g1_ref, sh_ref, sc_ref, g2_ref, ng_ref,
                    wo_ref, wg_ref, wu_ref, wd_ref, o_ref, acc_ref, *, tf):
    ka = a1_ref.shape[1]
    mix = _dot(a1_ref[...], wo_ref[0:ka, :]) + _dot(a2_ref[...], wo_ref[ka:2 * ka, :])
    x1 = x_ref[...] + g1_ref[0] * mix
    h = _modulate(x1, ng_ref[...], sh_ref[0], sc_ref[0]).astype(BF16)
    d_ff = wg_ref.shape[1]
    for c in range(d_ff // tf):
        sl = slice(c * tf, (c + 1) * tf)
        act = (_silu(_dot(h, wg_ref[:, sl])) * _dot(h, wu_ref[:, sl])).astype(BF16)
        part = _dot(act, wd_ref[sl, :])
        if c == 0:
            acc_ref[...] = part
        else:
            acc_ref[...] += part
    o_ref[...] = x1 + g2_ref[0] * acc_ref[...]


def _mix_ffn(x2, a, a_cols, mod3, row_fn, norm_g, w_out, ffn_w, tm):
    m, d = x2.shape
    a1, a2 = a
    layer, w_gate, w_up, w_down = ffn_w
    ka = w_out.shape[0] // 2
    tok = lambda c, j=0: pl.BlockSpec((tm, c), lambda i: (i, j))
    layer_spec = lambda w: pl.BlockSpec((None,) + w.shape[1:], lambda i: (layer, 0, 0), pipeline_mode=pl.Buffered(1))
    return pl.pallas_call(
        functools.partial(_mix_ffn_kernel, tf=256),
        grid=(m // tm,),
        in_specs=[tok(d), tok(ka, a_cols[0]), tok(ka, a_cols[1]),
                  _mod_spec(d, 2, row_fn), _mod_spec(d, 3, row_fn), _mod_spec(d, 4, row_fn),
                  _mod_spec(d, 5, row_fn), _const_spec((1, d)),
                  _const_spec(w_out.shape), layer_spec(w_gate), layer_spec(w_up), layer_spec(w_down)],
        out_specs=tok(d),
        out_shape=jax.ShapeDtypeStruct((m, d), F32),
        scratch_shapes=[pltpu.VMEM((tm, d), F32)],
        compiler_params=_cparams(("arbitrary",)),
        name="mix_ffn",
    )(x2, a1, a2, mod3, mod3, mod3, mod3, norm_g, w_out, w_gate, w_up, w_down)


def _ml_in_kernel(xp_ref, x_ref, xn_ref, sh_ref, sc_ref, g_ref, w_ref, wvt_ref, cw_ref, cb_ref, gb_ref, cos_ref,
                  sin_ref, *rest, tm, nt, bwd_scan):
    if bwd_scan:
        s0_ref, m0_ref, q_ref, k_ref, vt_ref, sg_ref, gatest_ref, spre_ref, mpre_ref, hs_ref, r_ref, s_ref, m_ref = rest
        i = nt - 1 - pl.program_id(1)
    else:
        q_ref, k_ref, vt_ref, sg_ref, gatest_ref, hs_ref, r_ref = rest
        i = pl.program_id(1)
    g, sh, sc = g_ref[...], sh_ref[0], sc_ref[0]
    hs_ref[0:HALO] = jnp.where(i > 0, _modulate(xp_ref[...], g, sh, sc), 0.0).astype(BF16)
    hs_ref[HALO:HALO + tm] = _modulate(x_ref[...], g, sh, sc).astype(BF16)
    hs_ref[HALO + tm:2 * HALO + tm] = jnp.where(i < nt - 1, _modulate(xn_ref[...], g, sh, sc), 0.0).astype(BF16)
    nqk = 2 * D_CQK
    lane = lax.broadcasted_iota(jnp.int32, (1, LANES), 1)
    r_ref[...] = _dot(hs_ref[...], w_ref[:, 0:nqk])
    hm = hs_ref[HALO:HALO + tm]
    vt = _dot_nt(wvt_ref[...], hm).astype(vt_ref.dtype)
    ones_rows = (lax.broadcasted_iota(jnp.int32, (VT_ROWS - ML_V_DIM, tm), 0) == 0).astype(vt_ref.dtype)
    for h in range(ML_HEADS):
        vt_ref[h * VT_ROWS:h * VT_ROWS + ML_V_DIM, :] = vt[h * ML_V_DIM:(h + 1) * ML_V_DIM]
        vt_ref[h * VT_ROWS + ML_V_DIM:(h + 1) * VT_ROWS, :] = ones_rows
    o_pre = _dot(hm, w_ref[:, nqk:nqk + D_CV])
    gt = _dot(hm, w_ref[:, nqk + D_CV:nqk + D_CV + LANES]) + gb_ref[...]
    sg_ref[...] = _sigmoid(o_pre).astype(sg_ref.dtype)
    log_sig = jnp.minimum(gt, 0.0) - jnp.log(1.0 + jnp.exp(-jnp.abs(gt)))
    gates = jnp.where(lane < 2 * ML_HEADS, gt, log_sig)
    gatest_ref[...] = gates.T
    pad = ML_SHORT_CONV // 2
    rows = tm + 2 * HALO
    qk_parts = []
    for j in range(nqk // LANES):
        rj = r_ref[:, j * LANES:(j + 1) * LANES]
        acc = jnp.broadcast_to(cb_ref[:, j * LANES:(j + 1) * LANES], (tm, LANES))
        for w in range(ML_SHORT_CONV):
            shifted = rj if w == pad else pltpu.roll(rj, (pad - w) % rows, 0)
            acc = acc + shifted[HALO:HALO + tm] * cw_ref[w:w + 1, j * LANES:(j + 1) * LANES]
        qk_parts.append(_silu(acc))
    cos = cos_ref[...]
    sin = sin_ref[...]
    first = (lane & 16) == 0
    for j in range(nqk // LANES):
        xg = qk_parts[j]
        sw = jnp.where(first, pltpu.roll(xg, LANES - 16, 1), pltpu.roll(xg, 16, 1))
        y = xg * cos + sw * sin
        if j < D_CQK // LANES:
            q_ref[:, j * LANES:(j + 1) * LANES] = (y * ML_QK_DIM ** -0.5).astype(q_ref.dtype)
        else:
            jj = j - D_CQK // LANES
            k_ref[:, jj * LANES:(jj + 1) * LANES] = y.astype(k_ref.dtype)
    if bwd_scan:
        @pl.when(pl.program_id(1) == 0)
        def _():
            s_ref[...] = s0_ref[0]
            m_ref[...] = m0_ref[0]

        _absorb_chunks(k_ref, vt_ref, gatest_ref, s_ref, m_ref, spre_ref.at[0], mpre_ref.at[0], True,
                       tm // ML_CHUNK)


def _ml_in(x2, mod3, row_fn, norm_g, w_in, w_vt, cw, cb, gb, cos_t, sin_t, batch, t, tm, bwd_state=None):
    m, d = x2.shape
    nt = t // tm
    hb = tm // HALO
    nhb = t // HALO
    nqk = 2 * D_CQK
    np_ = ML_HEADS // 2
    bwd_scan = bwd_state is not None
    ti = (lambda i: nt - 1 - i) if bwd_scan else (lambda i: i)
    tok = lambda c: pl.BlockSpec((tm, c), lambda b, i: (b * nt + ti(i), 0))
    rf = lambda b, i: row_fn(b)
    in_specs = [pl.BlockSpec((HALO, d), lambda b, i: (b * nhb + jnp.maximum(ti(i) * hb - 1, 0), 0)),
                tok(d),
                pl.BlockSpec((HALO, d), lambda b, i: (b * nhb + jnp.minimum((ti(i) + 1) * hb, nhb - 1), 0)),
                _mod_spec(d, 0, rf), _mod_spec(d, 1, rf), _const_spec((1, d)),
                _const_spec(w_in.shape), _const_spec(w_vt.shape), _const_spec((ML_SHORT_CONV, nqk)),
                _const_spec((1, nqk)), _const_spec((1, LANES)),
                pl.BlockSpec((tm, LANES), lambda b, i: (ti(i), 0)),
                pl.BlockSpec((tm, LANES), lambda b, i: (ti(i), 0))]
    out_specs = [tok(D_CQK), tok(D_CQK), pl.BlockSpec((ML_HEADS * VT_ROWS, tm), lambda b, i: (b, ti(i))),
                 tok(D_CV), pl.BlockSpec((LANES, tm), lambda b, i: (b, ti(i)))]
    out_shape = [jax.ShapeDtypeStruct((m, D_CQK), BF16), jax.ShapeDtypeStruct((m, D_CQK), BF16),
                 jax.ShapeDtypeStruct((batch * ML_HEADS * VT_ROWS, t), BF16),
                 jax.ShapeDtypeStruct((m, D_CV), BF16),
                 jax.ShapeDtypeStruct((batch * LANES, t), F32)]
    scratch = [pltpu.VMEM((tm + 2 * HALO, d), BF16), pltpu.VMEM((tm + 2 * HALO, nqk), F32)]
    args = [x2, x2, x2, mod3, mod3, norm_g, w_in, w_vt, cw, cb, gb, cos_t, sin_t]
    if bwd_scan:
        cpt = tm // ML_CHUNK
        assert tm % ML_CHUNK == 0
        in_specs += [pl.BlockSpec((1, np_, VT_ROWS, LANES), lambda b, i: (b, 0, 0, 0)),
                     pl.BlockSpec((1, ML_HEADS, LANES), lambda b, i: (b, 0, 0))]
        out_specs += [pl.BlockSpec((1, cpt, np_, VT_ROWS, LANES), lambda b, i: (b, ti(i), 0, 0, 0)),
                      pl.BlockSpec((1, cpt, ML_HEADS, LANES), lambda b, i: (b, ti(i), 0, 0))]
        out_shape += [jax.ShapeDtypeStruct((batch, t // ML_CHUNK, np_, VT_ROWS, LANES), F32),
                      jax.ShapeDtypeStruct((batch, t // ML_CHUNK, ML_HEADS, LANES), F32)]
        scratch += [pltpu.VMEM((np_, VT_ROWS, LANES), F32), pltpu.VMEM((ML_HEADS, LANES), F32)]
        args += list(bwd_state)
    return pl.pallas_call(
        functools.partial(_ml_in_kernel, tm=tm, nt=nt, bwd_scan=bwd_scan),
        grid=(batch, nt),
        in_specs=in_specs,
        out_specs=out_specs,
        out_shape=out_shape,
        scratch_shapes=scratch,
        compiler_params=_cparams(("arbitrary", "arbitrary")),
        name="ml_in_scan" if bwd_scan else "ml_in",
    )(*args)


def _rope_tables(t, use_rope):
    if not use_rope:
        return jnp.ones((t, LANES), F32), jnp.zeros((t, LANES), F32)
    in_head = np.arange(LANES) % ML_QK_DIM
    nf = ML_QK_DIM // 4
    inv = ROPE_BASE ** (-np.arange(nf, dtype=np.float64) / nf)
    freq = inv[in_head % nf]
    sign = np.where((in_head // nf) % 2 == 0, -1.0, 1.0)
    tok = np.arange(t)
    pos = np.where((in_head // (2 * nf))[None, :] == 0, (tok // GRID_W)[:, None], (tok % GRID_W)[:, None])
    ang = pos.astype(np.float64) * freq[None, :]
    return jnp.asarray(np.cos(ang), F32), jnp.asarray(np.sin(ang) * sign[None, :], F32)


N_GATES = 4 * ML_HEADS


def _log2_gates(gt_ref, cs):
    return gt_ref[0:N_GATES, cs] * LOG2E


def _cum_rows(x, backward):
    L = x.shape[1]
    hi = x.astype(BF16)
    r1 = x - hi.astype(F32)
    mid = r1.astype(BF16)
    lo = (r1 - mid.astype(F32)).astype(BF16)
    ri = lax.broadcasted_iota(jnp.int32, (L, L), 0)
    ci = lax.broadcasted_iota(jnp.int32, (L, L), 1)
    mat = ((ri >= ci) if backward else (ri <= ci)).astype(F32).astype(BF16)
    out = _dot(jnp.concatenate([hi, mid, lo], axis=0), mat)
    return out[0:N_GATES] + out[N_GATES:2 * N_GATES] + out[2 * N_GATES:3 * N_GATES]


def _gate_rows(gt, cumt, h, backward):
    L = gt.shape[1]
    ig = (ML_HEADS if backward else 0) + h
    lf = (3 * ML_HEADS if backward else 2 * ML_HEADS) + h
    end = 0 if backward else L - 1
    return gt[ig:ig + 1, :], cumt[lf:lf + 1, :], cumt[lf:lf + 1, end:end + 1]


def _chunk_update(k_pair, vts, rows, lo):
    lhs, m_locs, b_ends = [], [], []
    for e in range(2):
        ig_row, b_row, b_end = rows[e]
        w_end = b_end - b_row + ig_row
        m_loc = jnp.max(w_end, axis=1, keepdims=True)
        lhs.append((vts[e].astype(F32) * jnp.exp2(w_end - m_loc)).astype(BF16))
        m_locs.append(m_loc)
        b_ends.append(b_end)
    out = _dot(jnp.concatenate(lhs, axis=0), k_pair)
    return jnp.where(lo, out[0:VT_ROWS], out[VT_ROWS:2 * VT_ROWS]), m_locs, b_ends


def _apply_update(s_pair, upd, m_locs, b_ends, ms, lo):
    decays, gains, m_news = [], [], []
    for e in range(2):
        m_new = jnp.maximum(b_ends[e] + ms[e], m_locs[e])
        decays.append(jnp.exp2(b_ends[e] + ms[e] - m_new))
        gains.append(jnp.exp2(m_locs[e] - m_new))
        m_news.append(m_new)
    return jnp.where(lo, decays[0], decays[1]) * s_pair + jnp.where(lo, gains[0], gains[1]) * upd, m_news


def _absorb_chunks(k_ref, vt_ref, gt_ref, s_ref, m_ref, spre_ref, mpre_ref, backward, cps):
    L = ML_CHUNK
    lo = _lo_lanes()
    order = list(range(cps - 1, -1, -1)) if backward else list(range(cps))
    npair = ML_HEADS // 2
    updates = {}
    for u in order:
        cs = slice(u * L, (u + 1) * L)
        gt = _log2_gates(gt_ref, cs)
        cumt = _cum_rows(gt, backward)
        for p in range(npair):
            rows = [_gate_rows(gt, cumt, h, backward) for h in (2 * p, 2 * p + 1)]
            vts = [vt_ref[h * VT_ROWS:(h + 1) * VT_ROWS, cs] for h in (2 * p, 2 * p + 1)]
            updates[u, p] = _chunk_update(k_ref[cs, p * LANES:(p + 1) * LANES], vts, rows, lo)
    for u in order:
        spre_ref[u] = s_ref[...]
        mpre_ref[u] = m_ref[...]
        for p in range(npair):
            hs = (2 * p, 2 * p + 1)
            ms = [m_ref[h:h + 1, 0:1] for h in hs]
            s_new, m_new = _apply_update(s_ref[p], *updates[u, p], ms, lo)
            s_ref[p] = s_new
            for e in range(2):
                m_ref[hs[e]:hs[e] + 1, :] = jnp.broadcast_to(m_new[e], (1, LANES))


def _ml_state_kernel(k_ref, vt_ref, gt_ref, s0_ref, m0_ref, spre_ref, mpre_ref, sfin_ref, mfin_ref, *,
                     backward, cps):
    @pl.when(pl.program_id(1) == 0)
    def _():
        sfin_ref[...] = s0_ref[...]
        mfin_ref[...] = m0_ref[...]

    _absorb_chunks(k_ref, vt_ref, gt_ref, sfin_ref.at[0], mfin_ref.at[0], spre_ref.at[0], mpre_ref.at[0],
                   backward, cps)


def _ml_state_scan(k, vt, gt, s0, m0, batch, t, backward, cps):
    L = ML_CHUNK
    nc = t // L
    cps = min(cps, nc)
    assert t % L == 0 and nc % cps == 0
    steps = nc // cps
    np_ = ML_HEADS // 2
    si = (lambda s: steps - 1 - s) if backward else (lambda s: s)
    st = pl.BlockSpec((1, np_, VT_ROWS, LANES), lambda b, s: (b, 0, 0, 0))
    mx = pl.BlockSpec((1, ML_HEADS, LANES), lambda b, s: (b, 0, 0))
    return pl.pallas_call(
        functools.partial(_ml_state_kernel, backward=backward, cps=cps),
        grid=(batch, steps),
        in_specs=[pl.BlockSpec((cps * L, D_CQK), lambda b, s: (b * steps + si(s), 0)),
                  pl.BlockSpec((ML_HEADS * VT_ROWS, cps * L), lambda b, s: (b, si(s))),
                  pl.BlockSpec((LANES, cps * L), lambda b, s: (b, si(s))), st, mx],
        out_specs=[pl.BlockSpec((1, cps, np_, VT_ROWS, LANES), lambda b, s: (b, si(s), 0, 0, 0)),
                   pl.BlockSpec((1, cps, ML_HEADS, LANES), lambda b, s: (b, si(s), 0, 0)), st, mx],
        out_shape=[jax.ShapeDtypeStruct((batch, nc, np_, VT_ROWS, LANES), F32),
                   jax.ShapeDtypeStruct((batch, nc, ML_HEADS, LANES), F32),
                   jax.ShapeDtypeStruct((batch, np_, VT_ROWS, LANES), F32),
                   jax.ShapeDtypeStruct((batch, ML_HEADS, LANES), F32)],
        compiler_params=_cparams(("arbitrary", "arbitrary")),
        name="ml_state_bwd" if backward else "ml_state_fwd",
    )(k, vt, gt, s0, m0)


def _ml_out_kernel(q_ref, k_ref, vt_ref, gt_ref, sg_ref, sb_ref, mb_ref, s0_ref, m0_ref, ngb_ref, o_ref,
                   s_ref, m_ref, *, cps):
    @pl.when(pl.program_id(1) == 0)
    def _():
        s_ref[...] = s0_ref[0]
        m_ref[...] = m0_ref[0]

    L = ML_CHUNK
    lo = _lo_lanes()
    ri = lax.broadcasted_iota(jnp.int32, (L, L), 0)
    ci = lax.broadcasted_iota(jnp.int32, (L, L), 1)
    masks = (ri <= ci, ri >= ci)
    npair = ML_HEADS // 2
    for u in range(cps):
        cs = slice(u * L, (u + 1) * L)
        gt = _log2_gates(gt_ref, cs)
        cumt = (_cum_rows(gt, False), _cum_rows(gt, True))
        r_rows = jnp.concatenate([gt[0:ML_HEADS] - cumt[0][2 * ML_HEADS:3 * ML_HEADS],
                                  gt[ML_HEADS:2 * ML_HEADS] - cumt[1][3 * ML_HEADS:4 * ML_HEADS],
                                  jnp.zeros((LANES - 2 * ML_HEADS, L), F32)], axis=0)
        r_cols = r_rows.T
        prods, upds = [], []
        for p in range(npair):
            sl = slice(p * LANES, (p + 1) * LANES)
            stack = jnp.concatenate([k_ref[cs, sl], s_ref[p].astype(BF16), sb_ref[0, u, p].astype(BF16)], axis=0)
            prods.append(_dot_nt(stack, _pair_rows(q_ref[cs, sl], lo)))
            rows = [_gate_rows(gt, cumt[0], h, False) for h in (2 * p, 2 * p + 1)]
            vts = [vt_ref[h * VT_ROWS:(h + 1) * VT_ROWS, cs] for h in (2 * p, 2 * p + 1)]
            upds.append(_chunk_update(k_ref[cs, sl], vts, rows, lo))
        pts, inters, mqs = [], [], []
        for h in range(ML_HEADS):
            qs = slice((h % 2) * L, (h % 2 + 1) * L)
            for d in range(2):
                ig_row, b_row, _ = _gate_rows(gt, cumt[d], h, d == 1)
                r_col = r_cols[:, d * ML_HEADS + h:d * ML_HEADS + h + 1]
                m = mb_ref[0, u, h:h + 1, 0:1] if d else m_ref[h:h + 1, 0:1]
                g_row = b_row + m
                dt = jnp.where(masks[d], b_row + r_col, NEG_INF)
                m_q = jnp.maximum(g_row, jnp.max(dt, axis=0, keepdims=True))
                pts.append((jnp.exp2(dt - m_q) * prods[h // 2][0:L, qs]).astype(BF16))
                inters.append(jnp.exp2(g_row - m_q))
                mqs.append(m_q)
        nums = [_dot(vt_ref[h * VT_ROWS:(h + 1) * VT_ROWS, cs], jnp.concatenate(pts[2 * h:2 * h + 2], axis=1))
                for h in range(ML_HEADS)]
        for h in range(ML_HEADS):
            qs = slice((h % 2) * L, (h % 2 + 1) * L)
            hsum = None
            for d in range(2):
                i = 2 * h + d
                tot = (inters[i] * prods[h // 2][L + d * VT_ROWS:L + (d + 1) * VT_ROWS, qs]
                       + nums[h][:, d * L:(d + 1) * L])
                den = tot[ML_V_DIM:ML_V_DIM + 1]
                hd = tot[0:ML_V_DIM] / jnp.maximum(jnp.abs(den), jnp.exp2(-mqs[i]))
                hsum = hd if hsum is None else hsum + hd
            hs = slice(h * ML_V_DIM, (h + 1) * ML_V_DIM)
            ms = jnp.mean(hsum * hsum, axis=0, keepdims=True)
            y = hsum * lax.rsqrt(ms + EPS) * ngb_ref[hs, :]
            o_ref[cs, hs] = (y.T.astype(BF16) * sg_ref[cs, hs]).astype(o_ref.dtype)
        for p in range(npair):
            hp = (2 * p, 2 * p + 1)
            ms = [m_ref[h:h + 1, 0:1] for h in hp]
            s_new, m_new = _apply_update(s_ref[p], *upds[p], ms, lo)
            s_ref[p] = s_new
            for e in range(2):
                m_ref[hp[e]:hp[e] + 1, :] = jnp.broadcast_to(m_new[e], (1, LANES))


def _ml_out(q, k, vt, gt, sg, s_bwd, m_bwd, s0, m0, norm_gb, batch, t, cps):
    L = ML_CHUNK
    nc = t // L
    assert t % L == 0 and nc % cps == 0
    steps = nc // cps
    np_ = ML_HEADS // 2
    tok = lambda cols: pl.BlockSpec((cps * L, cols), lambda b, s: (b * steps + s, 0))
    return pl.pallas_call(
        functools.partial(_ml_out_kernel, cps=cps),
        grid=(batch, steps),
        in_specs=[tok(D_CQK), tok(D_CQK),
                  pl.BlockSpec((ML_HEADS * VT_ROWS, cps * L), lambda b, s: (b, s)),
                  pl.BlockSpec((LANES, cps * L), lambda b, s: (b, s)), tok(D_CV),
                  pl.BlockSpec((1, cps, np_, VT_ROWS, LANES), lambda b, s: (b, s, 0, 0, 0)),
                  pl.BlockSpec((1, cps, ML_HEADS, LANES), lambda b, s: (b, s, 0, 0)),
                  pl.BlockSpec((1, np_, VT_ROWS, LANES), lambda b, s: (b, 0, 0, 0)),
                  pl.BlockSpec((1, ML_HEADS, LANES), lambda b, s: (b, 0, 0)),
                  _const_spec((D_CV, LANES))],
        out_specs=tok(D_CV),
        out_shape=jax.ShapeDtypeStruct((batch * t, D_CV), BF16),
        scratch_shapes=[pltpu.VMEM((np_, VT_ROWS, LANES), F32), pltpu.VMEM((ML_HEADS, LANES), F32)],
        compiler_params=_cparams(("arbitrary", "arbitrary")),
        name="ml_out",
    )(q, k, vt, gt, sg, s_bwd, m_bwd, s0, m0, norm_gb)


def _even_layer(x2, ctx2, mod3, lat_row, ctx_row, batch, t, n, norm_mix_g, norm_ffn_g, ffn_w, w_in, qg, kg,
                table, cw, cb, lg, lb, w_out, ctx_out):
    tm = min(512, t)
    tmc = min(512, batch * n)
    ql, kl, vl, glu_l = _ab_in(x2, mod3, lat_row(t // tm), norm_mix_g, w_in, qg, kg, tm)
    qc, kc, vc, glu_c = _ab_in(ctx2, mod3, ctx_row, norm_mix_g, w_in, qg, kg, tmc)
    att_l = _na_attention(ql, kl, vl, kc, vc, table, batch, t, n, rps=4)
    conv_l = _conv_module(glu_l, cw, cb, lg, lb, batch, t)
    x2 = _mix_ffn(x2, (att_l, conv_l), (0, 0), mod3, lat_row(t // tm), norm_ffn_g, w_out, ffn_w, tm)
    if ctx_out:
        att_c = _ctx_attention(qc, kc, vc, batch, n)
        conv_c = _conv_module(glu_c, cw, cb, lg, lb, batch, n)
        ctx2 = _mix_ffn(ctx2, (att_c, conv_c), (0, 0), mod3, ctx_row, norm_ffn_g, w_out, ffn_w, tmc)
    return x2, ctx2


def _odd_layer(x2, ctx2, mod3, lat_row, ctx_row, batch, t, n, norm_mix_g, norm_ffn_g, ffn_w, w_in, w_vt, cw, cb, gb,
               norm_gb, w_out):
    tm = min(512, t)
    tmc = min(512, n)
    cos_l, sin_l = _rope_tables(t, True)
    cos_c, sin_c = _rope_tables(n, False)
    _, kc, vtc, _, gtc = _ml_in(ctx2, mod3, lambda b: ctx_row(b), norm_mix_g, w_in, w_vt, cw, cb, gb, cos_c,
                                sin_c, batch, n, tmc)
    s_zero = jnp.zeros((batch, ML_HEADS // 2, VT_ROWS, LANES), F32)
    m_zero = jnp.zeros((batch, ML_HEADS, LANES), F32)
    _, _, sf, mf = _ml_state_scan(kc, vtc, gtc, s_zero, m_zero, batch, n, False, cps=2)
    _, _, sb, mb = _ml_state_scan(kc, vtc, gtc, s_zero, m_zero, batch, n, True, cps=2)
    ql, kl, vtl, sgl, gtl, s_pre, m_pre = _ml_in(x2, mod3, lambda b: b, norm_mix_g, w_in, w_vt, cw, cb, gb, cos_l,
                                                 sin_l, batch, t, tm, bwd_state=(sb, mb))
    gated = _ml_out(ql, kl, vtl, gtl, sgl, s_pre, m_pre, sf, mf, norm_gb, batch, t, cps=2)
    return _mix_ffn(x2, (gated, gated), (0, 1), mod3, lat_row(t // tm), norm_ffn_g, w_out, ffn_w, tm)


def kernel(x, c, ctx, c_ctx, ada_w, ada_b, norm_mix_g, norm_ffn_g, ffn_w_gate, ffn_w_up, ffn_w_down, ab_w_in,
           na_q_norm_g, na_k_norm_g, na_rpb, conv_w, conv_b, conv_ln_g, conv_ln_b, ab_w_out, ml_w_in, ml_conv_w,
           ml_conv_b, ml_gate_b, ml_norm_g, ml_w_out):
    batch, t, d = x.shape
    n = ctx.shape[1]
    depth = ada_w.shape[0]
    assert batch + 1 <= MOD_ROWS and depth % 2 == 0, "odd layers are only implemented as the last-layer form"
    s_rows = jnp.zeros((MOD_ROWS, d), F32).at[:batch].set(c).at[batch].set(c_ctx)
    mod = _ada_mod(s_rows, ada_w, ada_b)
    x2 = x.reshape(batch * t, d)
    ctx2 = ctx.reshape(batch * n, d)
    lat_row = lambda tiles: (lambda i: i // tiles)
    ctx_row = lambda *_: batch
    ffn_bf16 = (ffn_w_gate.astype(BF16), ffn_w_up.astype(BF16), ffn_w_down.astype(BF16))
    for l in range(depth):
        j = l // 2
        last = l == depth - 1
        mod3 = mod[l].reshape(MOD_ROWS, 1, 6 * d)
        ffn_w = (l,) + ffn_bf16
        nmg = norm_mix_g[l].reshape(1, d)
        nfg = norm_ffn_g[l].reshape(1, d)
        if l % 2 == 0:
            qg = (jnp.tile(na_q_norm_g[j], NA_HEADS) * (NA_HEAD_DIM ** -0.5 * LOG2E)).reshape(1, D_A)
            kg = jnp.tile(na_k_norm_g[j], NA_HEADS).reshape(1, D_A)
            x2, ctx2 = _even_layer(x2, ctx2, mod3, lat_row, ctx_row, batch, t, n, nmg, nfg, ffn_w,
                                   ab_w_in[j].astype(BF16), qg, kg, _na_bias_table(na_rpb[j] * LOG2E),
                                   conv_w[j], conv_b[j].reshape(1, D_B), conv_ln_g[j].reshape(1, D_B),
                                   conv_ln_b[j].reshape(1, D_B), ab_w_out[j].astype(BF16), not last)
        else:
            assert last
            nqk = 2 * D_CQK
            w = ml_w_in[j]
            w_gates = jnp.pad(w[:, nqk + 2 * D_CV:], ((0, 0), (0, LANES - 4 * ML_HEADS)))
            w_in = jnp.concatenate([w[:, :nqk], w[:, nqk + D_CV:nqk + 2 * D_CV], w_gates], axis=1).astype(BF16)
            w_vt = w[:, nqk:nqk + D_CV].T.astype(BF16)
            gb = jnp.pad(ml_gate_b[j], (0, LANES - 4 * ML_HEADS)).reshape(1, LANES)
            norm_gb = jnp.broadcast_to(ml_norm_g[j][:, None], (D_CV, LANES))
            x2 = _odd_layer(x2, ctx2, mod3, lat_row, ctx_row, batch, t, n, nmg, nfg, ffn_w, w_in, w_vt,
                            ml_conv_w[j], ml_conv_b[j].reshape(1, nqk), gb, norm_gb, ml_w_out[j].astype(BF16))
    return x2.reshape(batch, t, d)
```

```python
import functools
import math

import jax
import jax.numpy as jnp
import numpy as np
from jax import lax
from jax.experimental import pallas as pl
from jax.experimental.pallas import tpu as pltpu

F32 = jnp.float32
BF16 = jnp.bfloat16

EPS = 1e-6
GRID_W = 64
NA_HEADS = 8
NA_HEAD_DIM = 64
D_A = NA_HEADS * NA_HEAD_DIM
NA_WIN_H = 8
NA_WIN_W = 16
D_B = 512
CONV_WIDTH = 31
ML_HEADS = 8
ML_QK_DIM = 64
ML_V_DIM = 128
D_CQK = ML_HEADS * ML_QK_DIM
D_CV = ML_HEADS * ML_V_DIM
ML_SHORT_CONV = 5
ML_CHUNK = 128
ROPE_BASE = 10000.0

LANES = 128
SUBLANES = 8
HALO = 16
VT_ROWS = ML_V_DIM + HALO
MOD_ROWS = 16
VMEM_LIMIT = 56 * 1024 * 1024
NEG_INF = float("-inf")
LOG2E = math.log2(math.e)


def _cparams(sem):
    return pltpu.CompilerParams(dimension_semantics=sem, vmem_limit_bytes=VMEM_LIMIT)


def _const_spec(shape):
    nd = len(shape)
    return pl.BlockSpec(shape, lambda *_: (0,) * nd, pipeline_mode=pl.Buffered(1))


def _sigmoid(x):
    return 1.0 / (1.0 + jnp.exp(-x))


def _silu(x):
    return x * _sigmoid(x)


def _modulate(x, g, shift, scale):
    ms = jnp.mean(x * x, axis=-1, keepdims=True)
    y = x * lax.rsqrt(ms + EPS) * g
    return y * (1.0 + scale) + shift


def _dot(a, b):
    return jnp.dot(a, b, preferred_element_type=F32)


def _dot_nt(a, b):
    return lax.dot_general(a, b, (((1,), (1,)), ((), ())), preferred_element_type=F32)


def _lo_lanes():
    return lax.broadcasted_iota(jnp.int32, (1, LANES), 1) < (LANES // 2)


def _ada_kernel(s_ref, w_ref, b_ref, o_ref):
    s = _silu(s_ref[...]).astype(BF16)
    o_ref[0] = _dot(s, w_ref[0].astype(BF16)) + b_ref[0]


def _ada_mod(s_rows, ada_w, ada_b):
    depth, d, n = ada_w.shape
    tn = n // 4
    return pl.pallas_call(
        _ada_kernel,
        grid=(depth, n // tn),
        in_specs=[pl.BlockSpec((MOD_ROWS, d), lambda l, j: (0, 0)),
                  pl.BlockSpec((1, d, tn), lambda l, j: (l, 0, j)),
                  pl.BlockSpec((1, 1, tn), lambda l, j: (l, 0, j))],
        out_specs=pl.BlockSpec((1, MOD_ROWS, tn), lambda l, j: (l, 0, j)),
        out_shape=jax.ShapeDtypeStruct((depth, MOD_ROWS, n), F32),
        compiler_params=_cparams(("arbitrary", "arbitrary")),
        name="ada_mod",
    )(s_rows, ada_w, ada_b.reshape(depth, 1, n))


def _mod_spec(d, sec, row_fn):
    return pl.BlockSpec((1, 1, d), lambda *idx: (row_fn(*idx), 0, sec))


def _conv_chunk(window, base, cw_ref, cb_ref, lg_ref, lb_ref, rc):
    first = HALO - CONV_WIDTH // 2
    groups = rc // SUBLANES
    accs = [cb_ref[...]] * groups
    for w in range(CONV_WIDTH):
        s, a = (first + w) % SUBLANES, (first + w) // SUBLANES
        wv = cw_ref[w]
        for gi in range(groups):
            accs[gi] = accs[gi] + window(s, base + (a + gi) * SUBLANES, SUBLANES) * wv
    acc = jnp.concatenate(accs, axis=0)
    mu = jnp.mean(acc, axis=-1, keepdims=True)
    xc = acc - mu
    var = jnp.mean(xc * xc, axis=-1, keepdims=True)
    rep = lambda ref: jnp.concatenate([ref[...]] * groups, axis=0)
    return _silu(xc * lax.rsqrt(var + EPS) * rep(lg_ref) + rep(lb_ref))


def _head_norm(r, gain_ref, out_ref, lo):
    for p in range(D_A // LANES):
        sl = slice(p * LANES, (p + 1) * LANES)
        xp = r[:, sl]
        sq = xp * xp
        s_all = jnp.sum(sq, axis=-1, keepdims=True)
        s_lo = jnp.sum(jnp.where(lo, sq, 0.0), axis=-1, keepdims=True)
        ms = jnp.where(lo, s_lo, s_all - s_lo) * (1.0 / NA_HEAD_DIM)
        out_ref[:, sl] = (xp * lax.rsqrt(ms + EPS) * gain_ref[:, sl]).astype(out_ref.dtype)


def _ab_in_kernel(x_ref, sh_ref, sc_ref, g_ref, w_ref, qg_ref, kg_ref, q_ref, k_ref, v_ref, glu_ref):
    h = _modulate(x_ref[...], g_ref[...], sh_ref[0], sc_ref[0]).astype(BF16)
    lo = _lo_lanes()
    _head_norm(_dot(h, w_ref[:, 0:D_A]), qg_ref, q_ref, lo)
    _head_norm(_dot(h, w_ref[:, D_A:2 * D_A]), kg_ref, k_ref, lo)
    v_ref[...] = _dot(h, w_ref[:, 2 * D_A:3 * D_A]).astype(v_ref.dtype)
    u = _dot(h, w_ref[:, 3 * D_A:3 * D_A + D_B])
    gt = _dot(h, w_ref[:, 3 * D_A + D_B:3 * D_A + 2 * D_B])
    glu_ref[...] = u * _sigmoid(gt)


def _ab_in_conv_kernel(x_ref, xn_ref, sh_ref, sc_ref, g_ref, w_ref, qg_ref, kg_ref, cw_ref, cb_ref, lg_ref, lb_ref,
                       q_ref, k_ref, v_ref, conv_ref, win_ref, shift_ref, *, tm, nts, rc):
    i = pl.program_id(0)

    @pl.when(i == 0)
    def _():
        win_ref[...] = jnp.zeros_like(win_ref)

    span = tm + 2 * HALO - SUBLANES
    for s in range(SUBLANES):
        shift_ref[s, 0:span] = win_ref[s:s + span]
    prev_tail = win_ref[tm:tm + HALO]
    window = lambda s, start, size: shift_ref[s, start:start + size, :]
    for c in range(tm // rc):
        conv_ref[c * rc:(c + 1) * rc, :] = _conv_chunk(window, c * rc, cw_ref, cb_ref, lg_ref, lb_ref, rc
                                                       ).astype(conv_ref.dtype)
    g, sh, sc = g_ref[...], sh_ref[0], sc_ref[0]
    h = _modulate(x_ref[...], g, sh, sc).astype(BF16)
    h_ext = jnp.concatenate([h, _modulate(xn_ref[...], g, sh, sc).astype(BF16)], axis=0)
    lo = _lo_lanes()
    _head_norm(_dot(h, w_ref[:, 0:D_A]), qg_ref, q_ref, lo)
    _head_norm(_dot(h, w_ref[:, D_A:2 * D_A]), kg_ref, k_ref, lo)
    v_ref[...] = _dot(h, w_ref[:, 2 * D_A:3 * D_A]).astype(v_ref.dtype)
    u = _dot(h_ext, w_ref[:, 3 * D_A:3 * D_A + D_B])
    gt = _dot(h_ext, w_ref[:, 3 * D_A + D_B:3 * D_A + 2 * D_B])
    glu = u * _sigmoid(gt)
    seq_pos = lax.rem(i, nts)
    win_ref[0:HALO] = jnp.where(seq_pos == 0, 0.0, prev_tail)
    win_ref[HALO:HALO + tm] = glu[0:tm]
    win_ref[HALO + tm:2 * HALO + tm] = jnp.where(seq_pos == nts - 1, 0.0, glu[tm:tm + HALO])


def _ab_in_conv(x2, mod3, norm_g, w_in, qg, kg, cw, cb, lg, lb, t, tm):
    m, d = x2.shape
    ntiles = m // tm
    nts = t // tm
    hb = tm // HALO
    cur = lambda i: jnp.minimum(i, ntiles - 1)
    tok = lambda c: pl.BlockSpec((tm, c), lambda i: (cur(i), 0))
    row = lambda i: cur(i) // nts
    rep = lambda: _const_spec((SUBLANES, D_B))
    return pl.pallas_call(
        functools.partial(_ab_in_conv_kernel, tm=tm, nts=nts, rc=32),
        grid=(ntiles + 1,),
        in_specs=[tok(d),
                  pl.BlockSpec((HALO, d), lambda i: (jnp.minimum((cur(i) + 1) * hb, m // HALO - 1), 0)),
                  _mod_spec(d, 0, row), _mod_spec(d, 1, row), _const_spec((1, d)),
                  _const_spec(w_in.shape), _const_spec((1, D_A)), _const_spec((1, D_A)),
                  _const_spec((CONV_WIDTH, SUBLANES, D_B)), rep(), rep(), rep()],
        out_specs=[tok(D_A), tok(D_A), tok(D_A),
                   pl.BlockSpec((tm, D_B), lambda i: (jnp.maximum(i - 1, 0), 0))],
        out_shape=[jax.ShapeDtypeStruct((m, D_A), BF16), jax.ShapeDtypeStruct((m, D_A), BF16),
                   jax.ShapeDtypeStruct((m, D_A), BF16), jax.ShapeDtypeStruct((m, D_B), BF16)],
        scratch_shapes=[pltpu.VMEM((tm + 2 * HALO, D_B), F32), pltpu.VMEM((SUBLANES, tm + 2 * HALO, D_B), F32)],
        compiler_params=_cparams(("arbitrary",)),
        name="ab_in_conv",
    )(x2, x2, mod3, mod3, norm_g, w_in, qg, kg, cw, cb, lg, lb)


def _ab_in(x2, mod3, row_fn, norm_g, w_in, qg, kg, tm):
    m, d = x2.shape
    tok = lambda c: pl.BlockSpec((tm, c), lambda i: (i, 0))
    return pl.pallas_call(
        _ab_in_kernel,
        grid=(m // tm,),
        in_specs=[tok(d), _mod_spec(d, 0, row_fn), _mod_spec(d, 1, row_fn), _const_spec((1, d)),
                  _const_spec(w_in.shape), _const_spec((1, D_A)), _const_spec((1, D_A))],
        out_specs=[tok(D_A), tok(D_A), tok(D_A), tok(D_B)],
        out_shape=[jax.ShapeDtypeStruct((m, D_A), BF16), jax.ShapeDtypeStruct((m, D_A), BF16),
                   jax.ShapeDtypeStruct((m, D_A), BF16), jax.ShapeDtypeStruct((m, D_B), F32)],
        compiler_params=_cparams(("arbitrary",)),
        name="ab_in",
    )(x2, mod3, mod3, norm_g, w_in, qg, kg)


def _softmax_pv(scores, values):
    m = functools.reduce(jnp.maximum, [jnp.max(s, axis=-1, keepdims=True) for s in scores])
    ps = [jnp.exp2(s - m) for s in scores]
    l = functools.reduce(jnp.add, [jnp.sum(p, axis=-1, keepdims=True) for p in ps])
    o = functools.reduce(jnp.add, [_dot(p.astype(BF16), v) for p, v in zip(ps, values)])
    return o / l


def _pair_rows(qp, lo):
    zero = jnp.zeros_like(qp)
    return jnp.concatenate([jnp.where(lo, qp, zero), jnp.where(lo, zero, qp)], axis=0)


def _na_kernel(q_ref, k_ref, v_ref, kc_ref, vc_ref, tb_ref, o_ref, *, rows, rps):
    nk = NA_WIN_H * GRID_W
    lo = _lo_lanes()
    npair = D_A // LANES
    sls = [slice(p * LANES, (p + 1) * LANES) for p in range(npair)]
    starts, rhos = [], []
    for i in range(rps):
        r = pl.program_id(1) * rps + i
        r0 = jnp.clip(r - NA_WIN_H // 2, 0, rows - NA_WIN_H)
        starts.append(pl.multiple_of(r0 * GRID_W, GRID_W))
        rhos.append(r0 - r + (NA_WIN_H - 1))
    scores = []
    for i in range(rps):
        for p in range(npair):
            q2 = _pair_rows(q_ref[i * GRID_W:(i + 1) * GRID_W, sls[p]], lo)
            bias = jnp.concatenate([tb_ref[rhos[i] + 2 * a, p] for a in range(NA_WIN_H // 2)], axis=1)
            scores.append((_dot_nt(q2, k_ref[pl.ds(starts[i], nk), sls[p]]) + bias,
                           _dot_nt(q2, kc_ref[:, sls[p]])))
    probs = []
    for s_loc, s_ctx in scores:
        m = jnp.maximum(jnp.max(s_loc, axis=-1, keepdims=True), jnp.max(s_ctx, axis=-1, keepdims=True))
        p_loc = jnp.exp2(s_loc - m)
        p_ctx = jnp.exp2(s_ctx - m)
        l = jnp.sum(p_loc, axis=-1, keepdims=True) + jnp.sum(p_ctx, axis=-1, keepdims=True)
        probs.append((p_loc.astype(BF16), p_ctx.astype(BF16), l))
    for i in range(rps):
        for p in range(npair):
            p_loc, p_ctx, l = probs[i * npair + p]
            o2 = (_dot(p_loc, v_ref[pl.ds(starts[i], nk), sls[p]]) + _dot(p_ctx, vc_ref[:, sls[p]])) / l
            o_ref[i * GRID_W:(i + 1) * GRID_W, sls[p]] = jnp.where(
                lo, o2[0:GRID_W], o2[GRID_W:2 * GRID_W]).astype(o_ref.dtype)


def _na_attention(q, k, v, kc, vc, table, batch, t, n, rps):
    rows = t // GRID_W
    assert rows >= NA_WIN_H and t % GRID_W == 0 and rows % rps == 0
    steps = rows // rps
    return pl.pallas_call(
        functools.partial(_na_kernel, rows=rows, rps=rps),
        grid=(batch, steps),
        in_specs=[pl.BlockSpec((rps * GRID_W, D_A), lambda b, s: (b * steps + s, 0)),
                  pl.BlockSpec((t, D_A), lambda b, s: (b, 0)),
                  pl.BlockSpec((t, D_A), lambda b, s: (b, 0)),
                  pl.BlockSpec((n, D_A), lambda b, s: (b, 0)),
                  pl.BlockSpec((n, D_A), lambda b, s: (b, 0)), _const_spec(table.shape)],
        out_specs=pl.BlockSpec((rps * GRID_W, D_A), lambda b, s: (b * steps + s, 0)),
        out_shape=jax.ShapeDtypeStruct((batch * t, D_A), BF16),
        compiler_params=_cparams(("arbitrary", "arbitrary")),
        name="na_attention",
    )(q, k, v, kc, vc, table)


def _ctx_attn_kernel(q_ref, k_ref, v_ref, o_ref):
    lo = _lo_lanes()
    for p in range(D_A // LANES):
        sl = slice(p * LANES, (p + 1) * LANES)
        qp = q_ref[:, sl]
        kp = k_ref[:, sl]
        vp = v_ref[:, sl]
        outs = []
        for e in range(2):
            qm = jnp.where(lo if e == 0 else jnp.logical_not(lo), qp, jnp.zeros_like(qp))
            outs.append(_softmax_pv([_dot_nt(qm, kp)], [vp]))
        o_ref[:, sl] = jnp.where(lo, outs[0], outs[1]).astype(o_ref.dtype)


def _ctx_attention(q, k, v, batch, n):
    spec = pl.BlockSpec((n, D_A), lambda b: (b, 0))
    return pl.pallas_call(
        _ctx_attn_kernel,
        grid=(batch,),
        in_specs=[spec, spec, spec],
        out_specs=spec,
        out_shape=jax.ShapeDtypeStruct((batch * n, D_A), BF16),
        compiler_params=_cparams(("arbitrary",)),
        name="ctx_attention",
    )(q, k, v)


def _na_bias_table(rpb):
    n_rel = 2 * NA_WIN_W - 1
    lead = GRID_W - NA_WIN_W
    ext = jnp.pad(rpb.astype(F32), ((0, 0), (0, 0), (lead, 2 * GRID_W - 1 - lead - n_rel)), constant_values=NEG_INF)
    band = jnp.stack([ext[:, :, GRID_W - 1 - c:2 * GRID_W - 1 - c] for c in range(GRID_W)], axis=2)
    col = np.arange(GRID_W)
    c0 = np.clip(col - NA_WIN_W // 2, 0, GRID_W - NA_WIN_W)
    inside = (col[None, :] >= c0[:, None]) & (col[None, :] < c0[:, None] + NA_WIN_W)
    band = jnp.where(inside[None, None], band, NEG_INF)
    two = jnp.concatenate([band[:, :-1], band[:, 1:]], axis=-1)
    two = two.reshape(NA_HEADS // 2, 2, 2 * NA_WIN_H - 2, GRID_W, 2 * GRID_W).transpose(2, 0, 1, 3, 4)
    return two.reshape(2 * NA_WIN_H - 2, NA_HEADS // 2, 2 * GRID_W, 2 * GRID_W)


def _conv_kernel(prev_ref, x_ref, next_ref, cw_ref, cb_ref, lg_ref, lb_ref, o_ref, xs_ref, *, tt, nt, rc):
    i = pl.program_id(1)
    xs_ref[0, 0:HALO] = jnp.where(i > 0, prev_ref[...], 0.0)
    xs_ref[0, HALO:HALO + tt] = x_ref[...]
    xs_ref[0, HALO + tt:2 * HALO + tt] = jnp.where(i < nt - 1, next_ref[...], 0.0)
    span = tt + 2 * HALO - SUBLANES
    for s in range(1, SUBLANES):
        xs_ref[s, 0:span] = xs_ref[0, s:s + span]
    window = lambda s, start, size: xs_ref[s, pl.ds(start, size), :]

    def chunk(c, carry):
        base = pl.multiple_of(c * rc, rc)
        o_ref[pl.ds(base, rc), :] = _conv_chunk(window, base, cw_ref, cb_ref, lg_ref, lb_ref, rc).astype(o_ref.dtype)
        return carry

    lax.fori_loop(0, tt // rc, chunk, 0, unroll=4)


def _conv_module(glu, cw, cb, lg, lb, batch, t):
    tt = min(t, 512)
    nt = t // tt
    hb = tt // HALO
    nhb = t // HALO
    return pl.pallas_call(
        functools.partial(_conv_kernel, tt=tt, nt=nt, rc=32),
        grid=(batch, nt),
        in_specs=[pl.BlockSpec((HALO, D_B), lambda b, i: (b * nhb + jnp.maximum(i * hb - 1, 0), 0)),
                  pl.BlockSpec((tt, D_B), lambda b, i: (b * nt + i, 0)),
                  pl.BlockSpec((HALO, D_B), lambda b, i: (b * nhb + jnp.minimum((i + 1) * hb, nhb - 1), 0)),
                  _const_spec((CONV_WIDTH, SUBLANES, D_B)), _const_spec((SUBLANES, D_B)),
                  _const_spec((SUBLANES, D_B)), _const_spec((SUBLANES, D_B))],
        out_specs=pl.BlockSpec((tt, D_B), lambda b, i: (b * nt + i, 0)),
        out_shape=jax.ShapeDtypeStruct((batch * t, D_B), BF16),
        scratch_shapes=[pltpu.VMEM((SUBLANES, tt + 2 * HALO, D_B), F32)],
        compiler_params=_cparams(("arbitrary", "arbitrary")),
        name="conv_module",
    )(glu, glu, glu, cw, cb, lg, lb)


def _mix_ffn_kernel(x_ref, a1_ref, a2_ref, g1_ref, sh_ref, sc_ref, g2_ref, ng_ref,
                    wo_ref, wg_ref, wu_ref, wd_ref, o_ref, acc_ref, *, tf):
    ka = a1_ref.shape[1]
    mix = _dot(a1_ref[...], wo_ref[0:ka, :]) + _dot(a2_ref[...], wo_ref[ka:2 * ka, :])
    x1 = x_ref[...] + g1_ref[0] * mix
    h = _modulate(x1, ng_ref[...], sh_ref[0], sc_ref[0]).astype(BF16)
    d_ff = wg_ref.shape[1]
    for c in range(d_ff // tf):
        sl = slice(c * tf, (c + 1) * tf)
        act = (_silu(_dot(h, wg_ref[:, sl])) * _dot(h, wu_ref[:, sl])).astype(BF16)
        part = _dot(act, wd_ref[sl, :])
        if c == 0:
            acc_ref[...] = part
        else:
            acc_ref[...] += part
    o_ref[...] = x1 + g2_ref[0] * acc_ref[...]


def _mix_ffn(x2, a, a_cols, mod3, row_fn, norm_g, w_out, ffn_w, tm):
    m, d = x2.shape
    a1, a2 = a
    layer, w_gate, w_up, w_down = ffn_w
    ka = w_out.shape[0] // 2
    tok = lambda c, j=0: pl.BlockSpec((tm, c), lambda i: (i, j))
    layer_spec = lambda w: pl.BlockSpec((None,) + w.shape[1:], lambda i: (layer, 0, 0), pipeline_mode=pl.Buffered(1))
    return pl.pallas_call(
        functools.partial(_mix_ffn_kernel, tf=256),
        grid=(m // tm,),
        in_specs=[tok(d), tok(ka, a_cols[0]), tok(ka, a_cols[1]),
                  _mod_spec(d, 2, row_fn), _mod_spec(d, 3, row_fn), _mod_spec(d, 4, row_fn),
                  _mod_spec(d, 5, row_fn), _const_spec((1, d)),
                  _const_spec(w_out.shape), layer_spec(w_gate), layer_spec(w_up), layer_spec(w_down)],
        out_specs=tok(d),
        out_shape=jax.ShapeDtypeStruct((m, d), F32),
        scratch_shapes=[pltpu.VMEM((tm, d), F32)],
        compiler_params=_cparams(("arbitrary",)),
        name="mix_ffn",
    )(x2, a1, a2, mod3, mod3, mod3, mod3, norm_g, w_out, w_gate, w_up, w_down)


def _ml_in_kernel(xp_ref, x_ref, xn_ref, sh_ref, sc_ref, g_ref, w_ref, wvt_ref, cw_ref, cb_ref, gb_ref, cos_ref,
                  sin_ref, q_ref, k_ref, vt_ref, sg_ref, gatest_ref, hs_ref, r_ref, *, tm, nt):
    i = pl.program_id(1)
    g, sh, sc = g_ref[...], sh_ref[0], sc_ref[0]
    hs_ref[0:HALO] = jnp.where(i > 0, _modulate(xp_ref[...], g, sh, sc), 0.0).astype(BF16)
    hs_ref[HALO:HALO + tm] = _modulate(x_ref[...], g, sh, sc).astype(BF16)
    hs_ref[HALO + tm:2 * HALO + tm] = jnp.where(i < nt - 1, _modulate(xn_ref[...], g, sh, sc), 0.0).astype(BF16)
    nqk = 2 * D_CQK
    lane = lax.broadcasted_iota(jnp.int32, (1, LANES), 1)
    r_ref[...] = _dot(hs_ref[...], w_ref[:, 0:nqk])
    hm = hs_ref[HALO:HALO + tm]
    vt = _dot_nt(wvt_ref[...], hm).astype(vt_ref.dtype)
    ones_rows = (lax.broadcasted_iota(jnp.int32, (VT_ROWS - ML_V_DIM, tm), 0) == 0).astype(vt_ref.dtype)
    for h in range(ML_HEADS):
        vt_ref[h * VT_ROWS:h * VT_ROWS + ML_V_DIM, :] = vt[h * ML_V_DIM:(h + 1) * ML_V_DIM]
        vt_ref[h * VT_ROWS + ML_V_DIM:(h + 1) * VT_ROWS, :] = ones_rows
    o_pre = _dot(hm, w_ref[:, nqk:nqk + D_CV])
    gt = _dot(hm, w_ref[:, nqk + D_CV:nqk + D_CV + LANES]) + gb_ref[...]
    sg_ref[...] = _sigmoid(o_pre).astype(sg_ref.dtype)
    log_sig = jnp.minimum(gt, 0.0) - jnp.log(1.0 + jnp.exp(-jnp.abs(gt)))
    gates = jnp.where(lane < 2 * ML_HEADS, gt, log_sig)
    gatest_ref[...] = gates.T
    pad = ML_SHORT_CONV // 2
    rows = tm + 2 * HALO
    qk_parts = []
    for j in range(nqk // LANES):
        rj = r_ref[:, j * LANES:(j + 1) * LANES]
        acc = jnp.broadcast_to(cb_ref[:, j * LANES:(j + 1) * LANES], (tm, LANES))
        for w in range(ML_SHORT_CONV):
            shifted = rj if w == pad else pltpu.roll(rj, (pad - w) % rows, 0)
            acc = acc + shifted[HALO:HALO + tm] * cw_ref[w:w + 1, j * LANES:(j + 1) * LANES]
        qk_parts.append(_silu(acc))
    cos = cos_ref[...]
    sin = sin_ref[...]
    first = (lane & 16) == 0
    for j in range(nqk // LANES):
        xg = qk_parts[j]
        sw = jnp.where(first, pltpu.roll(xg, LANES - 16, 1), pltpu.roll(xg, 16, 1))
        y = xg * cos + sw * sin
        if j < D_CQK // LANES:
            q_ref[:, j * LANES:(j + 1) * LANES] = (y * ML_QK_DIM ** -0.5).astype(q_ref.dtype)
        else:
            jj = j - D_CQK // LANES
            k_ref[:, jj * LANES:(jj + 1) * LANES] = y.astype(k_ref.dtype)


def _ml_in(x2, mod3, row_fn, norm_g, w_in, w_vt, cw, cb, gb, cos_t, sin_t, batch, t, tm):
    m, d = x2.shape
    nt = t // tm
    hb = tm // HALO
    nhb = t // HALO
    nqk = 2 * D_CQK
    tok = lambda c: pl.BlockSpec((tm, c), lambda b, i: (b * nt + i, 0))
    rf = lambda b, i: row_fn(b)
    return pl.pallas_call(
        functools.partial(_ml_in_kernel, tm=tm, nt=nt),
        grid=(batch, nt),
        in_specs=[pl.BlockSpec((HALO, d), lambda b, i: (b * nhb + jnp.maximum(i * hb - 1, 0), 0)),
                  tok(d),
                  pl.BlockSpec((HALO, d), lambda b, i: (b * nhb + jnp.minimum((i + 1) * hb, nhb - 1), 0)),
                  _mod_spec(d, 0, rf), _mod_spec(d, 1, rf), _const_spec((1, d)),
                  _const_spec(w_in.shape), _const_spec(w_vt.shape), _const_spec((ML_SHORT_CONV, nqk)),
                  _const_spec((1, nqk)), _const_spec((1, LANES)),
                  pl.BlockSpec((tm, LANES), lambda b, i: (i, 0)),
                  pl.BlockSpec((tm, LANES), lambda b, i: (i, 0))],
        out_specs=[tok(D_CQK), tok(D_CQK), pl.BlockSpec((ML_HEADS * VT_ROWS, tm), lambda b, i: (b, i)),
                   tok(D_CV), pl.BlockSpec((LANES, tm), lambda b, i: (b, i))],
        out_shape=[jax.ShapeDtypeStruct((m, D_CQK), BF16), jax.ShapeDtypeStruct((m, D_CQK), BF16),
                   jax.ShapeDtypeStruct((batch * ML_HEADS * VT_ROWS, t), BF16),
                   jax.ShapeDtypeStruct((m, D_CV), BF16),
                   jax.ShapeDtypeStruct((batch * LANES, t), F32)],
        scratch_shapes=[pltpu.VMEM((tm + 2 * HALO, d), BF16), pltpu.VMEM((tm + 2 * HALO, nqk), F32)],
        compiler_params=_cparams(("arbitrary", "arbitrary")),
        name="ml_in",
    )(x2, x2, x2, mod3, mod3, norm_g, w_in, w_vt, cw, cb, gb, cos_t, sin_t)


def _rope_tables(t, use_rope):
    if not use_rope:
        return jnp.ones((t, LANES), F32), jnp.zeros((t, LANES), F32)
    in_head = np.arange(LANES) % ML_QK_DIM
    nf = ML_QK_DIM // 4
    inv = ROPE_BASE ** (-np.arange(nf, dtype=np.float64) / nf)
    freq = inv[in_head % nf]
    sign = np.where((in_head // nf) % 2 == 0, -1.0, 1.0)
    tok = np.arange(t)
    pos = np.where((in_head // (2 * nf))[None, :] == 0, (tok // GRID_W)[:, None], (tok % GRID_W)[:, None])
    ang = pos.astype(np.float64) * freq[None, :]
    return jnp.asarray(np.cos(ang), F32), jnp.asarray(np.sin(ang) * sign[None, :], F32)


N_GATES = 4 * ML_HEADS


def _log2_gates(gt_ref, cs):
    return gt_ref[0:N_GATES, cs] * LOG2E


def _cum_rows(x, backward):
    L = x.shape[1]
    hi = x.astype(BF16)
    r1 = x - hi.astype(F32)
    mid = r1.astype(BF16)
    lo = (r1 - mid.astype(F32)).astype(BF16)
    ri = lax.broadcasted_iota(jnp.int32, (L, L), 0)
    ci = lax.broadcasted_iota(jnp.int32, (L, L), 1)
    mat = ((ri >= ci) if backward else (ri <= ci)).astype(F32).astype(BF16)
    out = _dot(jnp.concatenate([hi, mid, lo], axis=0), mat)
    return out[0:N_GATES] + out[N_GATES:2 * N_GATES] + out[2 * N_GATES:3 * N_GATES]


def _gate_rows(gt, cumt, h, backward):
    L = gt.shape[1]
    ig = (ML_HEADS if backward else 0) + h
    lf = (3 * ML_HEADS if backward else 2 * ML_HEADS) + h
    end = 0 if backward else L - 1
    return gt[ig:ig + 1, :], cumt[lf:lf + 1, :], cumt[lf:lf + 1, end:end + 1]


def _chunk_update(k_pair, vts, rows, lo):
    lhs, m_locs, b_ends = [], [], []
    for e in range(2):
        ig_row, b_row, b_end = rows[e]
        w_end = b_end - b_row + ig_row
        m_loc = jnp.max(w_end, axis=1, keepdims=True)
        lhs.append((vts[e].astype(F32) * jnp.exp2(w_end - m_loc)).astype(BF16))
        m_locs.append(m_loc)
        b_ends.append(b_end)
    out = _dot(jnp.concatenate(lhs, axis=0), k_pair)
    return jnp.where(lo, out[0:VT_ROWS], out[VT_ROWS:2 * VT_ROWS]), m_locs, b_ends


def _apply_update(s_pair, upd, m_locs, b_ends, ms, lo):
    decays, gains, m_news = [], [], []
    for e in range(2):
        m_new = jnp.maximum(b_ends[e] + ms[e], m_locs[e])
        decays.append(jnp.exp2(b_ends[e] + ms[e] - m_new))
        gains.append(jnp.exp2(m_locs[e] - m_new))
        m_news.append(m_new)
    return jnp.where(lo, decays[0], decays[1]) * s_pair + jnp.where(lo, gains[0], gains[1]) * upd, m_news


def _absorb_chunks(k_ref, vt_ref, gt_ref, s_ref, m_ref, spre_ref, mpre_ref, backward, cps):
    L = ML_CHUNK
    lo = _lo_lanes()
    order = list(range(cps - 1, -1, -1)) if backward else list(range(cps))
    npair = ML_HEADS // 2
    updates = {}
    for u in order:
        cs = slice(u * L, (u + 1) * L)
        gt = _log2_gates(gt_ref, cs)
        cumt = _cum_rows(gt, backward)
        for p in range(npair):
            rows = [_gate_rows(gt, cumt, h, backward) for h in (2 * p, 2 * p + 1)]
            vts = [vt_ref[h * VT_ROWS:(h + 1) * VT_ROWS, cs] for h in (2 * p, 2 * p + 1)]
            updates[u, p] = _chunk_update(k_ref[cs, p * LANES:(p + 1) * LANES], vts, rows, lo)
    for u in order:
        spre_ref[u] = s_ref[...]
        mpre_ref[u] = m_ref[...]
        for p in range(npair):
            hs = (2 * p, 2 * p + 1)
            ms = [m_ref[h:h + 1, 0:1] for h in hs]
            s_new, m_new = _apply_update(s_ref[p], *updates[u, p], ms, lo)
            s_ref[p] = s_new
            for e in range(2):
                m_ref[hs[e]:hs[e] + 1, :] = jnp.broadcast_to(m_new[e], (1, LANES))


def _ml_state_kernel(k_ref, vt_ref, gt_ref, s0_ref, m0_ref, spre_ref, mpre_ref, sfin_ref, mfin_ref, *,
                     backward, cps):
    @pl.when(pl.program_id(1) == 0)
    def _():
        sfin_ref[...] = s0_ref[...]
        mfin_ref[...] = m0_ref[...]

    _absorb_chunks(k_ref, vt_ref, gt_ref, sfin_ref.at[0], mfin_ref.at[0], spre_ref.at[0], mpre_ref.at[0],
                   backward, cps)


def _ml_state_scan(k, vt, gt, s0, m0, batch, t, backward, cps):
    L = ML_CHUNK
    nc = t // L
    cps = min(cps, nc)
    assert t % L == 0 and nc % cps == 0
    steps = nc // cps
    np_ = ML_HEADS // 2
    si = (lambda s: steps - 1 - s) if backward else (lambda s: s)
    st = pl.BlockSpec((1, np_, VT_ROWS, LANES), lambda b, s: (b, 0, 0, 0))
    mx = pl.BlockSpec((1, ML_HEADS, LANES), lambda b, s: (b, 0, 0))
    return pl.pallas_call(
        functools.partial(_ml_state_kernel, backward=backward, cps=cps),
        grid=(batch, steps),
        in_specs=[pl.BlockSpec((cps * L, D_CQK), lambda b, s: (b * steps + si(s), 0)),
                  pl.BlockSpec((ML_HEADS * VT_ROWS, cps * L), lambda b, s: (b, si(s))),
                  pl.BlockSpec((LANES, cps * L), lambda b, s: (b, si(s))), st, mx],
        out_specs=[pl.BlockSpec((1, cps, np_, VT_ROWS, LANES), lambda b, s: (b, si(s), 0, 0, 0)),
                   pl.BlockSpec((1, cps, ML_HEADS, LANES), lambda b, s: (b, si(s), 0, 0)), st, mx],
        out_shape=[jax.ShapeDtypeStruct((batch, nc, np_, VT_ROWS, LANES), F32),
                   jax.ShapeDtypeStruct((batch, nc, ML_HEADS, LANES), F32),
                   jax.ShapeDtypeStruct((batch, np_, VT_ROWS, LANES), F32),
                   jax.ShapeDtypeStruct((batch, ML_HEADS, LANES), F32)],
        compiler_params=_cparams(("arbitrary", "arbitrary")),
        name="ml_state_bwd" if backward else "ml_state_fwd",
    )(k, vt, gt, s0, m0)


def _ml_out_kernel(q_ref, k_ref, vt_ref, gt_ref, sg_ref, sb_ref, mb_ref, s0_ref, m0_ref, ngb_ref, o_ref,
                   s_ref, m_ref, *, cps):
    @pl.when(pl.program_id(1) == 0)
    def _():
        s_ref[...] = s0_ref[0]
        m_ref[...] = m0_ref[0]

    L = ML_CHUNK
    lo = _lo_lanes()
    ri = lax.broadcasted_iota(jnp.int32, (L, L), 0)
    ci = lax.broadcasted_iota(jnp.int32, (L, L), 1)
    masks = (ri <= ci, ri >= ci)
    npair = ML_HEADS // 2
    for u in range(cps):
        cs = slice(u * L, (u + 1) * L)
        gt = _log2_gates(gt_ref, cs)
        cumt = (_cum_rows(gt, False), _cum_rows(gt, True))
        r_rows = jnp.concatenate([gt[0:ML_HEADS] - cumt[0][2 * ML_HEADS:3 * ML_HEADS],
                                  gt[ML_HEADS:2 * ML_HEADS] - cumt[1][3 * ML_HEADS:4 * ML_HEADS],
                                  jnp.zeros((LANES - 2 * ML_HEADS, L), F32)], axis=0)
        r_cols = r_rows.T
        prods, upds = [], []
        for p in range(npair):
            sl = slice(p * LANES, (p + 1) * LANES)
            stack = jnp.concatenate([k_ref[cs, sl], s_ref[p].astype(BF16), sb_ref[0, u, p].astype(BF16)], axis=0)
            prods.append(_dot_nt(stack, _pair_rows(q_ref[cs, sl], lo)))
            rows = [_gate_rows(gt, cumt[0], h, False) for h in (2 * p, 2 * p + 1)]
            vts = [vt_ref[h * VT_ROWS:(h + 1) * VT_ROWS, cs] for h in (2 * p, 2 * p + 1)]
            upds.append(_chunk_update(k_ref[cs, sl], vts, rows, lo))
        pts, inters, mqs = [], [], []
        for h in range(ML_HEADS):
            qs = slice((h % 2) * L, (h % 2 + 1) * L)
            for d in range(2):
                ig_row, b_row, _ = _gate_rows(gt, cumt[d], h, d == 1)
                r_col = r_cols[:, d * ML_HEADS + h:d * ML_HEADS + h + 1]
                m = mb_ref[0, u, h:h + 1, 0:1] if d else m_ref[h:h + 1, 0:1]
                g_row = b_row + m
                dt = jnp.where(masks[d], b_row + r_col, NEG_INF)
                m_q = jnp.maximum(g_row, jnp.max(dt, axis=0, keepdims=True))
                pts.append((jnp.exp2(dt - m_q) * prods[h // 2][0:L, qs]).astype(BF16))
                inters.append(jnp.exp2(g_row - m_q))
                mqs.append(m_q)
        nums = [_dot(vt_ref[h * VT_ROWS:(h + 1) * VT_ROWS, cs], jnp.concatenate(pts[2 * h:2 * h + 2], axis=1))
                for h in range(ML_HEADS)]
        for h in range(ML_HEADS):
            qs = slice((h % 2) * L, (h % 2 + 1) * L)
            hsum = None
            for d in range(2):
                i = 2 * h + d
                tot = (inters[i] * prods[h // 2][L + d * VT_ROWS:L + (d + 1) * VT_ROWS, qs]
                       + nums[h][:, d * L:(d + 1) * L])
                den = tot[ML_V_DIM:ML_V_DIM + 1]
                hd = tot[0:ML_V_DIM] / jnp.maximum(jnp.abs(den), jnp.exp2(-mqs[i]))
                hsum = hd if hsum is None else hsum + hd
            hs = slice(h * ML_V_DIM, (h + 1) * ML_V_DIM)
            ms = jnp.mean(hsum * hsum, axis=0, keepdims=True)
            y = hsum * lax.rsqrt(ms + EPS) * ngb_ref[hs, :]
            o_ref[cs, hs] = (y.T.astype(BF16) * sg_ref[cs, hs]).astype(o_ref.dtype)
        for p in range(npair):
            hp = (2 * p, 2 * p + 1)
            ms = [m_ref[h:h + 1, 0:1] for h in hp]
            s_new, m_new = _apply_update(s_ref[p], *upds[p], ms, lo)
            s_ref[p] = s_new
            for e in range(2):
                m_ref[hp[e]:hp[e] + 1, :] = jnp.broadcast_to(m_new[e], (1, LANES))


def _ml_out(q, k, vt, gt, sg, s_bwd, m_bwd, s0, m0, norm_gb, batch, t, cps):
    L = ML_CHUNK
    nc = t // L
    assert t % L == 0 and nc % cps == 0
    steps = nc // cps
    np_ = ML_HEADS // 2
    tok = lambda cols: pl.BlockSpec((cps * L, cols), lambda b, s: (b * steps + s, 0))
    return pl.pallas_call(
        functools.partial(_ml_out_kernel, cps=cps),
        grid=(batch, steps),
        in_specs=[tok(D_CQK), tok(D_CQK),
                  pl.BlockSpec((ML_HEADS * VT_ROWS, cps * L), lambda b, s: (b, s)),
                  pl.BlockSpec((LANES, cps * L), lambda b, s: (b, s)), tok(D_CV),
                  pl.BlockSpec((1, cps, np_, VT_ROWS, LANES), lambda b, s: (b, s, 0, 0, 0)),
                  pl.BlockSpec((1, cps, ML_HEADS, LANES), lambda b, s: (b, s, 0, 0)),
                  pl.BlockSpec((1, np_, VT_ROWS, LANES), lambda b, s: (b, 0, 0, 0)),
                  pl.BlockSpec((1, ML_HEADS, LANES), lambda b, s: (b, 0, 0)),
                  _const_spec((D_CV, LANES))],
        out_specs=tok(D_CV),
        out_shape=jax.ShapeDtypeStruct((batch * t, D_CV), BF16),
        scratch_shapes=[pltpu.VMEM((np_, VT_ROWS, LANES), F32), pltpu.VMEM((ML_HEADS, LANES), F32)],
        compiler_params=_cparams(("arbitrary", "arbitrary")),
        name="ml_out",
    )(q, k, vt, gt, sg, s_bwd, m_bwd, s0, m0, norm_gb)


def _even_layer(x2, ctx2, mod3, lat_row, ctx_row, batch, t, n, norm_mix_g, norm_ffn_g, ffn_w, w_in, qg, kg,
                table, cw, cb, lg, lb, w_out, ctx_out):
    tm = min(512, t)
    tmc = min(512, batch * n)
    ql, kl, vl, conv_l = _ab_in_conv(x2, mod3, norm_mix_g, w_in, qg, kg, cw, cb, lg, lb, t, tm)
    qc, kc, vc, glu_c = _ab_in(ctx2, mod3, ctx_row, norm_mix_g, w_in, qg, kg, tmc)
    att_l = _na_attention(ql, kl, vl, kc, vc, table, batch, t, n, rps=4)
    x2 = _mix_ffn(x2, (att_l, conv_l), (0, 0), mod3, lat_row(t // tm), norm_ffn_g, w_out, ffn_w, tm)
    if ctx_out:
        att_c = _ctx_attention(qc, kc, vc, batch, n)
        conv_c = _conv_module(glu_c, cw, cb, lg, lb, batch, n)
        ctx2 = _mix_ffn(ctx2, (att_c, conv_c), (0, 0), mod3, ctx_row, norm_ffn_g, w_out, ffn_w, tmc)
    return x2, ctx2


def _odd_layer(x2, ctx2, mod3, lat_row, ctx_row, batch, t, n, norm_mix_g, norm_ffn_g, ffn_w, w_in, w_vt, cw, cb, gb,
               norm_gb, w_out):
    tm = min(512, t)
    tmc = min(512, n)
    cos_l, sin_l = _rope_tables(t, True)
    cos_c, sin_c = _rope_tables(n, False)
    _, kc, vtc, _, gtc = _ml_in(ctx2, mod3, lambda b: ctx_row(b), norm_mix_g, w_in, w_vt, cw, cb, gb, cos_c,
                                sin_c, batch, n, tmc)
    s_zero = jnp.zeros((batch, ML_HEADS // 2, VT_ROWS, LANES), F32)
    m_zero = jnp.zeros((batch, ML_HEADS, LANES), F32)
    _, _, sf, mf = _ml_state_scan(kc, vtc, gtc, s_zero, m_zero, batch, n, False, cps=2)
    _, _, sb, mb = _ml_state_scan(kc, vtc, gtc, s_zero, m_zero, batch, n, True, cps=2)
    ql, kl, vtl, sgl, gtl = _ml_in(x2, mod3, lambda b: b, norm_mix_g, w_in, w_vt, cw, cb, gb, cos_l, sin_l,
                                   batch, t, tm)
    s_pre, m_pre, _, _ = _ml_state_scan(kl, vtl, gtl, sb, mb, batch, t, True, cps=8)
    gated = _ml_out(ql, kl, vtl, gtl, sgl, s_pre, m_pre, sf, mf, norm_gb, batch, t, cps=2)
    return _mix_ffn(x2, (gated, gated), (0, 1), mod3, lat_row(t // tm), norm_ffn_g, w_out, ffn_w, tm)


def kernel(x, c, ctx, c_ctx, ada_w, ada_b, norm_mix_g, norm_ffn_g, ffn_w_gate, ffn_w_up, ffn_w_down, ab_w_in,
           na_q_norm_g, na_k_norm_g, na_rpb, conv_w, conv_b, conv_ln_g, conv_ln_b, ab_w_out, ml_w_in, ml_conv_w,
           ml_conv_b, ml_gate_b, ml_norm_g, ml_w_out):
    batch, t, d = x.shape
    n = ctx.shape[1]
    depth = ada_w.shape[0]
    assert batch + 1 <= MOD_ROWS and depth % 2 == 0, "odd layers are only implemented as the last-layer form"
    s_rows = jnp.zeros((MOD_ROWS, d), F32).at[:batch].set(c).at[batch].set(c_ctx)
    mod = _ada_mod(s_rows, ada_w, ada_b)
    x2 = x.reshape(batch * t, d)
    ctx2 = ctx.reshape(batch * n, d)
    lat_row = lambda tiles: (lambda i: i // tiles)
    ctx_row = lambda *_: batch
    ffn_bf16 = (ffn_w_gate.astype(BF16), ffn_w_up.astype(BF16), ffn_w_down.astype(BF16))
    rep8 = lambda v: jnp.broadcast_to(v[None, :], (SUBLANES, v.shape[0]))
    for l in range(depth):
        j = l // 2
        last = l == depth - 1
        mod3 = mod[l].reshape(MOD_ROWS, 1, 6 * d)
        ffn_w = (l,) + ffn_bf16
        nmg = norm_mix_g[l].reshape(1, d)
        nfg = norm_ffn_g[l].reshape(1, d)
        if l % 2 == 0:
            qg = (jnp.tile(na_q_norm_g[j], NA_HEADS) * (NA_HEAD_DIM ** -0.5 * LOG2E)).reshape(1, D_A)
            kg = jnp.tile(na_k_norm_g[j], NA_HEADS).reshape(1, D_A)
            x2, ctx2 = _even_layer(x2, ctx2, mod3, lat_row, ctx_row, batch, t, n, nmg, nfg, ffn_w,
                                   ab_w_in[j].astype(BF16), qg, kg, _na_bias_table(na_rpb[j] * LOG2E),
                                   jnp.broadcast_to(conv_w[j][:, None, :], (CONV_WIDTH, SUBLANES, D_B)),
                                   rep8(conv_b[j]), rep8(conv_ln_g[j]), rep8(conv_ln_b[j]),
                                   ab_w_out[j].astype(BF16), not last)
        else:
            assert last
            nqk = 2 * D_CQK
            w = ml_w_in[j]
            w_gates = jnp.pad(w[:, nqk + 2 * D_CV:], ((0, 0), (0, LANES - 4 * ML_HEADS)))
            w_in = jnp.concatenate([w[:, :nqk], w[:, nqk + D_CV:nqk + 2 * D_CV], w_gates], axis=1).astype(BF16)
            w_vt = w[:, nqk:nqk + D_CV].T.astype(BF16)
            gb = jnp.pad(ml_gate_b[j], (0, LANES - 4 * ML_HEADS)).reshape(1, LANES)
            norm_gb = jnp.broadcast_to(ml_norm_g[j][:, None], (D_CV, LANES))
            x2 = _odd_layer(x2, ctx2, mod3, lat_row, ctx_row, batch, t, n, nmg, nfg, ffn_w, w_in, w_vt,
                            ml_conv_w[j], ml_conv_b[j].reshape(1, nqk), gb, norm_gb, ml_w_out[j].astype(BF16))
    return x2.reshape(batch, t, d)
```

```python
import functools
import math

import jax
import jax.numpy as jnp
import numpy as np
from jax import lax
from jax.experimental import pallas as pl
from jax.experimental.pallas import tpu as pltpu

F32 = jnp.float32
BF16 = jnp.bfloat16

EPS = 1e-6
GRID_W = 64
NA_HEADS = 8
NA_HEAD_DIM = 64
D_A = NA_HEADS * NA_HEAD_DIM
NA_WIN_H = 8
NA_WIN_W = 16
D_B = 512
CONV_WIDTH = 31
ML_HEADS = 8
ML_QK_DIM = 64
ML_V_DIM = 128
D_CQK = ML_HEADS * ML_QK_DIM
D_CV = ML_HEADS * ML_V_DIM
ML_SHORT_CONV = 5
ML_CHUNK = 128
ROPE_BASE = 10000.0

LANES = 128
SUBLANES = 8
HALO = 16
VT_ROWS = ML_V_DIM + HALO
MOD_ROWS = 16
VMEM_LIMIT = 56 * 1024 * 1024
NEG_INF = float("-inf")
LOG2E = math.log2(math.e)


def _cparams(sem):
    return pltpu.CompilerParams(dimension_semantics=sem, vmem_limit_bytes=VMEM_LIMIT)


def _const_spec(shape):
    nd = len(shape)
    return pl.BlockSpec(shape, lambda *_: (0,) * nd, pipeline_mode=pl.Buffered(1))


def _sigmoid(x):
    return 1.0 / (1.0 + jnp.exp(-x))


def _silu(x):
    return x * _sigmoid(x)


def _modulate(x, g, shift, scale):
    ms = jnp.mean(x * x, axis=-1, keepdims=True)
    y = x * lax.rsqrt(ms + EPS) * g
    return y * (1.0 + scale) + shift


def _dot(a, b):
    return jnp.dot(a, b, preferred_element_type=F32)


def _dot_nt(a, b):
    return lax.dot_general(a, b, (((1,), (1,)), ((), ())), preferred_element_type=F32)


def _lo_lanes():
    return lax.broadcasted_iota(jnp.int32, (1, LANES), 1) < (LANES // 2)


def _ada_kernel(s_ref, w_ref, b_ref, o_ref):
    s = _silu(s_ref[...]).astype(BF16)
    o_ref[0] = _dot(s, w_ref[0].astype(BF16)) + b_ref[0]


def _ada_mod(s_rows, ada_w, ada_b):
    depth, d, n = ada_w.shape
    tn = n // 4
    return pl.pallas_call(
        _ada_kernel,
        grid=(depth, n // tn),
        in_specs=[pl.BlockSpec((MOD_ROWS, d), lambda l, j: (0, 0)),
                  pl.BlockSpec((1, d, tn), lambda l, j: (l, 0, j)),
                  pl.BlockSpec((1, 1, tn), lambda l, j: (l, 0, j))],
        out_specs=pl.BlockSpec((1, MOD_ROWS, tn), lambda l, j: (l, 0, j)),
        out_shape=jax.ShapeDtypeStruct((depth, MOD_ROWS, n), F32),
        compiler_params=_cparams(("arbitrary", "arbitrary")),
        name="ada_mod",
    )(s_rows, ada_w, ada_b.reshape(depth, 1, n))


def _mod_spec(d, sec, row_fn):
    return pl.BlockSpec((1, 1, d), lambda *idx: (row_fn(*idx), 0, sec))


def _conv_chunk(window, base, cw_ref, cb_ref, lg_ref, lb_ref, rc):
    first = HALO - CONV_WIDTH // 2
    groups = rc // SUBLANES
    accs = [cb_ref[...]] * groups
    for w in range(CONV_WIDTH):
        s, a = (first + w) % SUBLANES, (first + w) // SUBLANES
        wv = cw_ref[w]
        for gi in range(groups):
            accs[gi] = accs[gi] + window(s, base + (a + gi) * SUBLANES, SUBLANES) * wv
    acc = jnp.concatenate(accs, axis=0)
    mu = jnp.mean(acc, axis=-1, keepdims=True)
    xc = acc - mu
    var = jnp.mean(xc * xc, axis=-1, keepdims=True)
    rep = lambda ref: jnp.concatenate([ref[...]] * groups, axis=0)
    return _silu(xc * lax.rsqrt(var + EPS) * rep(lg_ref) + rep(lb_ref))


def _head_norm(r, gain_ref, out_ref, lo):
    for p in range(D_A // LANES):
        sl = slice(p * LANES, (p + 1) * LANES)
        xp = r[:, sl]
        sq = xp * xp
        s_all = jnp.sum(sq, axis=-1, keepdims=True)
        s_lo = jnp.sum(jnp.where(lo, sq, 0.0), axis=-1, keepdims=True)
        ms = jnp.where(lo, s_lo, s_all - s_lo) * (1.0 / NA_HEAD_DIM)
        out_ref[:, sl] = (xp * lax.rsqrt(ms + EPS) * gain_ref[:, sl]).astype(out_ref.dtype)


def _ab_in_kernel(x_ref, sh_ref, sc_ref, g_ref, w_ref, qg_ref, kg_ref, q_ref, k_ref, v_ref, glu_ref):
    h = _modulate(x_ref[...], g_ref[...], sh_ref[0], sc_ref[0]).astype(BF16)
    lo = _lo_lanes()
    _head_norm(_dot(h, w_ref[:, 0:D_A]), qg_ref, q_ref, lo)
    _head_norm(_dot(h, w_ref[:, D_A:2 * D_A]), kg_ref, k_ref, lo)
    v_ref[...] = _dot(h, w_ref[:, 2 * D_A:3 * D_A]).astype(v_ref.dtype)
    u = _dot(h, w_ref[:, 3 * D_A:3 * D_A + D_B])
    gt = _dot(h, w_ref[:, 3 * D_A + D_B:3 * D_A + 2 * D_B])
    glu_ref[...] = u * _sigmoid(gt)


def _ab_in_conv_kernel(x_ref, xn_ref, sh_ref, sc_ref, g_ref, w_ref, qg_ref, kg_ref, cw_ref, cb_ref, lg_ref, lb_ref,
                       q_ref, k_ref, v_ref, conv_ref, win_ref, shift_ref, *, tm, nts, rc):
    i = pl.program_id(0)

    @pl.when(i == 0)
    def _():
        win_ref[...] = jnp.zeros_like(win_ref)

    span = tm + 2 * HALO - SUBLANES
    for s in range(SUBLANES):
        shift_ref[s, 0:span] = win_ref[s:s + span]
    prev_tail = win_ref[tm:tm + HALO]
    window = lambda s, start, size: shift_ref[s, start:start + size, :]
    for c in range(tm // rc):
        conv_ref[c * rc:(c + 1) * rc, :] = _conv_chunk(window, c * rc, cw_ref, cb_ref, lg_ref, lb_ref, rc
                                                       ).astype(conv_ref.dtype)
    g, sh, sc = g_ref[...], sh_ref[0], sc_ref[0]
    h = _modulate(x_ref[...], g, sh, sc).astype(BF16)
    h_ext = jnp.concatenate([h, _modulate(xn_ref[...], g, sh, sc).astype(BF16)], axis=0)
    lo = _lo_lanes()
    _head_norm(_dot(h, w_ref[:, 0:D_A]), qg_ref, q_ref, lo)
    _head_norm(_dot(h, w_ref[:, D_A:2 * D_A]), kg_ref, k_ref, lo)
    v_ref[...] = _dot(h, w_ref[:, 2 * D_A:3 * D_A]).astype(v_ref.dtype)
    u = _dot(h_ext, w_ref[:, 3 * D_A:3 * D_A + D_B])
    gt = _dot(h_ext, w_ref[:, 3 * D_A + D_B:3 * D_A + 2 * D_B])
    glu = u * _sigmoid(gt)
    seq_pos = lax.rem(i, nts)
    win_ref[0:HALO] = jnp.where(seq_pos == 0, 0.0, prev_tail)
    win_ref[HALO:HALO + tm] = glu[0:tm]
    win_ref[HALO + tm:2 * HALO + tm] = jnp.where(seq_pos == nts - 1, 0.0, glu[tm:tm + HALO])


def _ab_in_conv(x2, mod3, norm_g, w_in, qg, kg, cw, cb, lg, lb, t, tm):
    m, d = x2.shape
    ntiles = m // tm
    nts = t // tm
    hb = tm // HALO
    cur = lambda i: jnp.minimum(i, ntiles - 1)
    tok = lambda c: pl.BlockSpec((tm, c), lambda i: (cur(i), 0))
    row = lambda i: cur(i) // nts
    rep = lambda: _const_spec((SUBLANES, D_B))
    return pl.pallas_call(
        functools.partial(_ab_in_conv_kernel, tm=tm, nts=nts, rc=32),
        grid=(ntiles + 1,),
        in_specs=[tok(d),
                  pl.BlockSpec((HALO, d), lambda i: (jnp.minimum((cur(i) + 1) * hb, m // HALO - 1), 0)),
                  _mod_spec(d, 0, row), _mod_spec(d, 1, row), _const_spec((1, d)),
                  _const_spec(w_in.shape), _const_spec((1, D_A)), _const_spec((1, D_A)),
                  _const_spec((CONV_WIDTH, SUBLANES, D_B)), rep(), rep(), rep()],
        out_specs=[tok(D_A), tok(D_A), tok(D_A),
                   pl.BlockSpec((tm, D_B), lambda i: (jnp.maximum(i - 1, 0), 0))],
        out_shape=[jax.ShapeDtypeStruct((m, D_A), BF16), jax.ShapeDtypeStruct((m, D_A), BF16),
                   jax.ShapeDtypeStruct((m, D_A), BF16), jax.ShapeDtypeStruct((m, D_B), BF16)],
        scratch_shapes=[pltpu.VMEM((tm + 2 * HALO, D_B), F32), pltpu.VMEM((SUBLANES, tm + 2 * HALO, D_B), F32)],
        compiler_params=_cparams(("arbitrary",)),
        name="ab_in_conv",
    )(x2, x2, mod3, mod3, norm_g, w_in, qg, kg, cw, cb, lg, lb)


def _ab_in(x2, mod3, row_fn, norm_g, w_in, qg, kg, tm):
    m, d = x2.shape
    tok = lambda c: pl.BlockSpec((tm, c), lambda i: (i, 0))
    return pl.pallas_call(
        _ab_in_kernel,
        grid=(m // tm,),
        in_specs=[tok(d), _mod_spec(d, 0, row_fn), _mod_spec(d, 1, row_fn), _const_spec((1, d)),
                  _const_spec(w_in.shape), _const_spec((1, D_A)), _const_spec((1, D_A))],
        out_specs=[tok(D_A), tok(D_A), tok(D_A), tok(D_B)],
        out_shape=[jax.ShapeDtypeStruct((m, D_A), BF16), jax.ShapeDtypeStruct((m, D_A), BF16),
                   jax.ShapeDtypeStruct((m, D_A), BF16), jax.ShapeDtypeStruct((m, D_B), F32)],
        compiler_params=_cparams(("arbitrary",)),
        name="ab_in",
    )(x2, mod3, mod3, norm_g, w_in, qg, kg)


def _softmax_pv(scores, values):
    m = functools.reduce(jnp.maximum, [jnp.max(s, axis=-1, keepdims=True) for s in scores])
    ps = [jnp.exp2(s - m) for s in scores]
    l = functools.reduce(jnp.add, [jnp.sum(p, axis=-1, keepdims=True) for p in ps])
    o = functools.reduce(jnp.add, [_dot(p.astype(BF16), v) for p, v in zip(ps, values)])
    return o / l


def _pair_rows(qp, lo):
    zero = jnp.zeros_like(qp)
    return jnp.concatenate([jnp.where(lo, qp, zero), jnp.where(lo, zero, qp)], axis=0)


def _na_kernel(q_ref, k_ref, v_ref, kc_ref, vc_ref, tb_ref, o_ref, *, rows, rps):
    nk = NA_WIN_H * GRID_W
    lo = _lo_lanes()
    npair = D_A // LANES
    sls = [slice(p * LANES, (p + 1) * LANES) for p in range(npair)]
    starts, rhos = [], []
    for i in range(rps):
        r = pl.program_id(1) * rps + i
        r0 = jnp.clip(r - NA_WIN_H // 2, 0, rows - NA_WIN_H)
        starts.append(pl.multiple_of(r0 * GRID_W, GRID_W))
        rhos.append(r0 - r + (NA_WIN_H - 1))
    scores = []
    for i in range(rps):
        for p in range(npair):
            q2 = _pair_rows(q_ref[i * GRID_W:(i + 1) * GRID_W, sls[p]], lo)
            bias = jnp.concatenate([tb_ref[rhos[i] + 2 * a, p] for a in range(NA_WIN_H // 2)], axis=1)
            scores.append((_dot_nt(q2, k_ref[pl.ds(starts[i], nk), sls[p]]) + bias,
                           _dot_nt(q2, kc_ref[:, sls[p]])))
    probs = []
    for s_loc, s_ctx in scores:
        m = jnp.maximum(jnp.max(s_loc, axis=-1, keepdims=True), jnp.max(s_ctx, axis=-1, keepdims=True))
        p_loc = jnp.exp2(s_loc - m)
        p_ctx = jnp.exp2(s_ctx - m)
        l = jnp.sum(p_loc, axis=-1, keepdims=True) + jnp.sum(p_ctx, axis=-1, keepdims=True)
        probs.append((p_loc.astype(BF16), p_ctx.astype(BF16), l))
    for i in range(rps):
        for p in range(npair):
            p_loc, p_ctx, l = probs[i * npair + p]
            o2 = (_dot(p_loc, v_ref[pl.ds(starts[i], nk), sls[p]]) + _dot(p_ctx, vc_ref[:, sls[p]])) / l
            o_ref[i * GRID_W:(i + 1) * GRID_W, sls[p]] = jnp.where(
                lo, o2[0:GRID_W], o2[GRID_W:2 * GRID_W]).astype(o_ref.dtype)


def _na_attention(q, k, v, kc, vc, table, batch, t, n, rps):
    rows = t // GRID_W
    assert rows >= NA_WIN_H and t % GRID_W == 0 and rows % rps == 0
    steps = rows // rps
    return pl.pallas_call(
        functools.partial(_na_kernel, rows=rows, rps=rps),
        grid=(batch, steps),
        in_specs=[pl.BlockSpec((rps * GRID_W, D_A), lambda b, s: (b * steps + s, 0)),
                  pl.BlockSpec((t, D_A), lambda b, s: (b, 0)),
                  pl.BlockSpec((t, D_A), lambda b, s: (b, 0)),
                  pl.BlockSpec((n, D_A), lambda b, s: (b, 0)),
                  pl.BlockSpec((n, D_A), lambda b, s: (b, 0)), _const_spec(table.shape)],
        out_specs=pl.BlockSpec((rps * GRID_W, D_A), lambda b, s: (b * steps + s, 0)),
        out_shape=jax.ShapeDtypeStruct((batch * t, D_A), BF16),
        compiler_params=_cparams(("arbitrary", "arbitrary")),
        name="na_attention",
    )(q, k, v, kc, vc, table)


def _ctx_attn_kernel(q_ref, k_ref, v_ref, o_ref):
    lo = _lo_lanes()
    for p in range(D_A // LANES):
        sl = slice(p * LANES, (p + 1) * LANES)
        qp = q_ref[:, sl]
        kp = k_ref[:, sl]
        vp = v_ref[:, sl]
        outs = []
        for e in range(2):
            qm = jnp.where(lo if e == 0 else jnp.logical_not(lo), qp, jnp.zeros_like(qp))
            outs.append(_softmax_pv([_dot_nt(qm, kp)], [vp]))
        o_ref[:, sl] = jnp.where(lo, outs[0], outs[1]).astype(o_ref.dtype)


def _ctx_attention(q, k, v, batch, n):
    spec = pl.BlockSpec((n, D_A), lambda b: (b, 0))
    return pl.pallas_call(
        _ctx_attn_kernel,
        grid=(batch,),
        in_specs=[spec, spec, spec],
        out_specs=spec,
        out_shape=jax.ShapeDtypeStruct((batch * n, D_A), BF16),
        compiler_params=_cparams(("arbitrary",)),
        name="ctx_attention",
    )(q, k, v)


def _na_bias_table(rpb):
    n_rel = 2 * NA_WIN_W - 1
    lead = GRID_W - NA_WIN_W
    ext = jnp.pad(rpb.astype(F32), ((0, 0), (0, 0), (lead, 2 * GRID_W - 1 - lead - n_rel)), constant_values=NEG_INF)
    band = jnp.stack([ext[:, :, GRID_W - 1 - c:2 * GRID_W - 1 - c] for c in range(GRID_W)], axis=2)
    col = np.arange(GRID_W)
    c0 = np.clip(col - NA_WIN_W // 2, 0, GRID_W - NA_WIN_W)
    inside = (col[None, :] >= c0[:, None]) & (col[None, :] < c0[:, None] + NA_WIN_W)
    band = jnp.where(inside[None, None], band, NEG_INF)
    two = jnp.concatenate([band[:, :-1], band[:, 1:]], axis=-1)
    two = two.reshape(NA_HEADS // 2, 2, 2 * NA_WIN_H - 2, GRID_W, 2 * GRID_W).transpose(2, 0, 1, 3, 4)
    return two.reshape(2 * NA_WIN_H - 2, NA_HEADS // 2, 2 * GRID_W, 2 * GRID_W)


def _conv_kernel(prev_ref, x_ref, next_ref, cw_ref, cb_ref, lg_ref, lb_ref, o_ref, xs_ref, *, tt, nt, rc):
    i = pl.program_id(1)
    xs_ref[0, 0:HALO] = jnp.where(i > 0, prev_ref[...], 0.0)
    xs_ref[0, HALO:HALO + tt] = x_ref[...]
    xs_ref[0, HALO + tt:2 * HALO + tt] = jnp.where(i < nt - 1, next_ref[...], 0.0)
    span = tt + 2 * HALO - SUBLANES
    for s in range(1, SUBLANES):
        xs_ref[s, 0:span] = xs_ref[0, s:s + span]
    window = lambda s, start, size: xs_ref[s, pl.ds(start, size), :]

    def chunk(c, carry):
        base = pl.multiple_of(c * rc, rc)
        o_ref[pl.ds(base, rc), :] = _conv_chunk(window, base, cw_ref, cb_ref, lg_ref, lb_ref, rc).astype(o_ref.dtype)
        return carry

    lax.fori_loop(0, tt // rc, chunk, 0, unroll=4)


def _conv_module(glu, cw, cb, lg, lb, batch, t):
    tt = min(t, 512)
    nt = t // tt
    hb = tt // HALO
    nhb = t // HALO
    return pl.pallas_call(
        functools.partial(_conv_kernel, tt=tt, nt=nt, rc=32),
        grid=(batch, nt),
        in_specs=[pl.BlockSpec((HALO, D_B), lambda b, i: (b * nhb + jnp.maximum(i * hb - 1, 0), 0)),
                  pl.BlockSpec((tt, D_B), lambda b, i: (b * nt + i, 0)),
                  pl.BlockSpec((HALO, D_B), lambda b, i: (b * nhb + jnp.minimum((i + 1) * hb, nhb - 1), 0)),
                  _const_spec((CONV_WIDTH, SUBLANES, D_B)), _const_spec((SUBLANES, D_B)),
                  _const_spec((SUBLANES, D_B)), _const_spec((SUBLANES, D_B))],
        out_specs=pl.BlockSpec((tt, D_B), lambda b, i: (b * nt + i, 0)),
        out_shape=jax.ShapeDtypeStruct((batch * t, D_B), BF16),
        scratch_shapes=[pltpu.VMEM((SUBLANES, tt + 2 * HALO, D_B), F32)],
        compiler_params=_cparams(("arbitrary", "arbitrary")),
        name="conv_module",
    )(glu, glu, glu, cw, cb, lg, lb)


def _mix_ffn_kernel(x_ref, a1_ref, a2_ref, g1_ref, sh_ref, sc_ref, g2_ref, ng_ref,
                    wo_ref, wg_ref, wu_ref, wd_ref, o_ref, acc_ref, *, tf):
    ka = a1_ref.shape[1]
    mix = _dot(a1_ref[...], wo_ref[0:ka, :]) + _dot(a2_ref[...], wo_ref[ka:2 * ka, :])
    x1 = x_ref[...] + g1_ref[0] * mix
    h = _modulate(x1, ng_ref[...], sh_ref[0], sc_ref[0]).astype(BF16)
    d_ff = wg_ref.shape[1]
    for c in range(d_ff // tf):
        sl = slice(c * tf, (c + 1) * tf)
        act = (_silu(_dot(h, wg_ref[:, sl])) * _dot(h, wu_ref[:, sl])).astype(BF16)
        part = _dot(act, wd_ref[sl, :])
        if c == 0:
            acc_ref[...] = part
        else:
            acc_ref[...] += part
    o_ref[...] = x1 + g2_ref[0] * acc_ref[...]


def _mix_ffn(x2, a, a_cols, mod3, row_fn, norm_g, w_out, ffn_w, tm):
    m, d = x2.shape
    a1, a2 = a
    layer, w_gate, w_up, w_down = ffn_w
    ka = w_out.shape[0] // 2
    tok = lambda c, j=0: pl.BlockSpec((tm, c), lambda i: (i, j))
    layer_spec = lambda w: pl.BlockSpec((None,) + w.shape[1:], lambda i: (layer, 0, 0), pipeline_mode=pl.Buffered(1))
    return pl.pallas_call(
        functools.partial(_mix_ffn_kernel, tf=256),
        grid=(m // tm,),
        in_specs=[tok(d), tok(ka, a_cols[0]), tok(ka, a_cols[1]),
                  _mod_spec(d, 2, row_fn), _mod_spec(d, 3, row_fn), _mod_spec(d, 4, row_fn),
                  _mod_spec(d, 5, row_fn), _const_spec((1, d)),
                  _const_spec(w_out.shape), layer_spec(w_gate), layer_spec(w_up), layer_spec(w_down)],
        out_specs=tok(d),
        out_shape=jax.ShapeDtypeStruct((m, d), F32),
        scratch_shapes=[pltpu.VMEM((tm, d), F32)],
        compiler_params=_cparams(("arbitrary",)),
        name="mix_ffn",
    )(x2, a1, a2, mod3, mod3, mod3, mod3, norm_g, w_out, w_gate, w_up, w_down)


def _ml_in_kernel(xp_ref, x_ref, xn_ref, sh_ref, sc_ref, g_ref, w_ref, wvt_ref, cw_ref, cb_ref, gb_ref, cos_ref,
                  sin_ref, q_ref, k_ref, vt_ref, sg_ref, gatest_ref, hs_ref, r_ref, *, tm, nt):
    i = pl.program_id(1)
    g, sh, sc = g_ref[...], sh_ref[0], sc_ref[0]
    hs_ref[0:HALO] = jnp.where(i > 0, _modulate(xp_ref[...], g, sh, sc), 0.0).astype(BF16)
    hs_ref[HALO:HALO + tm] = _modulate(x_ref[...], g, sh, sc).astype(BF16)
    hs_ref[HALO + tm:2 * HALO + tm] = jnp.where(i < nt - 1, _modulate(xn_ref[...], g, sh, sc), 0.0).astype(BF16)
    nqk = 2 * D_CQK
    lane = lax.broadcasted_iota(jnp.int32, (1, LANES), 1)
    r_ref[...] = _dot(hs_ref[...], w_ref[:, 0:nqk])
    hm = hs_ref[HALO:HALO + tm]
    vt = _dot_nt(wvt_ref[...], hm).astype(vt_ref.dtype)
    ones_rows = (lax.broadcasted_iota(jnp.int32, (VT_ROWS - ML_V_DIM, tm), 0) == 0).astype(vt_ref.dtype)
    for h in range(ML_HEADS):
        vt_ref[h * VT_ROWS:h * VT_ROWS + ML_V_DIM, :] = vt[h * ML_V_DIM:(h + 1) * ML_V_DIM]
        vt_ref[h * VT_ROWS + ML_V_DIM:(h + 1) * VT_ROWS, :] = ones_rows
    o_pre = _dot(hm, w_ref[:, nqk:nqk + D_CV])
    gt = _dot(hm, w_ref[:, nqk + D_CV:nqk + D_CV + LANES]) + gb_ref[...]
    sg_ref[...] = _sigmoid(o_pre).astype(sg_ref.dtype)
    log_sig = jnp.minimum(gt, 0.0) - jnp.log(1.0 + jnp.exp(-jnp.abs(gt)))
    gates = jnp.where(lane < 2 * ML_HEADS, gt, log_sig)
    gatest_ref[...] = gates.T
    pad = ML_SHORT_CONV // 2
    rows = tm + 2 * HALO
    qk_parts = []
    for j in range(nqk // LANES):
        rj = r_ref[:, j * LANES:(j + 1) * LANES]
        acc = jnp.broadcast_to(cb_ref[:, j * LANES:(j + 1) * LANES], (tm, LANES))
        for w in range(ML_SHORT_CONV):
            shifted = rj if w == pad else pltpu.roll(rj, (pad - w) % rows, 0)
            acc = acc + shifted[HALO:HALO + tm] * cw_ref[w:w + 1, j * LANES:(j + 1) * LANES]
        qk_parts.append(_silu(acc))
    cos = cos_ref[...]
    sin = sin_ref[...]
    first = (lane & 16) == 0
    for j in range(nqk // LANES):
        xg = qk_parts[j]
        sw = jnp.where(first, pltpu.roll(xg, LANES - 16, 1), pltpu.roll(xg, 16, 1))
        y = xg * cos + sw * sin
        if j < D_CQK // LANES:
            q_ref[:, j * LANES:(j + 1) * LANES] = (y * ML_QK_DIM ** -0.5).astype(q_ref.dtype)
        else:
            jj = j - D_CQK // LANES
            k_ref[:, jj * LANES:(jj + 1) * LANES] = y.astype(k_ref.dtype)


def _ml_in(x2, mod3, row_fn, norm_g, w_in, w_vt, cw, cb, gb, cos_t, sin_t, batch, t, tm):
    m, d = x2.shape
    nt = t // tm
    hb = tm // HALO
    nhb = t // HALO
    nqk = 2 * D_CQK
    tok = lambda c: pl.BlockSpec((tm, c), lambda b, i: (b * nt + i, 0))
    rf = lambda b, i: row_fn(b)
    return pl.pallas_call(
        functools.partial(_ml_in_kernel, tm=tm, nt=nt),
        grid=(batch, nt),
        in_specs=[pl.BlockSpec((HALO, d), lambda b, i: (b * nhb + jnp.maximum(i * hb - 1, 0), 0)),
                  tok(d),
                  pl.BlockSpec((HALO, d), lambda b, i: (b * nhb + jnp.minimum((i + 1) * hb, nhb - 1), 0)),
                  _mod_spec(d, 0, rf), _mod_spec(d, 1, rf), _const_spec((1, d)),
                  _const_spec(w_in.shape), _const_spec(w_vt.shape), _const_spec((ML_SHORT_CONV, nqk)),
                  _const_spec((1, nqk)), _const_spec((1, LANES)),
                  pl.BlockSpec((tm, LANES), lambda b, i: (i, 0)),
                  pl.BlockSpec((tm, LANES), lambda b, i: (i, 0))],
        out_specs=[tok(D_CQK), tok(D_CQK), pl.BlockSpec((ML_HEADS * VT_ROWS, tm), lambda b, i: (b, i)),
                   tok(D_CV), pl.BlockSpec((LANES, tm), lambda b, i: (b, i))],
        out_shape=[jax.ShapeDtypeStruct((m, D_CQK), BF16), jax.ShapeDtypeStruct((m, D_CQK), BF16),
                   jax.ShapeDtypeStruct((batch * ML_HEADS * VT_ROWS, t), BF16),
                   jax.ShapeDtypeStruct((m, D_CV), BF16),
                   jax.ShapeDtypeStruct((batch * LANES, t), F32)],
        scratch_shapes=[pltpu.VMEM((tm + 2 * HALO, d), BF16), pltpu.VMEM((tm + 2 * HALO, nqk), F32)],
        compiler_params=_cparams(("arbitrary", "arbitrary")),
        name="ml_in",
    )(x2, x2, x2, mod3, mod3, norm_g, w_in, w_vt, cw, cb, gb, cos_t, sin_t)


def _rope_tables(t, use_rope):
    if not use_rope:
        return jnp.ones((t, LANES), F32), jnp.zeros((t, LANES), F32)
    in_head = np.arange(LANES) % ML_QK_DIM
    nf = ML_QK_DIM // 4
    inv = ROPE_BASE ** (-np.arange(nf, dtype=np.float64) / nf)
    freq = inv[in_head % nf]
    sign = np.where((in_head // nf) % 2 == 0, -1.0, 1.0)
    tok = np.arange(t)
    pos = np.where((in_head // (2 * nf))[None, :] == 0, (tok // GRID_W)[:, None], (tok % GRID_W)[:, None])
    ang = pos.astype(np.float64) * freq[None, :]
    return jnp.asarray(np.cos(ang), F32), jnp.asarray(np.sin(ang) * sign[None, :], F32)


N_GATES = 4 * ML_HEADS


def _log2_gates(gt_ref, cs):
    return gt_ref[0:N_GATES, cs] * LOG2E


def _cum_rows(x, backward):
    L = x.shape[1]
    hi = x.astype(BF16)
    r1 = x - hi.astype(F32)
    mid = r1.astype(BF16)
    lo = (r1 - mid.astype(F32)).astype(BF16)
    ri = lax.broadcasted_iota(jnp.int32, (L, L), 0)
    ci = lax.broadcasted_iota(jnp.int32, (L, L), 1)
    mat = ((ri >= ci) if backward else (ri <= ci)).astype(F32).astype(BF16)
    out = _dot(jnp.concatenate([hi, mid, lo], axis=0), mat)
    return out[0:N_GATES] + out[N_GATES:2 * N_GATES] + out[2 * N_GATES:3 * N_GATES]


def _gate_rows(gt, cumt, h, backward):
    L = gt.shape[1]
    ig = (ML_HEADS if backward else 0) + h
    lf = (3 * ML_HEADS if backward else 2 * ML_HEADS) + h
    end = 0 if backward else L - 1
    return gt[ig:ig + 1, :], cumt[lf:lf + 1, :], cumt[lf:lf + 1, end:end + 1]


def _stage_weighted_values(lhs_ref, row0, vt_ref, cs, gt, cumt, backward):
    L = gt.shape[1]
    H = ML_HEADS
    ig0, lf0, end = (H, 3 * H, 0) if backward else (0, 2 * H, L - 1)
    b_rows = cumt[lf0:lf0 + H]
    b_end = b_rows[:, end:end + 1]
    w_end = b_end - b_rows + gt[ig0:ig0 + H]
    m_loc = jnp.max(w_end, axis=1, keepdims=True)
    wk = jnp.exp2(w_end - m_loc)
    for h in range(H):
        lhs_ref[row0 + h * VT_ROWS:row0 + (h + 1) * VT_ROWS, :] = (
            vt_ref[h * VT_ROWS:(h + 1) * VT_ROWS, cs].astype(F32) * wk[h:h + 1]).astype(lhs_ref.dtype)
    return m_loc, b_end


def _pair_update(lhs_ref, row0, k_pair, p, m_loc, b_end, lo):
    r0 = row0 + 2 * p * VT_ROWS
    out = _dot(lhs_ref[r0:r0 + 2 * VT_ROWS, :], k_pair)
    hs = (2 * p, 2 * p + 1)
    return (jnp.where(lo, out[0:VT_ROWS], out[VT_ROWS:2 * VT_ROWS]),
            [m_loc[h:h + 1] for h in hs], [b_end[h:h + 1] for h in hs])


def _apply_update(s_pair, upd, m_locs, b_ends, ms, lo):
    decays, gains, m_news = [], [], []
    for e in range(2):
        m_new = jnp.maximum(b_ends[e] + ms[e], m_locs[e])
        decays.append(jnp.exp2(b_ends[e] + ms[e] - m_new))
        gains.append(jnp.exp2(m_locs[e] - m_new))
        m_news.append(m_new)
    return jnp.where(lo, decays[0], decays[1]) * s_pair + jnp.where(lo, gains[0], gains[1]) * upd, m_news


def _absorb_chunks(k_ref, vt_ref, gt_ref, s_ref, m_ref, spre_ref, mpre_ref, lhs_ref, backward, cps):
    L = ML_CHUNK
    H = ML_HEADS
    lo = _lo_lanes()
    order = list(range(cps - 1, -1, -1)) if backward else list(range(cps))
    npair = H // 2
    scal = {}
    for u in order:
        cs = slice(u * L, (u + 1) * L)
        gt = _log2_gates(gt_ref, cs)
        scal[u] = _stage_weighted_values(lhs_ref, u * H * VT_ROWS, vt_ref, cs, gt, _cum_rows(gt, backward), backward)
    updates = {}
    for u in order:
        cs = slice(u * L, (u + 1) * L)
        for p in range(npair):
            updates[u, p] = _pair_update(lhs_ref, u * H * VT_ROWS, k_ref[cs, p * LANES:(p + 1) * LANES], p,
                                         *scal[u], lo)
    for u in order:
        spre_ref[u] = s_ref[...]
        mpre_ref[u] = m_ref[...]
        for p in range(npair):
            hs = (2 * p, 2 * p + 1)
            ms = [m_ref[h:h + 1, 0:1] for h in hs]
            s_new, m_new = _apply_update(s_ref[p], *updates[u, p], ms, lo)
            s_ref[p] = s_new
            for e in range(2):
                m_ref[hs[e]:hs[e] + 1, :] = jnp.broadcast_to(m_new[e], (1, LANES))


def _ml_state_kernel(k_ref, vt_ref, gt_ref, s0_ref, m0_ref, spre_ref, mpre_ref, sfin_ref, mfin_ref, lhs_ref, *,
                     backward, cps):
    @pl.when(pl.program_id(1) == 0)
    def _():
        sfin_ref[...] = s0_ref[...]
        mfin_ref[...] = m0_ref[...]

    _absorb_chunks(k_ref, vt_ref, gt_ref, sfin_ref.at[0], mfin_ref.at[0], spre_ref.at[0], mpre_ref.at[0],
                   lhs_ref, backward, cps)


def _ml_state_scan(k, vt, gt, s0, m0, batch, t, backward, cps):
    L = ML_CHUNK
    nc = t // L
    cps = min(cps, nc)
    assert t % L == 0 and nc % cps == 0
    steps = nc // cps
    np_ = ML_HEADS // 2
    si = (lambda s: steps - 1 - s) if backward else (lambda s: s)
    st = pl.BlockSpec((1, np_, VT_ROWS, LANES), lambda b, s: (b, 0, 0, 0))
    mx = pl.BlockSpec((1, ML_HEADS, LANES), lambda b, s: (b, 0, 0))
    return pl.pallas_call(
        functools.partial(_ml_state_kernel, backward=backward, cps=cps),
        grid=(batch, steps),
        in_specs=[pl.BlockSpec((cps * L, D_CQK), lambda b, s: (b * steps + si(s), 0)),
                  pl.BlockSpec((ML_HEADS * VT_ROWS, cps * L), lambda b, s: (b, si(s))),
                  pl.BlockSpec((LANES, cps * L), lambda b, s: (b, si(s))), st, mx],
        out_specs=[pl.BlockSpec((1, cps, np_, VT_ROWS, LANES), lambda b, s: (b, si(s), 0, 0, 0)),
                   pl.BlockSpec((1, cps, ML_HEADS, LANES), lambda b, s: (b, si(s), 0, 0)), st, mx],
        out_shape=[jax.ShapeDtypeStruct((batch, nc, np_, VT_ROWS, LANES), F32),
                   jax.ShapeDtypeStruct((batch, nc, ML_HEADS, LANES), F32),
                   jax.ShapeDtypeStruct((batch, np_, VT_ROWS, LANES), F32),
                   jax.ShapeDtypeStruct((batch, ML_HEADS, LANES), F32)],
        scratch_shapes=[pltpu.VMEM((cps * ML_HEADS * VT_ROWS, L), BF16)],
        compiler_params=_cparams(("arbitrary", "arbitrary")),
        name="ml_state_bwd" if backward else "ml_state_fwd",
    )(k, vt, gt, s0, m0)


def _ml_out_kernel(q_ref, k_ref, vt_ref, gt_ref, sg_ref, sb_ref, mb_ref, s0_ref, m0_ref, ngb_ref, o_ref,
                   s_ref, m_ref, lhs_ref, *, cps):
    @pl.when(pl.program_id(1) == 0)
    def _():
        s_ref[...] = s0_ref[0]
        m_ref[...] = m0_ref[0]

    L = ML_CHUNK
    lo = _lo_lanes()
    ri = lax.broadcasted_iota(jnp.int32, (L, L), 0)
    ci = lax.broadcasted_iota(jnp.int32, (L, L), 1)
    masks = (ri <= ci, ri >= ci)
    npair = ML_HEADS // 2
    for u in range(cps):
        cs = slice(u * L, (u + 1) * L)
        gt = _log2_gates(gt_ref, cs)
        cum_f = _cum_rows(gt, False)
        cumt = (cum_f, cum_f[:, L - 1:L] - cum_f + gt)
        r_rows = jnp.concatenate([gt[0:ML_HEADS] - cumt[0][2 * ML_HEADS:3 * ML_HEADS],
                                  gt[ML_HEADS:2 * ML_HEADS] - cumt[1][3 * ML_HEADS:4 * ML_HEADS],
                                  jnp.zeros((LANES - 2 * ML_HEADS, L), F32)], axis=0)
        r_cols = r_rows.T
        lhs_row0 = u * ML_HEADS * VT_ROWS
        upd_scal = _stage_weighted_values(lhs_ref, lhs_row0, vt_ref, cs, gt, cum_f, False)
        prods = []
        for p in range(npair):
            sl = slice(p * LANES, (p + 1) * LANES)
            stack = jnp.concatenate([k_ref[cs, sl], s_ref[p].astype(BF16), sb_ref[0, u, p].astype(BF16)], axis=0)
            prods.append(_dot_nt(stack, _pair_rows(q_ref[cs, sl], lo)))
        pts, inters, mqs = [], [], []
        for h in range(ML_HEADS):
            qs = slice((h % 2) * L, (h % 2 + 1) * L)
            for d in range(2):
                ig_row, b_row, _ = _gate_rows(gt, cumt[d], h, d == 1)
                r_col = r_cols[:, d * ML_HEADS + h:d * ML_HEADS + h + 1]
                m = mb_ref[0, u, h:h + 1, 0:1] if d else m_ref[h:h + 1, 0:1]
                g_row = b_row + m
                dt = jnp.where(masks[d], b_row + r_col, NEG_INF)
                m_q = jnp.maximum(g_row, jnp.max(dt, axis=0, keepdims=True))
                pts.append((jnp.exp2(dt - m_q) * prods[h // 2][0:L, qs]).astype(BF16))
                inters.append(jnp.exp2(g_row - m_q))
                mqs.append(m_q)
        nums = [_dot(vt_ref[h * VT_ROWS:(h + 1) * VT_ROWS, cs], jnp.concatenate(pts[2 * h:2 * h + 2], axis=1))
                for h in range(ML_HEADS)]
        for h in range(ML_HEADS):
            qs = slice((h % 2) * L, (h % 2 + 1) * L)
            hsum = None
            for d in range(2):
                i = 2 * h + d
                tot = (inters[i] * prods[h // 2][L + d * VT_ROWS:L + (d + 1) * VT_ROWS, qs]
                       + nums[h][:, d * L:(d + 1) * L])
                den = tot[ML_V_DIM:ML_V_DIM + 1]
                hd = tot[0:ML_V_DIM] / jnp.maximum(jnp.abs(den), jnp.exp2(-mqs[i]))
                hsum = hd if hsum is None else hsum + hd
            hs = slice(h * ML_V_DIM, (h + 1) * ML_V_DIM)
            ms = jnp.mean(hsum * hsum, axis=0, keepdims=True)
            y = hsum * lax.rsqrt(ms + EPS) * ngb_ref[hs, :]
            o_ref[cs, hs] = (y.T.astype(BF16) * sg_ref[cs, hs]).astype(o_ref.dtype)
        for p in range(npair):
            hp = (2 * p, 2 * p + 1)
            ms = [m_ref[h:h + 1, 0:1] for h in hp]
            upd = _pair_update(lhs_ref, lhs_row0, k_ref[cs, p * LANES:(p + 1) * LANES], p, *upd_scal, lo)
            s_new, m_new = _apply_update(s_ref[p], *upd, ms, lo)
            s_ref[p] = s_new
            for e in range(2):
                m_ref[hp[e]:hp[e] + 1, :] = jnp.broadcast_to(m_new[e], (1, LANES))


def _ml_out(q, k, vt, gt, sg, s_bwd, m_bwd, s0, m0, norm_gb, batch, t, cps):
    L = ML_CHUNK
    nc = t // L
    assert t % L == 0 and nc % cps == 0
    steps = nc // cps
    np_ = ML_HEADS // 2
    tok = lambda cols: pl.BlockSpec((cps * L, cols), lambda b, s: (b * steps + s, 0))
    return pl.pallas_call(
        functools.partial(_ml_out_kernel, cps=cps),
        grid=(batch, steps),
        in_specs=[tok(D_CQK), tok(D_CQK),
                  pl.BlockSpec((ML_HEADS * VT_ROWS, cps * L), lambda b, s: (b, s)),
                  pl.BlockSpec((LANES, cps * L), lambda b, s: (b, s)), tok(D_CV),
                  pl.BlockSpec((1, cps, np_, VT_ROWS, LANES), lambda b, s: (b, s, 0, 0, 0)),
                  pl.BlockSpec((1, cps, ML_HEADS, LANES), lambda b, s: (b, s, 0, 0)),
                  pl.BlockSpec((1, np_, VT_ROWS, LANES), lambda b, s: (b, 0, 0, 0)),
                  pl.BlockSpec((1, ML_HEADS, LANES), lambda b, s: (b, 0, 0)),
                  _const_spec((D_CV, LANES))],
        out_specs=tok(D_CV),
        out_shape=jax.ShapeDtypeStruct((batch * t, D_CV), BF16),
        scratch_shapes=[pltpu.VMEM((np_, VT_ROWS, LANES), F32), pltpu.VMEM((ML_HEADS, LANES), F32),
                        pltpu.VMEM((cps * ML_HEADS * VT_ROWS, L), BF16)],
        compiler_params=_cparams(("arbitrary", "arbitrary")),
        name="ml_out",
    )(q, k, vt, gt, sg, s_bwd, m_bwd, s0, m0, norm_gb)


def _even_layer(x2, ctx2, mod3, lat_row, ctx_row, batch, t, n, norm_mix_g, norm_ffn_g, ffn_w, w_in, qg, kg,
                table, cw, cb, lg, lb, w_out, ctx_out):
    tm = min(512, t)
    tmc = min(512, batch * n)
    ql, kl, vl, conv_l = _ab_in_conv(x2, mod3, norm_mix_g, w_in, qg, kg, cw, cb, lg, lb, t, tm)
    qc, kc, vc, glu_c = _ab_in(ctx2, mod3, ctx_row, norm_mix_g, w_in, qg, kg, tmc)
    att_l = _na_attention(ql, kl, vl, kc, vc, table, batch, t, n, rps=4)
    x2 = _mix_ffn(x2, (att_l, conv_l), (0, 0), mod3, lat_row(t // tm), norm_ffn_g, w_out, ffn_w, tm)
    if ctx_out:
        att_c = _ctx_attention(qc, kc, vc, batch, n)
        conv_c = _conv_module(glu_c, cw, cb, lg, lb, batch, n)
        ctx2 = _mix_ffn(ctx2, (att_c, conv_c), (0, 0), mod3, ctx_row, norm_ffn_g, w_out, ffn_w, tmc)
    return x2, ctx2


def _odd_layer(x2, ctx2, mod3, lat_row, ctx_row, batch, t, n, norm_mix_g, norm_ffn_g, ffn_w, w_in, w_vt, cw, cb, gb,
               norm_gb, w_out):
    tm = min(512, t)
    tmc = min(512, n)
    cos_l, sin_l = _rope_tables(t, True)
    cos_c, sin_c = _rope_tables(n, False)
    _, kc, vtc, _, gtc = _ml_in(ctx2, mod3, lambda b: ctx_row(b), norm_mix_g, w_in, w_vt, cw, cb, gb, cos_c,
                                sin_c, batch, n, tmc)
    s_zero = jnp.zeros((batch, ML_HEADS // 2, VT_ROWS, LANES), F32)
    m_zero = jnp.zeros((batch, ML_HEADS, LANES), F32)
    _, _, sf, mf = _ml_state_scan(kc, vtc, gtc, s_zero, m_zero, batch, n, False, cps=2)
    _, _, sb, mb = _ml_state_scan(kc, vtc, gtc, s_zero, m_zero, batch, n, True, cps=2)
    ql, kl, vtl, sgl, gtl = _ml_in(x2, mod3, lambda b: b, norm_mix_g, w_in, w_vt, cw, cb, gb, cos_l, sin_l,
                                   batch, t, tm)
    s_pre, m_pre, _, _ = _ml_state_scan(kl, vtl, gtl, sb, mb, batch, t, True, cps=8)
    gated = _ml_out(ql, kl, vtl, gtl, sgl, s_pre, m_pre, sf, mf, norm_gb, batch, t, cps=2)
    return _mix_ffn(x2, (gated, gated), (0, 1), mod3, lat_row(t // tm), norm_ffn_g, w_out, ffn_w, tm)


def kernel(x, c, ctx, c_ctx, ada_w, ada_b, norm_mix_g, norm_ffn_g, ffn_w_gate, ffn_w_up, ffn_w_down, ab_w_in,
           na_q_norm_g, na_k_norm_g, na_rpb, conv_w, conv_b, conv_ln_g, conv_ln_b, ab_w_out, ml_w_in, ml_conv_w,
           ml_conv_b, ml_gate_b, ml_norm_g, ml_w_out):
    batch, t, d = x.shape
    n = ctx.shape[1]
    depth = ada_w.shape[0]
    assert batch + 1 <= MOD_ROWS and depth % 2 == 0, "odd layers are only implemented as the last-layer form"
    s_rows = jnp.zeros((MOD_ROWS, d), F32).at[:batch].set(c).at[batch].set(c_ctx)
    mod = _ada_mod(s_rows, ada_w, ada_b)
    x2 = x.reshape(batch * t, d)
    ctx2 = ctx.reshape(batch * n, d)
    lat_row = lambda tiles: (lambda i: i // tiles)
    ctx_row = lambda *_: batch
    ffn_bf16 = (ffn_w_gate.astype(BF16), ffn_w_up.astype(BF16), ffn_w_down.astype(BF16))
    rep8 = lambda v: jnp.broadcast_to(v[None, :], (SUBLANES, v.shape[0]))
    for l in range(depth):
        j = l // 2
        last = l == depth - 1
        mod3 = mod[l].reshape(MOD_ROWS, 1, 6 * d)
        ffn_w = (l,) + ffn_bf16
        nmg = norm_mix_g[l].reshape(1, d)
        nfg = norm_ffn_g[l].reshape(1, d)
        if l % 2 == 0:
            qg = (jnp.tile(na_q_norm_g[j], NA_HEADS) * (NA_HEAD_DIM ** -0.5 * LOG2E)).reshape(1, D_A)
            kg = jnp.tile(na_k_norm_g[j], NA_HEADS).reshape(1, D_A)
            x2, ctx2 = _even_layer(x2, ctx2, mod3, lat_row, ctx_row, batch, t, n, nmg, nfg, ffn_w,
                                   ab_w_in[j].astype(BF16), qg, kg, _na_bias_table(na_rpb[j] * LOG2E),
                                   jnp.broadcast_to(conv_w[j][:, None, :], (CONV_WIDTH, SUBLANES, D_B)),
                                   rep8(conv_b[j]), rep8(conv_ln_g[j]), rep8(conv_ln_b[j]),
                                   ab_w_out[j].astype(BF16), not last)
        else:
            assert last
            nqk = 2 * D_CQK
            w = ml_w_in[j]
            w_gates = jnp.pad(w[:, nqk + 2 * D_CV:], ((0, 0), (0, LANES - 4 * ML_HEADS)))
            w_in = jnp.concatenate([w[:, :nqk], w[:, nqk + D_CV:nqk + 2 * D_CV], w_gates], axis=1).astype(BF16)
            w_vt = w[:, nqk:nqk + D_CV].T.astype(BF16)
            gb = jnp.pad(ml_gate_b[j], (0, LANES - 4 * ML_HEADS)).reshape(1, LANES)
            norm_gb = jnp.broadcast_to(ml_norm_g[j][:, None], (D_CV, LANES))
            x2 = _odd_layer(x2, ctx2, mod3, lat_row, ctx_row, batch, t, n, nmg, nfg, ffn_w, w_in, w_vt,
                            ml_conv_w[j], ml_conv_b[j].reshape(1, nqk), gb, norm_gb, ml_w_out[j].astype(BF16))
    return x2.reshape(batch, t, d)
```

```python
import functools
import math

import jax
import jax.numpy as jnp
import numpy as np
from jax import lax
from jax.experimental import pallas as pl
from jax.experimental.pallas import tpu as pltpu

F32 = jnp.float32
BF16 = jnp.bfloat16

EPS = 1e-6
GRID_W = 64
NA_HEADS = 8
NA_HEAD_DIM = 64
D_A = NA_HEADS * NA_HEAD_DIM
NA_WIN_H = 8
NA_WIN_W = 16
D_B = 512
CONV_WIDTH = 31
ML_HEADS = 8
ML_QK_DIM = 64
ML_V_DIM = 128
D_CQK = ML_HEADS * ML_QK_DIM
D_CV = ML_HEADS * ML_V_DIM
ML_SHORT_CONV = 5
ML_CHUNK = 128
ROPE_BASE = 10000.0

LANES = 128
SUBLANES = 8
HALO = 16
VT_ROWS = ML_V_DIM + HALO
MOD_ROWS = 16
VMEM_LIMIT = 56 * 1024 * 1024
FFN_BLOCK_ROWS = 1024
NEG_INF = float("-inf")
LOG2E = math.log2(math.e)


def _cparams(sem):
    return pltpu.CompilerParams(dimension_semantics=sem, vmem_limit_bytes=VMEM_LIMIT)


def _const_spec(shape):
    nd = len(shape)
    return pl.BlockSpec(shape, lambda *_: (0,) * nd, pipeline_mode=pl.Buffered(1))


def _sigmoid(x):
    return 1.0 / (1.0 + jnp.exp2(x * (-LOG2E)))


def _silu(x):
    return x * _sigmoid(x)


def _modulate(x, g, shift, scale):
    ms = jnp.mean(x * x, axis=-1, keepdims=True)
    return x * lax.rsqrt(ms + EPS) * (g * (1.0 + scale)) + shift


def _dot(a, b):
    return jnp.dot(a, b, preferred_element_type=F32)


def _dot_nt(a, b):
    return lax.dot_general(a, b, (((1,), (1,)), ((), ())), preferred_element_type=F32)


def _lo_lanes():
    return lax.broadcasted_iota(jnp.int32, (1, LANES), 1) < (LANES // 2)


def _ada_kernel(s_ref, w_ref, b_ref, o_ref):
    s = _silu(s_ref[...]).astype(BF16)
    o_ref[0] = _dot(s, w_ref[0].astype(BF16)) + b_ref[0]


def _ada_mod(s_rows, ada_w, ada_b):
    depth, d, n = ada_w.shape
    tn = n // 4
    return pl.pallas_call(
        _ada_kernel,
        grid=(depth, n // tn),
        in_specs=[pl.BlockSpec((MOD_ROWS, d), lambda l, j: (0, 0)),
                  pl.BlockSpec((1, d, tn), lambda l, j: (l, 0, j)),
                  pl.BlockSpec((1, 1, tn), lambda l, j: (l, 0, j))],
        out_specs=pl.BlockSpec((1, MOD_ROWS, tn), lambda l, j: (l, 0, j)),
        out_shape=jax.ShapeDtypeStruct((depth, MOD_ROWS, n), F32),
        compiler_params=_cparams(("arbitrary", "arbitrary")),
        name="ada_mod",
    )(s_rows, ada_w, ada_b.reshape(depth, 1, n))


def _mod_spec(d, sec, row_fn):
    return pl.BlockSpec((1, 1, d), lambda *idx: (row_fn(*idx), 0, sec))


def _conv_chunk(window, base, cw_ref, cb_ref, lg_ref, lb_ref, rc):
    first = HALO - CONV_WIDTH // 2
    groups = rc // SUBLANES
    accs = [cb_ref[...]] * groups
    for w in range(CONV_WIDTH):
        s, a = (first + w) % SUBLANES, (first + w) // SUBLANES
        wv = cw_ref[w]
        for gi in range(groups):
            accs[gi] = accs[gi] + window(s, base + (a + gi) * SUBLANES, SUBLANES) * wv
    acc = jnp.concatenate(accs, axis=0)
    mu = jnp.mean(acc, axis=-1, keepdims=True)
    xc = acc - mu
    var = jnp.mean(xc * xc, axis=-1, keepdims=True)
    rep = lambda ref: jnp.concatenate([ref[...]] * groups, axis=0)
    return _silu(xc * lax.rsqrt(var + EPS) * rep(lg_ref) + rep(lb_ref))


def _head_norm(r, gain_ref, out_ref, lo):
    for p in range(D_A // LANES):
        sl = slice(p * LANES, (p + 1) * LANES)
        xp = r[:, sl]
        sq = xp * xp
        s_all = jnp.sum(sq, axis=-1, keepdims=True)
        s_lo = jnp.sum(jnp.where(lo, sq, 0.0), axis=-1, keepdims=True)
        ms = jnp.where(lo, s_lo, s_all - s_lo) * (1.0 / NA_HEAD_DIM)
        out_ref[:, sl] = (xp * lax.rsqrt(ms + EPS) * gain_ref[:, sl]).astype(out_ref.dtype)


def _ab_in_kernel(x_ref, sh_ref, sc_ref, g_ref, w_ref, qg_ref, kg_ref, q_ref, k_ref, v_ref, glu_ref):
    h = _modulate(x_ref[...], g_ref[...], sh_ref[0], sc_ref[0]).astype(BF16)
    lo = _lo_lanes()
    _head_norm(_dot(h, w_ref[:, 0:D_A]), qg_ref, q_ref, lo)
    _head_norm(_dot(h, w_ref[:, D_A:2 * D_A]), kg_ref, k_ref, lo)
    v_ref[...] = _dot(h, w_ref[:, 2 * D_A:3 * D_A]).astype(v_ref.dtype)
    u = _dot(h, w_ref[:, 3 * D_A:3 * D_A + D_B])
    gt = _dot(h, w_ref[:, 3 * D_A + D_B:3 * D_A + 2 * D_B])
    glu_ref[...] = u * _sigmoid(gt)


def _ab_in_conv_kernel(x_ref, xn_ref, sh_ref, sc_ref, g_ref, w_ref, qg_ref, kg_ref, cw_ref, cb_ref, lg_ref, lb_ref,
                       q_ref, k_ref, v_ref, conv_ref, win_ref, shift_ref, *, tm, nts, rc):
    i = pl.program_id(0)

    @pl.when(i == 0)
    def _():
        win_ref[...] = jnp.zeros_like(win_ref)

    span = tm + 2 * HALO - SUBLANES
    for s in range(SUBLANES):
        shift_ref[s, 0:span] = win_ref[s:s + span]
    prev_tail = win_ref[tm:tm + HALO]
    window = lambda s, start, size: shift_ref[s, start:start + size, :]
    for c in range(tm // rc):
        conv_ref[c * rc:(c + 1) * rc, :] = _conv_chunk(window, c * rc, cw_ref, cb_ref, lg_ref, lb_ref, rc
                                                       ).astype(conv_ref.dtype)
    g, sh, sc = g_ref[...], sh_ref[0], sc_ref[0]
    h = _modulate(x_ref[...], g, sh, sc).astype(BF16)
    h_ext = jnp.concatenate([h, _modulate(xn_ref[...], g, sh, sc).astype(BF16)], axis=0)
    lo = _lo_lanes()
    _head_norm(_dot(h, w_ref[:, 0:D_A]), qg_ref, q_ref, lo)
    _head_norm(_dot(h, w_ref[:, D_A:2 * D_A]), kg_ref, k_ref, lo)
    v_ref[...] = _dot(h, w_ref[:, 2 * D_A:3 * D_A]).astype(v_ref.dtype)
    u = _dot(h_ext, w_ref[:, 3 * D_A:3 * D_A + D_B])
    gt = _dot(h_ext, w_ref[:, 3 * D_A + D_B:3 * D_A + 2 * D_B])
    glu = u * _sigmoid(gt)
    seq_pos = lax.rem(i, nts)
    win_ref[0:HALO] = jnp.where(seq_pos == 0, 0.0, prev_tail)
    win_ref[HALO:HALO + tm] = glu[0:tm]
    win_ref[HALO + tm:2 * HALO + tm] = jnp.where(seq_pos == nts - 1, 0.0, glu[tm:tm + HALO])


def _ab_in_conv(x2, mod3, norm_g, w_in, qg, kg, cw, cb, lg, lb, t, tm):
    m, d = x2.shape
    ntiles = m // tm
    nts = t // tm
    hb = tm // HALO
    cur = lambda i: jnp.minimum(i, ntiles - 1)
    tok = lambda c: pl.BlockSpec((tm, c), lambda i: (cur(i), 0))
    row = lambda i: cur(i) // nts
    rep = lambda: _const_spec((SUBLANES, D_B))
    return pl.pallas_call(
        functools.partial(_ab_in_conv_kernel, tm=tm, nts=nts, rc=32),
        grid=(ntiles + 1,),
        in_specs=[tok(d),
                  pl.BlockSpec((HALO, d), lambda i: (jnp.minimum((cur(i) + 1) * hb, m // HALO - 1), 0)),
                  _mod_spec(d, 0, row), _mod_spec(d, 1, row), _const_spec((1, d)),
                  _const_spec(w_in.shape), _const_spec((1, D_A)), _const_spec((1, D_A)),
                  _const_spec((CONV_WIDTH, SUBLANES, D_B)), rep(), rep(), rep()],
        out_specs=[tok(D_A), tok(D_A), tok(D_A),
                   pl.BlockSpec((tm, D_B), lambda i: (jnp.maximum(i - 1, 0), 0))],
        out_shape=[jax.ShapeDtypeStruct((m, D_A), BF16), jax.ShapeDtypeStruct((m, D_A), BF16),
                   jax.ShapeDtypeStruct((m, D_A), BF16), jax.ShapeDtypeStruct((m, D_B), BF16)],
        scratch_shapes=[pltpu.VMEM((tm + 2 * HALO, D_B), F32), pltpu.VMEM((SUBLANES, tm + 2 * HALO, D_B), F32)],
        compiler_params=_cparams(("arbitrary",)),
        name="ab_in_conv",
    )(x2, x2, mod3, mod3, norm_g, w_in, qg, kg, cw, cb, lg, lb)


def _ab_in(x2, mod3, row_fn, norm_g, w_in, qg, kg, tm):
    m, d = x2.shape
    tok = lambda c: pl.BlockSpec((tm, c), lambda i: (i, 0))
    return pl.pallas_call(
        _ab_in_kernel,
        grid=(m // tm,),
        in_specs=[tok(d), _mod_spec(d, 0, row_fn), _mod_spec(d, 1, row_fn), _const_spec((1, d)),
                  _const_spec(w_in.shape), _const_spec((1, D_A)), _const_spec((1, D_A))],
        out_specs=[tok(D_A), tok(D_A), tok(D_A), tok(D_B)],
        out_shape=[jax.ShapeDtypeStruct((m, D_A), BF16), jax.ShapeDtypeStruct((m, D_A), BF16),
                   jax.ShapeDtypeStruct((m, D_A), BF16), jax.ShapeDtypeStruct((m, D_B), F32)],
        compiler_params=_cparams(("arbitrary",)),
        name="ab_in",
    )(x2, mod3, mod3, norm_g, w_in, qg, kg)


def _softmax_pv(scores, values):
    m = functools.reduce(jnp.maximum, [jnp.max(s, axis=-1, keepdims=True) for s in scores])
    ps = [jnp.exp2(s - m) for s in scores]
    l = functools.reduce(jnp.add, [jnp.sum(p, axis=-1, keepdims=True) for p in ps])
    o = functools.reduce(jnp.add, [_dot(p.astype(BF16), v) for p, v in zip(ps, values)])
    return o / l


def _pair_rows(qp, lo):
    zero = jnp.zeros_like(qp)
    return jnp.concatenate([jnp.where(lo, qp, zero), jnp.where(lo, zero, qp)], axis=0)


def _na_kernel(q_ref, k_ref, v_ref, kc_ref, vc_ref, tb_ref, o_ref, *, rows, rps):
    nk = NA_WIN_H * GRID_W
    lo = _lo_lanes()
    npair = D_A // LANES
    sls = [slice(p * LANES, (p + 1) * LANES) for p in range(npair)]
    starts, rhos = [], []
    for i in range(rps):
        r = pl.program_id(1) * rps + i
        r0 = jnp.clip(r - NA_WIN_H // 2, 0, rows - NA_WIN_H)
        starts.append(pl.multiple_of(r0 * GRID_W, GRID_W))
        rhos.append(r0 - r + (NA_WIN_H - 1))
    scores = []
    for i in range(rps):
        for p in range(npair):
            q2 = _pair_rows(q_ref[i * GRID_W:(i + 1) * GRID_W, sls[p]], lo)
            bias = jnp.concatenate([tb_ref[rhos[i] + 2 * a, p] for a in range(NA_WIN_H // 2)], axis=1)
            scores.append((_dot_nt(q2, k_ref[pl.ds(starts[i], nk), sls[p]]) + bias,
                           _dot_nt(q2, kc_ref[:, sls[p]])))
    probs = []
    for s_loc, s_ctx in scores:
        m = jnp.maximum(jnp.max(s_loc, axis=-1, keepdims=True), jnp.max(s_ctx, axis=-1, keepdims=True))
        p_loc = jnp.exp2(s_loc - m)
        p_ctx = jnp.exp2(s_ctx - m)
        l = jnp.sum(p_loc, axis=-1, keepdims=True) + jnp.sum(p_ctx, axis=-1, keepdims=True)
        probs.append((p_loc.astype(BF16), p_ctx.astype(BF16), l))
    for i in range(rps):
        for p in range(npair):
            p_loc, p_ctx, l = probs[i * npair + p]
            o2 = (_dot(p_loc, v_ref[pl.ds(starts[i], nk), sls[p]]) + _dot(p_ctx, vc_ref[:, sls[p]])) / l
            o_ref[i * GRID_W:(i + 1) * GRID_W, sls[p]] = jnp.where(
                lo, o2[0:GRID_W], o2[GRID_W:2 * GRID_W]).astype(o_ref.dtype)


def _na_attention(q, k, v, kc, vc, table, batch, t, n, rps):
    rows = t // GRID_W
    assert rows >= NA_WIN_H and t % GRID_W == 0 and rows % rps == 0
    steps = rows // rps
    return pl.pallas_call(
        functools.partial(_na_kernel, rows=rows, rps=rps),
        grid=(batch, steps),
        in_specs=[pl.BlockSpec((rps * GRID_W, D_A), lambda b, s: (b * steps + s, 0)),
                  pl.BlockSpec((t, D_A), lambda b, s: (b, 0)),
                  pl.BlockSpec((t, D_A), lambda b, s: (b, 0)),
                  pl.BlockSpec((n, D_A), lambda b, s: (b, 0)),
                  pl.BlockSpec((n, D_A), lambda b, s: (b, 0)), _const_spec(table.shape)],
        out_specs=pl.BlockSpec((rps * GRID_W, D_A), lambda b, s: (b * steps + s, 0)),
        out_shape=jax.ShapeDtypeStruct((batch * t, D_A), BF16),
        compiler_params=_cparams(("arbitrary", "arbitrary")),
        name="na_attention",
    )(q, k, v, kc, vc, table)


def _ctx_attn_kernel(q_ref, k_ref, v_ref, o_ref):
    lo = _lo_lanes()
    for p in range(D_A // LANES):
        sl = slice(p * LANES, (p + 1) * LANES)
        qp = q_ref[:, sl]
        kp = k_ref[:, sl]
        vp = v_ref[:, sl]
        outs = []
        for e in range(2):
            qm = jnp.where(lo if e == 0 else jnp.logical_not(lo), qp, jnp.zeros_like(qp))
            outs.append(_softmax_pv([_dot_nt(qm, kp)], [vp]))
        o_ref[:, sl] = jnp.where(lo, outs[0], outs[1]).astype(o_ref.dtype)


def _ctx_attention(q, k, v, batch, n):
    spec = pl.BlockSpec((n, D_A), lambda b: (b, 0))
    return pl.pallas_call(
        _ctx_attn_kernel,
        grid=(batch,),
        in_specs=[spec, spec, spec],
        out_specs=spec,
        out_shape=jax.ShapeDtypeStruct((batch * n, D_A), BF16),
        compiler_params=_cparams(("arbitrary",)),
        name="ctx_attention",
    )(q, k, v)


def _na_bias_table(rpb):
    n_rel = 2 * NA_WIN_W - 1
    lead = GRID_W - NA_WIN_W
    ext = jnp.pad(rpb.astype(F32), ((0, 0), (0, 0), (lead, 2 * GRID_W - 1 - lead - n_rel)), constant_values=NEG_INF)
    band = jnp.stack([ext[:, :, GRID_W - 1 - c:2 * GRID_W - 1 - c] for c in range(GRID_W)], axis=2)
    col = np.arange(GRID_W)
    c0 = np.clip(col - NA_WIN_W // 2, 0, GRID_W - NA_WIN_W)
    inside = (col[None, :] >= c0[:, None]) & (col[None, :] < c0[:, None] + NA_WIN_W)
    band = jnp.where(inside[None, None], band, NEG_INF)
    two = jnp.concatenate([band[:, :-1], band[:, 1:]], axis=-1)
    two = two.reshape(NA_HEADS // 2, 2, 2 * NA_WIN_H - 2, GRID_W, 2 * GRID_W).transpose(2, 0, 1, 3, 4)
    return two.reshape(2 * NA_WIN_H - 2, NA_HEADS // 2, 2 * GRID_W, 2 * GRID_W)


def _conv_kernel(prev_ref, x_ref, next_ref, cw_ref, cb_ref, lg_ref, lb_ref, o_ref, xs_ref, *, tt, nt, rc):
    i = pl.program_id(1)
    xs_ref[0, 0:HALO] = jnp.where(i > 0, prev_ref[...], 0.0)
    xs_ref[0, HALO:HALO + tt] = x_ref[...]
    xs_ref[0, HALO + tt:2 * HALO + tt] = jnp.where(i < nt - 1, next_ref[...], 0.0)
    span = tt + 2 * HALO - SUBLANES
    for s in range(1, SUBLANES):
        xs_ref[s, 0:span] = xs_ref[0, s:s + span]
    window = lambda s, start, size: xs_ref[s, pl.ds(start, size), :]

    def chunk(c, carry):
        base = pl.multiple_of(c * rc, rc)
        o_ref[pl.ds(base, rc), :] = _conv_chunk(window, base, cw_ref, cb_ref, lg_ref, lb_ref, rc).astype(o_ref.dtype)
        return carry

    lax.fori_loop(0, tt // rc, chunk, 0, unroll=4)


def _conv_module(glu, cw, cb, lg, lb, batch, t):
    tt = min(t, 512)
    nt = t // tt
    hb = tt // HALO
    nhb = t // HALO
    return pl.pallas_call(
        functools.partial(_conv_kernel, tt=tt, nt=nt, rc=32),
        grid=(batch, nt),
        in_specs=[pl.BlockSpec((HALO, D_B), lambda b, i: (b * nhb + jnp.maximum(i * hb - 1, 0), 0)),
                  pl.BlockSpec((tt, D_B), lambda b, i: (b * nt + i, 0)),
                  pl.BlockSpec((HALO, D_B), lambda b, i: (b * nhb + jnp.minimum((i + 1) * hb, nhb - 1), 0)),
                  _const_spec((CONV_WIDTH, SUBLANES, D_B)), _const_spec((SUBLANES, D_B)),
                  _const_spec((SUBLANES, D_B)), _const_spec((SUBLANES, D_B))],
        out_specs=pl.BlockSpec((tt, D_B), lambda b, i: (b * nt + i, 0)),
        out_shape=jax.ShapeDtypeStruct((batch * t, D_B), BF16),
        scratch_shapes=[pltpu.VMEM((SUBLANES, tt + 2 * HALO, D_B), F32)],
        compiler_params=_cparams(("arbitrary", "arbitrary")),
        name="conv_module",
    )(glu, glu, glu, cw, cb, lg, lb)


def _mix_ffn_kernel(x_ref, a1_ref, a2_ref, g1_ref, sh_ref, sc_ref, g2_ref, ng_ref,
                    wo_ref, wg_ref, wu_ref, wd_ref, o_ref, acc_ref, *, tf, sub):
    ka = a1_ref.shape[1]
    d_ff = wg_ref.shape[1]
    tiles = [slice(s * sub, (s + 1) * sub) for s in range(x_ref.shape[0] // sub)]
    hs = []
    for rs in tiles:
        mix = _dot(a1_ref[rs, :], wo_ref[0:ka, :]) + _dot(a2_ref[rs, :], wo_ref[ka:2 * ka, :])
        x1 = x_ref[rs, :] + g1_ref[0] * mix
        o_ref[rs, :] = x1
        hs.append(_modulate(x1, ng_ref[...], sh_ref[0], sc_ref[0]).astype(BF16))
    for rs, h in zip(tiles, hs):
        for c in range(d_ff // tf):
            sl = slice(c * tf, (c + 1) * tf)
            act = (_silu(_dot(h, wg_ref[:, sl])) * _dot(h, wu_ref[:, sl])).astype(BF16)
            part = _dot(act, wd_ref[sl, :])
            if c == 0:
                acc_ref[rs, :] = part
            else:
                acc_ref[rs, :] += part
        o_ref[rs, :] = o_ref[rs, :] + g2_ref[0] * acc_ref[rs, :]


def _mix_ffn(x2, a, a_cols, mod3, row_fn, norm_g, w_out, ffn_w, tm):
    m, d = x2.shape
    a1, a2 = a
    layer, w_gate, w_up, w_down = ffn_w
    ka = w_out.shape[0] // 2
    tok = lambda c, j=0: pl.BlockSpec((tm, c), lambda i: (i, j))
    layer_spec = lambda w: pl.BlockSpec((None,) + w.shape[1:], lambda i: (layer, 0, 0), pipeline_mode=pl.Buffered(1))
    return pl.pallas_call(
        functools.partial(_mix_ffn_kernel, tf=256, sub=min(tm, 512)),
        grid=(m // tm,),
        in_specs=[tok(d), tok(ka, a_cols[0]), tok(ka, a_cols[1]),
                  _mod_spec(d, 2, row_fn), _mod_spec(d, 3, row_fn), _mod_spec(d, 4, row_fn),
                  _mod_spec(d, 5, row_fn), _const_spec((1, d)),
                  _const_spec(w_out.shape), layer_spec(w_gate), layer_spec(w_up), layer_spec(w_down)],
        out_specs=tok(d),
        out_shape=jax.ShapeDtypeStruct((m, d), F32),
        scratch_shapes=[pltpu.VMEM((tm, d), F32)],
        compiler_params=_cparams(("arbitrary",)),
        name="mix_ffn",
    )(x2, a1, a2, mod3, mod3, mod3, mod3, norm_g, w_out, w_gate, w_up, w_down)


def _ml_in_kernel(xp_ref, x_ref, xn_ref, sh_ref, sc_ref, g_ref, w_ref, wvt_ref, cw_ref, cb_ref, gb_ref, cos_ref,
                  sin_ref, q_ref, k_ref, vt_ref, sg_ref, gatest_ref, hs_ref, r_ref, *, tm, nt):
    i = pl.program_id(1)
    g, sh, sc = g_ref[...], sh_ref[0], sc_ref[0]
    hs_ref[0:HALO] = jnp.where(i > 0, _modulate(xp_ref[...], g, sh, sc), 0.0).astype(BF16)
    hs_ref[HALO:HALO + tm] = _modulate(x_ref[...], g, sh, sc).astype(BF16)
    hs_ref[HALO + tm:2 * HALO + tm] = jnp.where(i < nt - 1, _modulate(xn_ref[...], g, sh, sc), 0.0).astype(BF16)
    nqk = 2 * D_CQK
    lane = lax.broadcasted_iota(jnp.int32, (1, LANES), 1)
    r_ref[...] = _dot(hs_ref[...], w_ref[:, 0:nqk])
    hm = hs_ref[HALO:HALO + tm]
    vt = _dot_nt(wvt_ref[...], hm).astype(vt_ref.dtype)
    ones_rows = (lax.broadcasted_iota(jnp.int32, (VT_ROWS - ML_V_DIM, tm), 0) == 0).astype(vt_ref.dtype)
    for h in range(ML_HEADS):
        vt_ref[h * VT_ROWS:h * VT_ROWS + ML_V_DIM, :] = vt[h * ML_V_DIM:(h + 1) * ML_V_DIM]
        vt_ref[h * VT_ROWS + ML_V_DIM:(h + 1) * VT_ROWS, :] = ones_rows
    o_pre = _dot(hm, w_ref[:, nqk:nqk + D_CV])
    gt = _dot(hm, w_ref[:, nqk + D_CV:nqk + D_CV + LANES]) + gb_ref[...]
    sg_ref[...] = _sigmoid(o_pre).astype(sg_ref.dtype)
    log_sig = jnp.minimum(gt, 0.0) - jnp.log(1.0 + jnp.exp(-jnp.abs(gt)))
    gates = jnp.where(lane < 2 * ML_HEADS, gt, log_sig)
    gatest_ref[...] = gates.T
    pad = ML_SHORT_CONV // 2
    rows = tm + 2 * HALO
    qk_parts = []
    for j in range(nqk // LANES):
        rj = r_ref[:, j * LANES:(j + 1) * LANES]
        acc = jnp.broadcast_to(cb_ref[:, j * LANES:(j + 1) * LANES], (tm, LANES))
        for w in range(ML_SHORT_CONV):
            shifted = rj if w == pad else pltpu.roll(rj, (pad - w) % rows, 0)
            acc = acc + shifted[HALO:HALO + tm] * cw_ref[w:w + 1, j * LANES:(j + 1) * LANES]
        qk_parts.append(_silu(acc))
    cos = cos_ref[...]
    sin = sin_ref[...]
    first = (lane & 16) == 0
    for j in range(nqk // LANES):
        xg = qk_parts[j]
        sw = jnp.where(first, pltpu.roll(xg, LANES - 16, 1), pltpu.roll(xg, 16, 1))
        y = xg * cos + sw * sin
        if j < D_CQK // LANES:
            q_ref[:, j * LANES:(j + 1) * LANES] = (y * ML_QK_DIM ** -0.5).astype(q_ref.dtype)
        else:
            jj = j - D_CQK // LANES
            k_ref[:, jj * LANES:(jj + 1) * LANES] = y.astype(k_ref.dtype)


def _ml_in(x2, mod3, row_fn, norm_g, w_in, w_vt, cw, cb, gb, cos_t, sin_t, batch, t, tm):
    m, d = x2.shape
    nt = t // tm
    hb = tm // HALO
    nhb = t // HALO
    nqk = 2 * D_CQK
    tok = lambda c: pl.BlockSpec((tm, c), lambda b, i: (b * nt + i, 0))
    rf = lambda b, i: row_fn(b)
    return pl.pallas_call(
        functools.partial(_ml_in_kernel, tm=tm, nt=nt),
        grid=(batch, nt),
        in_specs=[pl.BlockSpec((HALO, d), lambda b, i: (b * nhb + jnp.maximum(i * hb - 1, 0), 0)),
                  tok(d),
                  pl.BlockSpec((HALO, d), lambda b, i: (b * nhb + jnp.minimum((i + 1) * hb, nhb - 1), 0)),
                  _mod_spec(d, 0, rf), _mod_spec(d, 1, rf), _const_spec((1, d)),
                  _const_spec(w_in.shape), _const_spec(w_vt.shape), _const_spec((ML_SHORT_CONV, nqk)),
                  _const_spec((1, nqk)), _const_spec((1, LANES)),
                  pl.BlockSpec((tm, LANES), lambda b, i: (i, 0)),
                  pl.BlockSpec((tm, LANES), lambda b, i: (i, 0))],
        out_specs=[tok(D_CQK), tok(D_CQK), pl.BlockSpec((ML_HEADS * VT_ROWS, tm), lambda b, i: (b, i)),
                   tok(D_CV), pl.BlockSpec((LANES, tm), lambda b, i: (b, i))],
        out_shape=[jax.ShapeDtypeStruct((m, D_CQK), BF16), jax.ShapeDtypeStruct((m, D_CQK), BF16),
                   jax.ShapeDtypeStruct((batch * ML_HEADS * VT_ROWS, t), BF16),
                   jax.ShapeDtypeStruct((m, D_CV), BF16),
                   jax.ShapeDtypeStruct((batch * LANES, t), F32)],
        scratch_shapes=[pltpu.VMEM((tm + 2 * HALO, d), BF16), pltpu.VMEM((tm + 2 * HALO, nqk), F32)],
        compiler_params=_cparams(("arbitrary", "arbitrary")),
        name="ml_in",
    )(x2, x2, x2, mod3, mod3, norm_g, w_in, w_vt, cw, cb, gb, cos_t, sin_t)


def _rope_tables(t, use_rope):
    if not use_rope:
        return jnp.ones((t, LANES), F32), jnp.zeros((t, LANES), F32)
    in_head = np.arange(LANES) % ML_QK_DIM
    nf = ML_QK_DIM // 4
    inv = ROPE_BASE ** (-np.arange(nf, dtype=np.float64) / nf)
    freq = inv[in_head % nf]
    sign = np.where((in_head // nf) % 2 == 0, -1.0, 1.0)
    tok = np.arange(t)
    pos = np.where((in_head // (2 * nf))[None, :] == 0, (tok // GRID_W)[:, None], (tok % GRID_W)[:, None])
    ang = pos.astype(np.float64) * freq[None, :]
    return jnp.asarray(np.cos(ang), F32), jnp.asarray(np.sin(ang) * sign[None, :], F32)


N_GATES = 4 * ML_HEADS


def _log2_gates(gt_ref, cs):
    return gt_ref[0:N_GATES, cs] * LOG2E


def _cum_rows(x, backward):
    L = x.shape[1]
    hi = x.astype(BF16)
    r1 = x - hi.astype(F32)
    mid = r1.astype(BF16)
    lo = (r1 - mid.astype(F32)).astype(BF16)
    ri = lax.broadcasted_iota(jnp.int32, (L, L), 0)
    ci = lax.broadcasted_iota(jnp.int32, (L, L), 1)
    mat = ((ri >= ci) if backward else (ri <= ci)).astype(F32).astype(BF16)
    out = _dot(jnp.concatenate([hi, mid, lo], axis=0), mat)
    return out[0:N_GATES] + out[N_GATES:2 * N_GATES] + out[2 * N_GATES:3 * N_GATES]


def _gate_rows(gt, cumt, h, backward):
    L = gt.shape[1]
    ig = (ML_HEADS if backward else 0) + h
    lf = (3 * ML_HEADS if backward else 2 * ML_HEADS) + h
    end = 0 if backward else L - 1
    return gt[ig:ig + 1, :], cumt[lf:lf + 1, :], cumt[lf:lf + 1, end:end + 1]


def _stage_weighted_values(lhs_ref, row0, vt_ref, cs, gt, cumt, backward):
    L = gt.shape[1]
    H = ML_HEADS
    ig0, lf0, end = (H, 3 * H, 0) if backward else (0, 2 * H, L - 1)
    b_rows = cumt[lf0:lf0 + H]
    b_end = b_rows[:, end:end + 1]
    w_end = b_end - b_rows + gt[ig0:ig0 + H]
    m_loc = jnp.max(w_end, axis=1, keepdims=True)
    wk = jnp.exp2(w_end - m_loc)
    for h in range(H):
        lhs_ref[row0 + h * VT_ROWS:row0 + (h + 1) * VT_ROWS, :] = (
            vt_ref[h * VT_ROWS:(h + 1) * VT_ROWS, cs].astype(F32) * wk[h:h + 1]).astype(lhs_ref.dtype)
    return m_loc, b_end


def _pair_update(lhs_ref, row0, k_pair, p, m_loc, b_end, lo):
    r0 = row0 + 2 * p * VT_ROWS
    out = _dot(lhs_ref[r0:r0 + 2 * VT_ROWS, :], k_pair)
    hs = (2 * p, 2 * p + 1)
    return (jnp.where(lo, out[0:VT_ROWS], out[VT_ROWS:2 * VT_ROWS]),
            [m_loc[h:h + 1] for h in hs], [b_end[h:h + 1] for h in hs])


def _apply_update(s_pair, upd, m_locs, b_ends, ms, lo):
    decays, gains, m_news = [], [], []
    for e in range(2):
        m_new = jnp.maximum(b_ends[e] + ms[e], m_locs[e])
        decays.append(jnp.exp2(b_ends[e] + ms[e] - m_new))
        gains.append(jnp.exp2(m_locs[e] - m_new))
        m_news.append(m_new)
    return jnp.where(lo, decays[0], decays[1]) * s_pair + jnp.where(lo, gains[0], gains[1]) * upd, m_news


def _absorb_chunks(k_ref, vt_ref, gt_ref, s_ref, m_ref, spre_ref, mpre_ref, lhs_ref, backward, cps):
    L = ML_CHUNK
    H = ML_HEADS
    lo = _lo_lanes()
    order = list(range(cps - 1, -1, -1)) if backward else list(range(cps))
    npair = H // 2
    scal = {}
    for u in order:
        cs = slice(u * L, (u + 1) * L)
        gt = _log2_gates(gt_ref, cs)
        scal[u] = _stage_weighted_values(lhs_ref, u * H * VT_ROWS, vt_ref, cs, gt, _cum_rows(gt, backward), backward)
    updates = {}
    for u in order:
        cs = slice(u * L, (u + 1) * L)
        for p in range(npair):
            updates[u, p] = _pair_update(lhs_ref, u * H * VT_ROWS, k_ref[cs, p * LANES:(p + 1) * LANES], p,
                                         *scal[u], lo)
    for u in order:
        spre_ref[u] = s_ref[...]
        mpre_ref[u] = m_ref[...]
        for p in range(npair):
            hs = (2 * p, 2 * p + 1)
            ms = [m_ref[h:h + 1, 0:1] for h in hs]
            s_new, m_new = _apply_update(s_ref[p], *updates[u, p], ms, lo)
            s_ref[p] = s_new
            for e in range(2):
                m_ref[hs[e]:hs[e] + 1, :] = jnp.broadcast_to(m_new[e], (1, LANES))


def _ml_state_kernel(k_ref, vt_ref, gt_ref, s0_ref, m0_ref, spre_ref, mpre_ref, sfin_ref, mfin_ref, lhs_ref, *,
                     backward, cps):
    @pl.when(pl.program_id(1) == 0)
    def _():
        sfin_ref[...] = s0_ref[...]
        mfin_ref[...] = m0_ref[...]

    _absorb_chunks(k_ref, vt_ref, gt_ref, sfin_ref.at[0], mfin_ref.at[0], spre_ref.at[0], mpre_ref.at[0],
                   lhs_ref, backward, cps)


def _ml_state_scan(k, vt, gt, s0, m0, batch, t, backward, cps):
    L = ML_CHUNK
    nc = t // L
    cps = min(cps, nc)
    assert t % L == 0 and nc % cps == 0
    steps = nc // cps
    np_ = ML_HEADS // 2
    si = (lambda s: steps - 1 - s) if backward else (lambda s: s)
    st = pl.BlockSpec((1, np_, VT_ROWS, LANES), lambda b, s: (b, 0, 0, 0))
    mx = pl.BlockSpec((1, ML_HEADS, LANES), lambda b, s: (b, 0, 0))
    return pl.pallas_call(
        functools.partial(_ml_state_kernel, backward=backward, cps=cps),
        grid=(batch, steps),
        in_specs=[pl.BlockSpec((cps * L, D_CQK), lambda b, s: (b * steps + si(s), 0)),
                  pl.BlockSpec((ML_HEADS * VT_ROWS, cps * L), lambda b, s: (b, si(s))),
                  pl.BlockSpec((LANES, cps * L), lambda b, s: (b, si(s))), st, mx],
        out_specs=[pl.BlockSpec((1, cps, np_, VT_ROWS, LANES), lambda b, s: (b, si(s), 0, 0, 0)),
                   pl.BlockSpec((1, cps, ML_HEADS, LANES), lambda b, s: (b, si(s), 0, 0)), st, mx],
        out_shape=[jax.ShapeDtypeStruct((batch, nc, np_, VT_ROWS, LANES), F32),
                   jax.ShapeDtypeStruct((batch, nc, ML_HEADS, LANES), F32),
                   jax.ShapeDtypeStruct((batch, np_, VT_ROWS, LANES), F32),
                   jax.ShapeDtypeStruct((batch, ML_HEADS, LANES), F32)],
        scratch_shapes=[pltpu.VMEM((cps * ML_HEADS * VT_ROWS, L), BF16)],
        compiler_params=_cparams(("arbitrary", "arbitrary")),
        name="ml_state_bwd" if backward else "ml_state_fwd",
    )(k, vt, gt, s0, m0)


def _ml_out_kernel(q_ref, k_ref, vt_ref, gt_ref, sg_ref, sb_ref, mb_ref, s0_ref, m0_ref, ngb_ref, o_ref,
                   s_ref, m_ref, lhs_ref, *, cps):
    @pl.when(pl.program_id(1) == 0)
    def _():
        s_ref[...] = s0_ref[0]
        m_ref[...] = m0_ref[0]

    L = ML_CHUNK
    lo = _lo_lanes()
    ri = lax.broadcasted_iota(jnp.int32, (L, L), 0)
    ci = lax.broadcasted_iota(jnp.int32, (L, L), 1)
    masks = (ri <= ci, ri >= ci)
    npair = ML_HEADS // 2
    for u in range(cps):
        cs = slice(u * L, (u + 1) * L)
        gt = _log2_gates(gt_ref, cs)
        cum_f = _cum_rows(gt, False)
        cumt = (cum_f, cum_f[:, L - 1:L] - cum_f + gt)
        r_rows = jnp.concatenate([gt[0:ML_HEADS] - cumt[0][2 * ML_HEADS:3 * ML_HEADS],
                                  gt[ML_HEADS:2 * ML_HEADS] - cumt[1][3 * ML_HEADS:4 * ML_HEADS],
                                  jnp.zeros((LANES - 2 * ML_HEADS, L), F32)], axis=0)
        r_cols = r_rows.T
        lhs_row0 = u * ML_HEADS * VT_ROWS
        upd_scal = _stage_weighted_values(lhs_ref, lhs_row0, vt_ref, cs, gt, cum_f, False)
        prods = []
        for p in range(npair):
            sl = slice(p * LANES, (p + 1) * LANES)
            stack = jnp.concatenate([k_ref[cs, sl], s_ref[p].astype(BF16), sb_ref[0, u, p].astype(BF16)], axis=0)
            prods.append(_dot_nt(stack, _pair_rows(q_ref[cs, sl], lo)))
        pts, inters, mqs = [], [], []
        for h in range(ML_HEADS):
            qs = slice((h % 2) * L, (h % 2 + 1) * L)
            for d in range(2):
                ig_row, b_row, _ = _gate_rows(gt, cumt[d], h, d == 1)
                r_col = r_cols[:, d * ML_HEADS + h:d * ML_HEADS + h + 1]
                m = mb_ref[0, u, h:h + 1, 0:1] if d else m_ref[h:h + 1, 0:1]
                g_row = b_row + m
                dt = jnp.where(masks[d], b_row + r_col, NEG_INF)
                m_q = jnp.maximum(g_row, jnp.max(dt, axis=0, keepdims=True))
                pts.append((jnp.exp2(dt - m_q) * prods[h // 2][0:L, qs]).astype(BF16))
                inters.append(jnp.exp2(g_row - m_q))
                mqs.append(m_q)
        nums = [_dot(vt_ref[h * VT_ROWS:(h + 1) * VT_ROWS, cs], jnp.concatenate(pts[2 * h:2 * h + 2], axis=1))
                for h in range(ML_HEADS)]
        for h in range(ML_HEADS):
            qs = slice((h % 2) * L, (h % 2 + 1) * L)
            hsum = None
            for d in range(2):
                i = 2 * h + d
                tot = (inters[i] * prods[h // 2][L + d * VT_ROWS:L + (d + 1) * VT_ROWS, qs]
                       + nums[h][:, d * L:(d + 1) * L])
                den = tot[ML_V_DIM:ML_V_DIM + 1]
                hd = tot[0:ML_V_DIM] / jnp.maximum(jnp.abs(den), jnp.exp2(-mqs[i]))
                hsum = hd if hsum is None else hsum + hd
            hs = slice(h * ML_V_DIM, (h + 1) * ML_V_DIM)
            ms = jnp.mean(hsum * hsum, axis=0, keepdims=True)
            y = hsum * lax.rsqrt(ms + EPS) * ngb_ref[hs, :]
            o_ref[cs, hs] = (y.T.astype(BF16) * sg_ref[cs, hs]).astype(o_ref.dtype)
        for p in range(npair):
            hp = (2 * p, 2 * p + 1)
            ms = [m_ref[h:h + 1, 0:1] for h in hp]
            upd = _pair_update(lhs_ref, lhs_row0, k_ref[cs, p * LANES:(p + 1) * LANES], p, *upd_scal, lo)
            s_new, m_new = _apply_update(s_ref[p], *upd, ms, lo)
            s_ref[p] = s_new
            for e in range(2):
                m_ref[hp[e]:hp[e] + 1, :] = jnp.broadcast_to(m_new[e], (1, LANES))


def _ml_out(q, k, vt, gt, sg, s_bwd, m_bwd, s0, m0, norm_gb, batch, t, cps):
    L = ML_CHUNK
    nc = t // L
    assert t % L == 0 and nc % cps == 0
    steps = nc // cps
    np_ = ML_HEADS // 2
    tok = lambda cols: pl.BlockSpec((cps * L, cols), lambda b, s: (b * steps + s, 0))
    return pl.pallas_call(
        functools.partial(_ml_out_kernel, cps=cps),
        grid=(batch, steps),
        in_specs=[tok(D_CQK), tok(D_CQK),
                  pl.BlockSpec((ML_HEADS * VT_ROWS, cps * L), lambda b, s: (b, s)),
                  pl.BlockSpec((LANES, cps * L), lambda b, s: (b, s)), tok(D_CV),
                  pl.BlockSpec((1, cps, np_, VT_ROWS, LANES), lambda b, s: (b, s, 0, 0, 0)),
                  pl.BlockSpec((1, cps, ML_HEADS, LANES), lambda b, s: (b, s, 0, 0)),
                  pl.BlockSpec((1, np_, VT_ROWS, LANES), lambda b, s: (b, 0, 0, 0)),
                  pl.BlockSpec((1, ML_HEADS, LANES), lambda b, s: (b, 0, 0)),
                  _const_spec((D_CV, LANES))],
        out_specs=tok(D_CV),
        out_shape=jax.ShapeDtypeStruct((batch * t, D_CV), BF16),
        scratch_shapes=[pltpu.VMEM((np_, VT_ROWS, LANES), F32), pltpu.VMEM((ML_HEADS, LANES), F32),
                        pltpu.VMEM((cps * ML_HEADS * VT_ROWS, L), BF16)],
        compiler_params=_cparams(("arbitrary", "arbitrary")),
        name="ml_out",
    )(q, k, vt, gt, sg, s_bwd, m_bwd, s0, m0, norm_gb)


def _even_layer(x2, ctx2, mod3, lat_row, ctx_row, batch, t, n, norm_mix_g, norm_ffn_g, ffn_w, w_in, qg, kg,
                table, cw, cb, lg, lb, w_out, ctx_out):
    tm = min(512, t)
    tmc = min(512, batch * n)
    ql, kl, vl, conv_l = _ab_in_conv(x2, mod3, norm_mix_g, w_in, qg, kg, cw, cb, lg, lb, t, tm)
    qc, kc, vc, glu_c = _ab_in(ctx2, mod3, ctx_row, norm_mix_g, w_in, qg, kg, tmc)
    att_l = _na_attention(ql, kl, vl, kc, vc, table, batch, t, n, rps=8)
    tmf = min(FFN_BLOCK_ROWS, t)
    x2 = _mix_ffn(x2, (att_l, conv_l), (0, 0), mod3, lat_row(t // tmf), norm_ffn_g, w_out, ffn_w, tmf)
    if ctx_out:
        att_c = _ctx_attention(qc, kc, vc, batch, n)
        conv_c = _conv_module(glu_c, cw, cb, lg, lb, batch, n)
        ctx2 = _mix_ffn(ctx2, (att_c, conv_c), (0, 0), mod3, ctx_row, norm_ffn_g, w_out, ffn_w, tmc)
    return x2, ctx2


def _odd_layer(x2, ctx2, mod3, lat_row, ctx_row, batch, t, n, norm_mix_g, norm_ffn_g, ffn_w, w_in, w_vt, cw, cb, gb,
               norm_gb, w_out):
    tm = min(512, t)
    tmc = min(512, n)
    cos_l, sin_l = _rope_tables(t, True)
    cos_c, sin_c = _rope_tables(n, False)
    _, kc, vtc, _, gtc = _ml_in(ctx2, mod3, lambda b: ctx_row(b), norm_mix_g, w_in, w_vt, cw, cb, gb, cos_c,
                                sin_c, batch, n, tmc)
    s_zero = jnp.zeros((batch, ML_HEADS // 2, VT_ROWS, LANES), F32)
    m_zero = jnp.zeros((batch, ML_HEADS, LANES), F32)
    _, _, sf, mf = _ml_state_scan(kc, vtc, gtc, s_zero, m_zero, batch, n, False, cps=2)
    _, _, sb, mb = _ml_state_scan(kc, vtc, gtc, s_zero, m_zero, batch, n, True, cps=2)
    ql, kl, vtl, sgl, gtl = _ml_in(x2, mod3, lambda b: b, norm_mix_g, w_in, w_vt, cw, cb, gb, cos_l, sin_l,
                                   batch, t, tm)
    s_pre, m_pre, _, _ = _ml_state_scan(kl, vtl, gtl, sb, mb, batch, t, True, cps=8)
    gated = _ml_out(ql, kl, vtl, gtl, sgl, s_pre, m_pre, sf, mf, norm_gb, batch, t, cps=4)
    tmf = min(FFN_BLOCK_ROWS, t)
    return _mix_ffn(x2, (gated, gated), (0, 1), mod3, lat_row(t // tmf), norm_ffn_g, w_out, ffn_w, tmf)


def kernel(x, c, ctx, c_ctx, ada_w, ada_b, norm_mix_g, norm_ffn_g, ffn_w_gate, ffn_w_up, ffn_w_down, ab_w_in,
           na_q_norm_g, na_k_norm_g, na_rpb, conv_w, conv_b, conv_ln_g, conv_ln_b, ab_w_out, ml_w_in, ml_conv_w,
           ml_conv_b, ml_gate_b, ml_norm_g, ml_w_out):
    batch, t, d = x.shape
    n = ctx.shape[1]
    depth = ada_w.shape[0]
    assert batch + 1 <= MOD_ROWS and depth % 2 == 0, "odd layers are only implemented as the last-layer form"
    s_rows = jnp.zeros((MOD_ROWS, d), F32).at[:batch].set(c).at[batch].set(c_ctx)
    mod = _ada_mod(s_rows, ada_w, ada_b)
    x2 = x.reshape(batch * t, d)
    ctx2 = ctx.reshape(batch * n, d)
    lat_row = lambda tiles: (lambda i: i // tiles)
    ctx_row = lambda *_: batch
    ffn_bf16 = (ffn_w_gate.astype(BF16), ffn_w_up.astype(BF16), ffn_w_down.astype(BF16))
    rep8 = lambda v: jnp.broadcast_to(v[None, :], (SUBLANES, v.shape[0]))
    for l in range(depth):
        j = l // 2
        last = l == depth - 1
        mod3 = mod[l].reshape(MOD_ROWS, 1, 6 * d)
        ffn_w = (l,) + ffn_bf16
        nmg = norm_mix_g[l].reshape(1, d)
        nfg = norm_ffn_g[l].reshape(1, d)
        if l % 2 == 0:
            qg = (jnp.tile(na_q_norm_g[j], NA_HEADS) * (NA_HEAD_DIM ** -0.5 * LOG2E)).reshape(1, D_A)
            kg = jnp.tile(na_k_norm_g[j], NA_HEADS).reshape(1, D_A)
            x2, ctx2 = _even_layer(x2, ctx2, mod3, lat_row, ctx_row, batch, t, n, nmg, nfg, ffn_w,
                                   ab_w_in[j].astype(BF16), qg, kg, _na_bias_table(na_rpb[j] * LOG2E),
                                   jnp.broadcast_to(conv_w[j][:, None, :], (CONV_WIDTH, SUBLANES, D_B)),
                                   rep8(conv_b[j]), rep8(conv_ln_g[j]), rep8(conv_ln_b[j]),
                                   ab_w_out[j].astype(BF16), not last)
        else:
            assert last
            nqk = 2 * D_CQK
            w = ml_w_in[j]
            w_gates = jnp.pad(w[:, nqk + 2 * D_CV:], ((0, 0), (0, LANES - 4 * ML_HEADS)))
            w_in = jnp.concatenate([w[:, :nqk], w[:, nqk + D_CV:nqk + 2 * D_CV], w_gates], axis=1).astype(BF16)
            w_vt = w[:, nqk:nqk + D_CV].T.astype(BF16)
            gb = jnp.pad(ml_gate_b[j], (0, LANES - 4 * ML_HEADS)).reshape(1, LANES)
            norm_gb = jnp.broadcast_to(ml_norm_g[j][:, None], (D_CV, LANES))
            x2 = _odd_layer(x2, ctx2, mod3, lat_row, ctx_row, batch, t, n, nmg, nfg, ffn_w, w_in, w_vt,
                            ml_conv_w[j], ml_conv_b[j].reshape(1, nqk), gb, norm_gb, ml_w_out[j].astype(BF16))
    return x2.reshape(batch, t, d)
```

```python
import functools
import math

import jax
import jax.numpy as jnp
import numpy as np
from jax import lax
from jax.experimental import pallas as pl
from jax.experimental.pallas import tpu as pltpu

F32 = jnp.float32
BF16 = jnp.bfloat16

EPS = 1e-6
GRID_W = 64
NA_HEADS = 8
NA_HEAD_DIM = 64
D_A = NA_HEADS * NA_HEAD_DIM
NA_WIN_H = 8
NA_WIN_W = 16
D_B = 512
CONV_WIDTH = 31
ML_HEADS = 8
ML_QK_DIM = 64
ML_V_DIM = 128
D_CQK = ML_HEADS * ML_QK_DIM
D_CV = ML_HEADS * ML_V_DIM
ML_SHORT_CONV = 5
ML_CHUNK = 128
ROPE_BASE = 10000.0

LANES = 128
SUBLANES = 8
HALO = 16
VT_ROWS = ML_V_DIM + HALO
MOD_ROWS = 16
VMEM_LIMIT = 56 * 1024 * 1024
TILE_ROWS = 512
FFN_BLOCK_ROWS = 1024
FFN_COLS = 256
CONV_ROWS = 32
NA_ROWS_PER_STEP = 8
SCAN_CHUNKS_PER_STEP = 8
OUT_CHUNKS_PER_STEP = 4
NEG_INF = float("-inf")
LOG2E = math.log2(math.e)


def _cparams(sem):
    return pltpu.CompilerParams(dimension_semantics=sem, vmem_limit_bytes=VMEM_LIMIT)


def _const_spec(shape):
    nd = len(shape)
    return pl.BlockSpec(shape, lambda *_: (0,) * nd, pipeline_mode=pl.Buffered(1))


def _sigmoid(x):
    return 1.0 / (1.0 + jnp.exp2(x * (-LOG2E)))


def _silu(x):
    return x * _sigmoid(x)


def _modulate(x, g, shift, scale):
    ms = jnp.mean(x * x, axis=-1, keepdims=True)
    return x * lax.rsqrt(ms + EPS) * (g * (1.0 + scale)) + shift


def _dot(a, b):
    return jnp.dot(a, b, preferred_element_type=F32)


def _dot_nt(a, b):
    return lax.dot_general(a, b, (((1,), (1,)), ((), ())), preferred_element_type=F32)


def _lo_lanes():
    return lax.broadcasted_iota(jnp.int32, (1, LANES), 1) < (LANES // 2)


def _ada_kernel(s_ref, w_ref, b_ref, o_ref):
    s = _silu(s_ref[...]).astype(BF16)
    o_ref[0] = _dot(s, w_ref[0].astype(BF16)) + b_ref[0]


def _ada_mod(s_rows, ada_w, ada_b):
    depth, d, n = ada_w.shape
    tn = n // 4
    return pl.pallas_call(
        _ada_kernel,
        grid=(depth, n // tn),
        in_specs=[pl.BlockSpec((MOD_ROWS, d), lambda l, j: (0, 0)),
                  pl.BlockSpec((1, d, tn), lambda l, j: (l, 0, j)),
                  pl.BlockSpec((1, 1, tn), lambda l, j: (l, 0, j))],
        out_specs=pl.BlockSpec((1, MOD_ROWS, tn), lambda l, j: (l, 0, j)),
        out_shape=jax.ShapeDtypeStruct((depth, MOD_ROWS, n), F32),
        compiler_params=_cparams(("arbitrary", "arbitrary")),
        name="ada_mod",
    )(s_rows, ada_w, ada_b.reshape(depth, 1, n))


def _mod_spec(d, sec, row_fn):
    return pl.BlockSpec((1, 1, d), lambda *idx: (row_fn(*idx), 0, sec))


def _conv_chunk(window, base, cw_ref, cb_ref, lg_ref, lb_ref, rc):
    first = HALO - CONV_WIDTH // 2
    groups = rc // SUBLANES
    accs = [cb_ref[...]] * groups
    for w in range(CONV_WIDTH):
        s, a = (first + w) % SUBLANES, (first + w) // SUBLANES
        wv = cw_ref[w]
        for gi in range(groups):
            accs[gi] = accs[gi] + window(s, base + (a + gi) * SUBLANES, SUBLANES) * wv
    acc = jnp.concatenate(accs, axis=0)
    mu = jnp.mean(acc, axis=-1, keepdims=True)
    xc = acc - mu
    var = jnp.mean(xc * xc, axis=-1, keepdims=True)
    rep = lambda ref: jnp.concatenate([ref[...]] * groups, axis=0)
    return _silu(xc * lax.rsqrt(var + EPS) * rep(lg_ref) + rep(lb_ref))


def _head_norm(r, gain_ref, out_ref, lo):
    for p in range(D_A // LANES):
        sl = slice(p * LANES, (p + 1) * LANES)
        xp = r[:, sl]
        sq = xp * xp
        s_all = jnp.sum(sq, axis=-1, keepdims=True)
        s_lo = jnp.sum(jnp.where(lo, sq, 0.0), axis=-1, keepdims=True)
        ms = jnp.where(lo, s_lo, s_all - s_lo) * (1.0 / NA_HEAD_DIM)
        out_ref[:, sl] = (xp * lax.rsqrt(ms + EPS) * gain_ref[:, sl]).astype(out_ref.dtype)


def _ab_in_kernel(x_ref, sh_ref, sc_ref, g_ref, w_ref, qg_ref, kg_ref, q_ref, k_ref, v_ref, glu_ref):
    h = _modulate(x_ref[...], g_ref[...], sh_ref[0], sc_ref[0]).astype(BF16)
    lo = _lo_lanes()
    _head_norm(_dot(h, w_ref[:, 0:D_A]), qg_ref, q_ref, lo)
    _head_norm(_dot(h, w_ref[:, D_A:2 * D_A]), kg_ref, k_ref, lo)
    v_ref[...] = _dot(h, w_ref[:, 2 * D_A:3 * D_A]).astype(v_ref.dtype)
    u = _dot(h, w_ref[:, 3 * D_A:3 * D_A + D_B])
    gt = _dot(h, w_ref[:, 3 * D_A + D_B:3 * D_A + 2 * D_B])
    glu_ref[...] = u * _sigmoid(gt)


def _ab_in_conv_kernel(x_ref, xn_ref, sh_ref, sc_ref, g_ref, w_ref, qg_ref, kg_ref, cw_ref, cb_ref, lg_ref, lb_ref,
                       q_ref, k_ref, v_ref, conv_ref, win_ref, shift_ref, *, tm, nts, rc):
    i = pl.program_id(0)

    @pl.when(i == 0)
    def _():
        win_ref[...] = jnp.zeros_like(win_ref)

    span = tm + 2 * HALO - SUBLANES
    for s in range(SUBLANES):
        shift_ref[s, 0:span] = win_ref[s:s + span]
    prev_tail = win_ref[tm:tm + HALO]
    window = lambda s, start, size: shift_ref[s, start:start + size, :]
    for c in range(tm // rc):
        conv_ref[c * rc:(c + 1) * rc, :] = _conv_chunk(window, c * rc, cw_ref, cb_ref, lg_ref, lb_ref, rc
                                                       ).astype(conv_ref.dtype)
    g, sh, sc = g_ref[...], sh_ref[0], sc_ref[0]
    h = _modulate(x_ref[...], g, sh, sc).astype(BF16)
    h_ext = jnp.concatenate([h, _modulate(xn_ref[...], g, sh, sc).astype(BF16)], axis=0)
    lo = _lo_lanes()
    _head_norm(_dot(h, w_ref[:, 0:D_A]), qg_ref, q_ref, lo)
    _head_norm(_dot(h, w_ref[:, D_A:2 * D_A]), kg_ref, k_ref, lo)
    v_ref[...] = _dot(h, w_ref[:, 2 * D_A:3 * D_A]).astype(v_ref.dtype)
    u = _dot(h_ext, w_ref[:, 3 * D_A:3 * D_A + D_B])
    gt = _dot(h_ext, w_ref[:, 3 * D_A + D_B:3 * D_A + 2 * D_B])
    glu = u * _sigmoid(gt)
    seq_pos = lax.rem(i, nts)
    win_ref[0:HALO] = jnp.where(seq_pos == 0, 0.0, prev_tail)
    win_ref[HALO:HALO + tm] = glu[0:tm]
    win_ref[HALO + tm:2 * HALO + tm] = jnp.where(seq_pos == nts - 1, 0.0, glu[tm:tm + HALO])


def _ab_in_conv(x2, mod3, norm_g, w_in, qg, kg, cw, cb, lg, lb, t, tm):
    m, d = x2.shape
    ntiles = m // tm
    nts = t // tm
    hb = tm // HALO
    cur = lambda i: jnp.minimum(i, ntiles - 1)
    tok = lambda c: pl.BlockSpec((tm, c), lambda i: (cur(i), 0))
    row = lambda i: cur(i) // nts
    rep = lambda: _const_spec((SUBLANES, D_B))
    return pl.pallas_call(
        functools.partial(_ab_in_conv_kernel, tm=tm, nts=nts, rc=CONV_ROWS),
        grid=(ntiles + 1,),
        in_specs=[tok(d),
                  pl.BlockSpec((HALO, d), lambda i: (jnp.minimum((cur(i) + 1) * hb, m // HALO - 1), 0)),
                  _mod_spec(d, 0, row), _mod_spec(d, 1, row), _const_spec((1, d)),
                  _const_spec(w_in.shape), _const_spec((1, D_A)), _const_spec((1, D_A)),
                  _const_spec((CONV_WIDTH, SUBLANES, D_B)), rep(), rep(), rep()],
        out_specs=[tok(D_A), tok(D_A), tok(D_A),
                   pl.BlockSpec((tm, D_B), lambda i: (jnp.maximum(i - 1, 0), 0))],
        out_shape=[jax.ShapeDtypeStruct((m, D_A), BF16), jax.ShapeDtypeStruct((m, D_A), BF16),
                   jax.ShapeDtypeStruct((m, D_A), BF16), jax.ShapeDtypeStruct((m, D_B), BF16)],
        scratch_shapes=[pltpu.VMEM((tm + 2 * HALO, D_B), F32), pltpu.VMEM((SUBLANES, tm + 2 * HALO, D_B), F32)],
        compiler_params=_cparams(("arbitrary",)),
        name="ab_in_conv",
    )(x2, x2, mod3, mod3, norm_g, w_in, qg, kg, cw, cb, lg, lb)


def _ab_in(x2, mod3, row_fn, norm_g, w_in, qg, kg, tm):
    m, d = x2.shape
    tok = lambda c: pl.BlockSpec((tm, c), lambda i: (i, 0))
    return pl.pallas_call(
        _ab_in_kernel,
        grid=(m // tm,),
        in_specs=[tok(d), _mod_spec(d, 0, row_fn), _mod_spec(d, 1, row_fn), _const_spec((1, d)),
                  _const_spec(w_in.shape), _const_spec((1, D_A)), _const_spec((1, D_A))],
        out_specs=[tok(D_A), tok(D_A), tok(D_A), tok(D_B)],
        out_shape=[jax.ShapeDtypeStruct((m, D_A), BF16), jax.ShapeDtypeStruct((m, D_A), BF16),
                   jax.ShapeDtypeStruct((m, D_A), BF16), jax.ShapeDtypeStruct((m, D_B), F32)],
        compiler_params=_cparams(("arbitrary",)),
        name="ab_in",
    )(x2, mod3, mod3, norm_g, w_in, qg, kg)


def _softmax_pv(scores, values):
    m = functools.reduce(jnp.maximum, [jnp.max(s, axis=-1, keepdims=True) for s in scores])
    ps = [jnp.exp2(s - m) for s in scores]
    l = functools.reduce(jnp.add, [jnp.sum(p, axis=-1, keepdims=True) for p in ps])
    o = functools.reduce(jnp.add, [_dot(p.astype(BF16), v) for p, v in zip(ps, values)])
    return o / l


def _pair_rows(qp, lo):
    zero = jnp.zeros_like(qp)
    return jnp.concatenate([jnp.where(lo, qp, zero), jnp.where(lo, zero, qp)], axis=0)


def _na_kernel(q_ref, k_ref, v_ref, kc_ref, vc_ref, tb_ref, o_ref, *, rows, rps):
    nk = NA_WIN_H * GRID_W
    lo = _lo_lanes()
    npair = D_A // LANES
    sls = [slice(p * LANES, (p + 1) * LANES) for p in range(npair)]
    starts, rhos = [], []
    for i in range(rps):
        r = pl.program_id(1) * rps + i
        r0 = jnp.clip(r - NA_WIN_H // 2, 0, rows - NA_WIN_H)
        starts.append(pl.multiple_of(r0 * GRID_W, GRID_W))
        rhos.append(r0 - r + (NA_WIN_H - 1))
    scores = []
    for i in range(rps):
        for p in range(npair):
            q2 = _pair_rows(q_ref[i * GRID_W:(i + 1) * GRID_W, sls[p]], lo)
            bias = jnp.concatenate([tb_ref[rhos[i] + 2 * a, p] for a in range(NA_WIN_H // 2)], axis=1)
            scores.append((_dot_nt(q2, k_ref[pl.ds(starts[i], nk), sls[p]]) + bias,
                           _dot_nt(q2, kc_ref[:, sls[p]])))
    probs = []
    for s_loc, s_ctx in scores:
        m = jnp.maximum(jnp.max(s_loc, axis=-1, keepdims=True), jnp.max(s_ctx, axis=-1, keepdims=True))
        p_loc = jnp.exp2(s_loc - m)
        p_ctx = jnp.exp2(s_ctx - m)
        l = jnp.sum(p_loc, axis=-1, keepdims=True) + jnp.sum(p_ctx, axis=-1, keepdims=True)
        probs.append((p_loc.astype(BF16), p_ctx.astype(BF16), l))
    for i in range(rps):
        for p in range(npair):
            p_loc, p_ctx, l = probs[i * npair + p]
            o2 = (_dot(p_loc, v_ref[pl.ds(starts[i], nk), sls[p]]) + _dot(p_ctx, vc_ref[:, sls[p]])) / l
            o_ref[i * GRID_W:(i + 1) * GRID_W, sls[p]] = jnp.where(
                lo, o2[0:GRID_W], o2[GRID_W:2 * GRID_W]).astype(o_ref.dtype)


def _na_attention(q, k, v, kc, vc, table, batch, t, n, rps):
    rows = t // GRID_W
    assert rows >= NA_WIN_H and t % GRID_W == 0 and rows % rps == 0
    steps = rows // rps
    return pl.pallas_call(
        functools.partial(_na_kernel, rows=rows, rps=rps),
        grid=(batch, steps),
        in_specs=[pl.BlockSpec((rps * GRID_W, D_A), lambda b, s: (b * steps + s, 0)),
                  pl.BlockSpec((t, D_A), lambda b, s: (b, 0)),
                  pl.BlockSpec((t, D_A), lambda b, s: (b, 0)),
                  pl.BlockSpec((n, D_A), lambda b, s: (b, 0)),
                  pl.BlockSpec((n, D_A), lambda b, s: (b, 0)), _const_spec(table.shape)],
        out_specs=pl.BlockSpec((rps * GRID_W, D_A), lambda b, s: (b * steps + s, 0)),
        out_shape=jax.ShapeDtypeStruct((batch * t, D_A), BF16),
        compiler_params=_cparams(("arbitrary", "arbitrary")),
        name="na_attention",
    )(q, k, v, kc, vc, table)


def _ctx_attn_kernel(q_ref, k_ref, v_ref, o_ref):
    lo = _lo_lanes()
    for p in range(D_A // LANES):
        sl = slice(p * LANES, (p + 1) * LANES)
        qp = q_ref[:, sl]
        kp = k_ref[:, sl]
        vp = v_ref[:, sl]
        outs = []
        for e in range(2):
            qm = jnp.where(lo if e == 0 else jnp.logical_not(lo), qp, jnp.zeros_like(qp))
            outs.append(_softmax_pv([_dot_nt(qm, kp)], [vp]))
        o_ref[:, sl] = jnp.where(lo, outs[0], outs[1]).astype(o_ref.dtype)


def _ctx_attention(q, k, v, batch, n):
    spec = pl.BlockSpec((n, D_A), lambda b: (b, 0))
    return pl.pallas_call(
        _ctx_attn_kernel,
        grid=(batch,),
        in_specs=[spec, spec, spec],
        out_specs=spec,
        out_shape=jax.ShapeDtypeStruct((batch * n, D_A), BF16),
        compiler_params=_cparams(("arbitrary",)),
        name="ctx_attention",
    )(q, k, v)


def _na_bias_table(rpb):
    n_rel = 2 * NA_WIN_W - 1
    lead = GRID_W - NA_WIN_W
    ext = jnp.pad(rpb.astype(F32), ((0, 0), (0, 0), (lead, 2 * GRID_W - 1 - lead - n_rel)), constant_values=NEG_INF)
    band = jnp.stack([ext[:, :, GRID_W - 1 - c:2 * GRID_W - 1 - c] for c in range(GRID_W)], axis=2)
    col = np.arange(GRID_W)
    c0 = np.clip(col - NA_WIN_W // 2, 0, GRID_W - NA_WIN_W)
    inside = (col[None, :] >= c0[:, None]) & (col[None, :] < c0[:, None] + NA_WIN_W)
    band = jnp.where(inside[None, None], band, NEG_INF)
    two = jnp.concatenate([band[:, :-1], band[:, 1:]], axis=-1)
    two = two.reshape(NA_HEADS // 2, 2, 2 * NA_WIN_H - 2, GRID_W, 2 * GRID_W).transpose(2, 0, 1, 3, 4)
    return two.reshape(2 * NA_WIN_H - 2, NA_HEADS // 2, 2 * GRID_W, 2 * GRID_W)


def _conv_kernel(prev_ref, x_ref, next_ref, cw_ref, cb_ref, lg_ref, lb_ref, o_ref, xs_ref, *, tt, nt, rc):
    i = pl.program_id(1)
    xs_ref[0, 0:HALO] = jnp.where(i > 0, prev_ref[...], 0.0)
    xs_ref[0, HALO:HALO + tt] = x_ref[...]
    xs_ref[0, HALO + tt:2 * HALO + tt] = jnp.where(i < nt - 1, next_ref[...], 0.0)
    span = tt + 2 * HALO - SUBLANES
    for s in range(1, SUBLANES):
        xs_ref[s, 0:span] = xs_ref[0, s:s + span]
    window = lambda s, start, size: xs_ref[s, pl.ds(start, size), :]

    def chunk(c, carry):
        base = pl.multiple_of(c * rc, rc)
        o_ref[pl.ds(base, rc), :] = _conv_chunk(window, base, cw_ref, cb_ref, lg_ref, lb_ref, rc).astype(o_ref.dtype)
        return carry

    lax.fori_loop(0, tt // rc, chunk, 0, unroll=4)


def _conv_module(glu, cw, cb, lg, lb, batch, t):
    tt = min(t, TILE_ROWS)
    nt = t // tt
    hb = tt // HALO
    nhb = t // HALO
    return pl.pallas_call(
        functools.partial(_conv_kernel, tt=tt, nt=nt, rc=CONV_ROWS),
        grid=(batch, nt),
        in_specs=[pl.BlockSpec((HALO, D_B), lambda b, i: (b * nhb + jnp.maximum(i * hb - 1, 0), 0)),
                  pl.BlockSpec((tt, D_B), lambda b, i: (b * nt + i, 0)),
                  pl.BlockSpec((HALO, D_B), lambda b, i: (b * nhb + jnp.minimum((i + 1) * hb, nhb - 1), 0)),
                  _const_spec((CONV_WIDTH, SUBLANES, D_B)), _const_spec((SUBLANES, D_B)),
                  _const_spec((SUBLANES, D_B)), _const_spec((SUBLANES, D_B))],
        out_specs=pl.BlockSpec((tt, D_B), lambda b, i: (b * nt + i, 0)),
        out_shape=jax.ShapeDtypeStruct((batch * t, D_B), BF16),
        scratch_shapes=[pltpu.VMEM((SUBLANES, tt + 2 * HALO, D_B), F32)],
        compiler_params=_cparams(("arbitrary", "arbitrary")),
        name="conv_module",
    )(glu, glu, glu, cw, cb, lg, lb)


def _mix_ffn_kernel(x_ref, a1_ref, a2_ref, g1_ref, sh_ref, sc_ref, g2_ref, ng_ref,
                    wo_ref, wg_ref, wu_ref, wd_ref, o_ref, acc_ref, *, tf, sub):
    ka = a1_ref.shape[1]
    d_ff = wg_ref.shape[1]
    tiles = [slice(s * sub, (s + 1) * sub) for s in range(x_ref.shape[0] // sub)]
    hs = []
    for rs in tiles:
        mix = _dot(a1_ref[rs, :], wo_ref[0:ka, :]) + _dot(a2_ref[rs, :], wo_ref[ka:2 * ka, :])
        x1 = x_ref[rs, :] + g1_ref[0] * mix
        o_ref[rs, :] = x1
        hs.append(_modulate(x1, ng_ref[...], sh_ref[0], sc_ref[0]).astype(BF16))
    for rs, h in zip(tiles, hs):
        for c in range(d_ff // tf):
            sl = slice(c * tf, (c + 1) * tf)
            act = (_silu(_dot(h, wg_ref[:, sl])) * _dot(h, wu_ref[:, sl])).astype(BF16)
            part = _dot(act, wd_ref[sl, :])
            if c == 0:
                acc_ref[rs, :] = part
            else:
                acc_ref[rs, :] += part
        o_ref[rs, :] = o_ref[rs, :] + g2_ref[0] * acc_ref[rs, :]


def _mix_ffn(x2, a, a_cols, mod3, row_fn, norm_g, w_out, ffn_w, tm):
    m, d = x2.shape
    a1, a2 = a
    layer, w_gate, w_up, w_down = ffn_w
    ka = w_out.shape[0] // 2
    tok = lambda c, j=0: pl.BlockSpec((tm, c), lambda i: (i, j))
    layer_spec = lambda w: pl.BlockSpec((None,) + w.shape[1:], lambda i: (layer, 0, 0), pipeline_mode=pl.Buffered(1))
    return pl.pallas_call(
        functools.partial(_mix_ffn_kernel, tf=FFN_COLS, sub=min(tm, TILE_ROWS)),
        grid=(m // tm,),
        in_specs=[tok(d), tok(ka, a_cols[0]), tok(ka, a_cols[1]),
                  _mod_spec(d, 2, row_fn), _mod_spec(d, 3, row_fn), _mod_spec(d, 4, row_fn),
                  _mod_spec(d, 5, row_fn), _const_spec((1, d)),
                  _const_spec(w_out.shape), layer_spec(w_gate), layer_spec(w_up), layer_spec(w_down)],
        out_specs=tok(d),
        out_shape=jax.ShapeDtypeStruct((m, d), F32),
        scratch_shapes=[pltpu.VMEM((tm, d), F32)],
        compiler_params=_cparams(("arbitrary",)),
        name="mix_ffn",
    )(x2, a1, a2, mod3, mod3, mod3, mod3, norm_g, w_out, w_gate, w_up, w_down)


def _ml_in_kernel(xp_ref, x_ref, xn_ref, sh_ref, sc_ref, g_ref, w_ref, wvt_ref, cw_ref, cb_ref, gb_ref, cos_ref,
                  sin_ref, q_ref, k_ref, vt_ref, sg_ref, gatest_ref, hs_ref, r_ref, *, tm, nt):
    i = pl.program_id(1)
    g, sh, sc = g_ref[...], sh_ref[0], sc_ref[0]
    hs_ref[0:HALO] = jnp.where(i > 0, _modulate(xp_ref[...], g, sh, sc), 0.0).astype(BF16)
    hs_ref[HALO:HALO + tm] = _modulate(x_ref[...], g, sh, sc).astype(BF16)
    hs_ref[HALO + tm:2 * HALO + tm] = jnp.where(i < nt - 1, _modulate(xn_ref[...], g, sh, sc), 0.0).astype(BF16)
    nqk = 2 * D_CQK
    lane = lax.broadcasted_iota(jnp.int32, (1, LANES), 1)
    r_ref[...] = _dot(hs_ref[...], w_ref[:, 0:nqk])
    hm = hs_ref[HALO:HALO + tm]
    vt = _dot_nt(wvt_ref[...], hm).astype(vt_ref.dtype)
    ones_rows = (lax.broadcasted_iota(jnp.int32, (VT_ROWS - ML_V_DIM, tm), 0) == 0).astype(vt_ref.dtype)
    for h in range(ML_HEADS):
        vt_ref[h * VT_ROWS:h * VT_ROWS + ML_V_DIM, :] = vt[h * ML_V_DIM:(h + 1) * ML_V_DIM]
        vt_ref[h * VT_ROWS + ML_V_DIM:(h + 1) * VT_ROWS, :] = ones_rows
    o_pre = _dot(hm, w_ref[:, nqk:nqk + D_CV])
    gt = _dot(hm, w_ref[:, nqk + D_CV:nqk + D_CV + LANES]) + gb_ref[...]
    sg_ref[...] = _sigmoid(o_pre).astype(sg_ref.dtype)
    log_sig = jnp.minimum(gt, 0.0) - jnp.log(1.0 + jnp.exp(-jnp.abs(gt)))
    gates = jnp.where(lane < 2 * ML_HEADS, gt, log_sig)
    gatest_ref[...] = gates.T
    pad = ML_SHORT_CONV // 2
    rows = tm + 2 * HALO
    qk_parts = []
    for j in range(nqk // LANES):
        rj = r_ref[:, j * LANES:(j + 1) * LANES]
        acc = jnp.broadcast_to(cb_ref[:, j * LANES:(j + 1) * LANES], (tm, LANES))
        for w in range(ML_SHORT_CONV):
            shifted = rj if w == pad else pltpu.roll(rj, (pad - w) % rows, 0)
            acc = acc + shifted[HALO:HALO + tm] * cw_ref[w:w + 1, j * LANES:(j + 1) * LANES]
        qk_parts.append(_silu(acc))
    cos = cos_ref[...]
    sin = sin_ref[...]
    first = (lane & 16) == 0
    for j in range(nqk // LANES):
        xg = qk_parts[j]
        sw = jnp.where(first, pltpu.roll(xg, LANES - 16, 1), pltpu.roll(xg, 16, 1))
        y = xg * cos + sw * sin
        if j < D_CQK // LANES:
            q_ref[:, j * LANES:(j + 1) * LANES] = (y * ML_QK_DIM ** -0.5).astype(q_ref.dtype)
        else:
            jj = j - D_CQK // LANES
            k_ref[:, jj * LANES:(jj + 1) * LANES] = y.astype(k_ref.dtype)


def _ml_in(x2, mod3, row_fn, norm_g, w_in, w_vt, cw, cb, gb, cos_t, sin_t, batch, t, tm):
    m, d = x2.shape
    nt = t // tm
    hb = tm // HALO
    nhb = t // HALO
    nqk = 2 * D_CQK
    tok = lambda c: pl.BlockSpec((tm, c), lambda b, i: (b * nt + i, 0))
    rf = lambda b, i: row_fn(b)
    return pl.pallas_call(
        functools.partial(_ml_in_kernel, tm=tm, nt=nt),
        grid=(batch, nt),
        in_specs=[pl.BlockSpec((HALO, d), lambda b, i: (b * nhb + jnp.maximum(i * hb - 1, 0), 0)),
                  tok(d),
                  pl.BlockSpec((HALO, d), lambda b, i: (b * nhb + jnp.minimum((i + 1) * hb, nhb - 1), 0)),
                  _mod_spec(d, 0, rf), _mod_spec(d, 1, rf), _const_spec((1, d)),
                  _const_spec(w_in.shape), _const_spec(w_vt.shape), _const_spec((ML_SHORT_CONV, nqk)),
                  _const_spec((1, nqk)), _const_spec((1, LANES)),
                  pl.BlockSpec((tm, LANES), lambda b, i: (i, 0)),
                  pl.BlockSpec((tm, LANES), lambda b, i: (i, 0))],
        out_specs=[tok(D_CQK), tok(D_CQK), pl.BlockSpec((ML_HEADS * VT_ROWS, tm), lambda b, i: (b, i)),
                   tok(D_CV), pl.BlockSpec((LANES, tm), lambda b, i: (b, i))],
        out_shape=[jax.ShapeDtypeStruct((m, D_CQK), BF16), jax.ShapeDtypeStruct((m, D_CQK), BF16),
                   jax.ShapeDtypeStruct((batch * ML_HEADS * VT_ROWS, t), BF16),
                   jax.ShapeDtypeStruct((m, D_CV), BF16),
                   jax.ShapeDtypeStruct((batch * LANES, t), F32)],
        scratch_shapes=[pltpu.VMEM((tm + 2 * HALO, d), BF16), pltpu.VMEM((tm + 2 * HALO, nqk), F32)],
        compiler_params=_cparams(("arbitrary", "arbitrary")),
        name="ml_in",
    )(x2, x2, x2, mod3, mod3, norm_g, w_in, w_vt, cw, cb, gb, cos_t, sin_t)


def _rope_tables(t, use_rope):
    if not use_rope:
        return jnp.ones((t, LANES), F32), jnp.zeros((t, LANES), F32)
    in_head = np.arange(LANES) % ML_QK_DIM
    nf = ML_QK_DIM // 4
    inv = ROPE_BASE ** (-np.arange(nf, dtype=np.float64) / nf)
    freq = inv[in_head % nf]
    sign = np.where((in_head // nf) % 2 == 0, -1.0, 1.0)
    tok = np.arange(t)
    pos = np.where((in_head // (2 * nf))[None, :] == 0, (tok // GRID_W)[:, None], (tok % GRID_W)[:, None])
    ang = pos.astype(np.float64) * freq[None, :]
    return jnp.asarray(np.cos(ang), F32), jnp.asarray(np.sin(ang) * sign[None, :], F32)


N_GATES = 4 * ML_HEADS


def _log2_gates(gt_ref, cs):
    return gt_ref[0:N_GATES, cs] * LOG2E


def _cum_rows(x, backward):
    L = x.shape[1]
    hi = x.astype(BF16)
    r1 = x - hi.astype(F32)
    mid = r1.astype(BF16)
    lo = (r1 - mid.astype(F32)).astype(BF16)
    ri = lax.broadcasted_iota(jnp.int32, (L, L), 0)
    ci = lax.broadcasted_iota(jnp.int32, (L, L), 1)
    mat = ((ri >= ci) if backward else (ri <= ci)).astype(F32).astype(BF16)
    out = _dot(jnp.concatenate([hi, mid, lo], axis=0), mat)
    return out[0:N_GATES] + out[N_GATES:2 * N_GATES] + out[2 * N_GATES:3 * N_GATES]


def _gate_rows(gt, cumt, h, backward):
    L = gt.shape[1]
    ig = (ML_HEADS if backward else 0) + h
    lf = (3 * ML_HEADS if backward else 2 * ML_HEADS) + h
    end = 0 if backward else L - 1
    return gt[ig:ig + 1, :], cumt[lf:lf + 1, :], cumt[lf:lf + 1, end:end + 1]


def _stage_weighted_values(lhs_ref, row0, vt_ref, cs, gt, cumt, backward):
    L = gt.shape[1]
    H = ML_HEADS
    ig0, lf0, end = (H, 3 * H, 0) if backward else (0, 2 * H, L - 1)
    b_rows = cumt[lf0:lf0 + H]
    b_end = b_rows[:, end:end + 1]
    w_end = b_end - b_rows + gt[ig0:ig0 + H]
    m_loc = jnp.max(w_end, axis=1, keepdims=True)
    wk = jnp.exp2(w_end - m_loc)
    for h in range(H):
        lhs_ref[row0 + h * VT_ROWS:row0 + (h + 1) * VT_ROWS, :] = (
            vt_ref[h * VT_ROWS:(h + 1) * VT_ROWS, cs].astype(F32) * wk[h:h + 1]).astype(lhs_ref.dtype)
    return m_loc, b_end


def _pair_update(lhs_ref, row0, k_pair, p, m_loc, b_end, lo):
    r0 = row0 + 2 * p * VT_ROWS
    out = _dot(lhs_ref[r0:r0 + 2 * VT_ROWS, :], k_pair)
    hs = (2 * p, 2 * p + 1)
    return (jnp.where(lo, out[0:VT_ROWS], out[VT_ROWS:2 * VT_ROWS]),
            [m_loc[h:h + 1] for h in hs], [b_end[h:h + 1] for h in hs])


def _apply_update(s_pair, upd, m_locs, b_ends, ms, lo):
    decays, gains, m_news = [], [], []
    for e in range(2):
        m_new = jnp.maximum(b_ends[e] + ms[e], m_locs[e])
        decays.append(jnp.exp2(b_ends[e] + ms[e] - m_new))
        gains.append(jnp.exp2(m_locs[e] - m_new))
        m_news.append(m_new)
    return jnp.where(lo, decays[0], decays[1]) * s_pair + jnp.where(lo, gains[0], gains[1]) * upd, m_news


def _absorb_chunks(k_ref, vt_ref, gt_ref, s_ref, m_ref, spre_ref, mpre_ref, lhs_ref, backward, cps):
    L = ML_CHUNK
    H = ML_HEADS
    lo = _lo_lanes()
    order = list(range(cps - 1, -1, -1)) if backward else list(range(cps))
    npair = H // 2
    scal = {}
    for u in order:
        cs = slice(u * L, (u + 1) * L)
        gt = _log2_gates(gt_ref, cs)
        scal[u] = _stage_weighted_values(lhs_ref, u * H * VT_ROWS, vt_ref, cs, gt, _cum_rows(gt, backward), backward)
    updates = {}
    for u in order:
        cs = slice(u * L, (u + 1) * L)
        for p in range(npair):
            updates[u, p] = _pair_update(lhs_ref, u * H * VT_ROWS, k_ref[cs, p * LANES:(p + 1) * LANES], p,
                                         *scal[u], lo)
    for u in order:
        spre_ref[u] = s_ref[...]
        mpre_ref[u] = m_ref[...]
        for p in range(npair):
            hs = (2 * p, 2 * p + 1)
            ms = [m_ref[h:h + 1, 0:1] for h in hs]
            s_new, m_new = _apply_update(s_ref[p], *updates[u, p], ms, lo)
            s_ref[p] = s_new
            for e in range(2):
                m_ref[hs[e]:hs[e] + 1, :] = jnp.broadcast_to(m_new[e], (1, LANES))


def _ml_state_kernel(k_ref, vt_ref, gt_ref, s0_ref, m0_ref, spre_ref, mpre_ref, sfin_ref, mfin_ref, lhs_ref, *,
                     backward, cps):
    @pl.when(pl.program_id(1) == 0)
    def _():
        sfin_ref[...] = s0_ref[...]
        mfin_ref[...] = m0_ref[...]

    _absorb_chunks(k_ref, vt_ref, gt_ref, sfin_ref.at[0], mfin_ref.at[0], spre_ref.at[0], mpre_ref.at[0],
                   lhs_ref, backward, cps)


def _ml_state_scan(k, vt, gt, s0, m0, batch, t, backward, cps):
    L = ML_CHUNK
    nc = t // L
    cps = min(cps, nc)
    assert t % L == 0 and nc % cps == 0
    steps = nc // cps
    np_ = ML_HEADS // 2
    si = (lambda s: steps - 1 - s) if backward else (lambda s: s)
    st = pl.BlockSpec((1, np_, VT_ROWS, LANES), lambda b, s: (b, 0, 0, 0))
    mx = pl.BlockSpec((1, ML_HEADS, LANES), lambda b, s: (b, 0, 0))
    return pl.pallas_call(
        functools.partial(_ml_state_kernel, backward=backward, cps=cps),
        grid=(batch, steps),
        in_specs=[pl.BlockSpec((cps * L, D_CQK), lambda b, s: (b * steps + si(s), 0)),
                  pl.BlockSpec((ML_HEADS * VT_ROWS, cps * L), lambda b, s: (b, si(s))),
                  pl.BlockSpec((LANES, cps * L), lambda b, s: (b, si(s))), st, mx],
        out_specs=[pl.BlockSpec((1, cps, np_, VT_ROWS, LANES), lambda b, s: (b, si(s), 0, 0, 0)),
                   pl.BlockSpec((1, cps, ML_HEADS, LANES), lambda b, s: (b, si(s), 0, 0)), st, mx],
        out_shape=[jax.ShapeDtypeStruct((batch, nc, np_, VT_ROWS, LANES), F32),
                   jax.ShapeDtypeStruct((batch, nc, ML_HEADS, LANES), F32),
                   jax.ShapeDtypeStruct((batch, np_, VT_ROWS, LANES), F32),
                   jax.ShapeDtypeStruct((batch, ML_HEADS, LANES), F32)],
        scratch_shapes=[pltpu.VMEM((cps * ML_HEADS * VT_ROWS, L), BF16)],
        compiler_params=_cparams(("arbitrary", "arbitrary")),
        name="ml_state_bwd" if backward else "ml_state_fwd",
    )(k, vt, gt, s0, m0)


def _ml_out_kernel(q_ref, k_ref, vt_ref, gt_ref, sg_ref, sb_ref, mb_ref, s0_ref, m0_ref, ngb_ref, o_ref,
                   s_ref, m_ref, lhs_ref, *, cps):
    @pl.when(pl.program_id(1) == 0)
    def _():
        s_ref[...] = s0_ref[0]
        m_ref[...] = m0_ref[0]

    L = ML_CHUNK
    lo = _lo_lanes()
    ri = lax.broadcasted_iota(jnp.int32, (L, L), 0)
    ci = lax.broadcasted_iota(jnp.int32, (L, L), 1)
    masks = (ri <= ci, ri >= ci)
    npair = ML_HEADS // 2
    for u in range(cps):
        cs = slice(u * L, (u + 1) * L)
        gt = _log2_gates(gt_ref, cs)
        cum_f = _cum_rows(gt, False)
        cumt = (cum_f, cum_f[:, L - 1:L] - cum_f + gt)
        r_rows = jnp.concatenate([gt[0:ML_HEADS] - cumt[0][2 * ML_HEADS:3 * ML_HEADS],
                                  gt[ML_HEADS:2 * ML_HEADS] - cumt[1][3 * ML_HEADS:4 * ML_HEADS],
                                  jnp.zeros((LANES - 2 * ML_HEADS, L), F32)], axis=0)
        r_cols = r_rows.T
        lhs_row0 = u * ML_HEADS * VT_ROWS
        upd_scal = _stage_weighted_values(lhs_ref, lhs_row0, vt_ref, cs, gt, cum_f, False)
        prods = []
        for p in range(npair):
            sl = slice(p * LANES, (p + 1) * LANES)
            stack = jnp.concatenate([k_ref[cs, sl], s_ref[p].astype(BF16), sb_ref[0, u, p].astype(BF16)], axis=0)
            prods.append(_dot_nt(stack, _pair_rows(q_ref[cs, sl], lo)))
        pts, inters, mqs = [], [], []
        for h in range(ML_HEADS):
            qs = slice((h % 2) * L, (h % 2 + 1) * L)
            for d in range(2):
                ig_row, b_row, _ = _gate_rows(gt, cumt[d], h, d == 1)
                r_col = r_cols[:, d * ML_HEADS + h:d * ML_HEADS + h + 1]
                m = mb_ref[0, u, h:h + 1, 0:1] if d else m_ref[h:h + 1, 0:1]
                g_row = b_row + m
                dt = jnp.where(masks[d], b_row + r_col, NEG_INF)
                m_q = jnp.maximum(g_row, jnp.max(dt, axis=0, keepdims=True))
                pts.append((jnp.exp2(dt - m_q) * prods[h // 2][0:L, qs]).astype(BF16))
                inters.append(jnp.exp2(g_row - m_q))
                mqs.append(m_q)
        nums = [_dot(vt_ref[h * VT_ROWS:(h + 1) * VT_ROWS, cs], jnp.concatenate(pts[2 * h:2 * h + 2], axis=1))
                for h in range(ML_HEADS)]
        for h in range(ML_HEADS):
            qs = slice((h % 2) * L, (h % 2 + 1) * L)
            hsum = None
            for d in range(2):
                i = 2 * h + d
                tot = (inters[i] * prods[h // 2][L + d * VT_ROWS:L + (d + 1) * VT_ROWS, qs]
                       + nums[h][:, d * L:(d + 1) * L])
                den = tot[ML_V_DIM:ML_V_DIM + 1]
                hd = tot[0:ML_V_DIM] / jnp.maximum(jnp.abs(den), jnp.exp2(-mqs[i]))
                hsum = hd if hsum is None else hsum + hd
            hs = slice(h * ML_V_DIM, (h + 1) * ML_V_DIM)
            ms = jnp.mean(hsum * hsum, axis=0, keepdims=True)
            y = hsum * lax.rsqrt(ms + EPS) * ngb_ref[hs, :]
            o_ref[cs, hs] = (y.T.astype(BF16) * sg_ref[cs, hs]).astype(o_ref.dtype)
        for p in range(npair):
            hp = (2 * p, 2 * p + 1)
            ms = [m_ref[h:h + 1, 0:1] for h in hp]
            upd = _pair_update(lhs_ref, lhs_row0, k_ref[cs, p * LANES:(p + 1) * LANES], p, *upd_scal, lo)
            s_new, m_new = _apply_update(s_ref[p], *upd, ms, lo)
            s_ref[p] = s_new
            for e in range(2):
                m_ref[hp[e]:hp[e] + 1, :] = jnp.broadcast_to(m_new[e], (1, LANES))


def _ml_out(q, k, vt, gt, sg, s_bwd, m_bwd, s0, m0, norm_gb, batch, t, cps):
    L = ML_CHUNK
    nc = t // L
    assert t % L == 0 and nc % cps == 0
    steps = nc // cps
    np_ = ML_HEADS // 2
    tok = lambda cols: pl.BlockSpec((cps * L, cols), lambda b, s: (b * steps + s, 0))
    return pl.pallas_call(
        functools.partial(_ml_out_kernel, cps=cps),
        grid=(batch, steps),
        in_specs=[tok(D_CQK), tok(D_CQK),
                  pl.BlockSpec((ML_HEADS * VT_ROWS, cps * L), lambda b, s: (b, s)),
                  pl.BlockSpec((LANES, cps * L), lambda b, s: (b, s)), tok(D_CV),
                  pl.BlockSpec((1, cps, np_, VT_ROWS, LANES), lambda b, s: (b, s, 0, 0, 0)),
                  pl.BlockSpec((1, cps, ML_HEADS, LANES), lambda b, s: (b, s, 0, 0)),
                  pl.BlockSpec((1, np_, VT_ROWS, LANES), lambda b, s: (b, 0, 0, 0)),
                  pl.BlockSpec((1, ML_HEADS, LANES), lambda b, s: (b, 0, 0)),
                  _const_spec((D_CV, LANES))],
        out_specs=tok(D_CV),
        out_shape=jax.ShapeDtypeStruct((batch * t, D_CV), BF16),
        scratch_shapes=[pltpu.VMEM((np_, VT_ROWS, LANES), F32), pltpu.VMEM((ML_HEADS, LANES), F32),
                        pltpu.VMEM((cps * ML_HEADS * VT_ROWS, L), BF16)],
        compiler_params=_cparams(("arbitrary", "arbitrary")),
        name="ml_out",
    )(q, k, vt, gt, sg, s_bwd, m_bwd, s0, m0, norm_gb)


def _even_layer(x2, ctx2, mod3, lat_row, ctx_row, batch, t, n, norm_mix_g, norm_ffn_g, ffn_w, w_in, qg, kg,
                table, cw, cb, lg, lb, w_out, ctx_out):
    tm = min(TILE_ROWS, t)
    tmc = min(TILE_ROWS, batch * n)
    ql, kl, vl, conv_l = _ab_in_conv(x2, mod3, norm_mix_g, w_in, qg, kg, cw, cb, lg, lb, t, tm)
    qc, kc, vc, glu_c = _ab_in(ctx2, mod3, ctx_row, norm_mix_g, w_in, qg, kg, tmc)
    att_l = _na_attention(ql, kl, vl, kc, vc, table, batch, t, n, rps=NA_ROWS_PER_STEP)
    tmf = min(FFN_BLOCK_ROWS, t)
    x2 = _mix_ffn(x2, (att_l, conv_l), (0, 0), mod3, lat_row(t // tmf), norm_ffn_g, w_out, ffn_w, tmf)
    if ctx_out:
        att_c = _ctx_attention(qc, kc, vc, batch, n)
        conv_c = _conv_module(glu_c, cw, cb, lg, lb, batch, n)
        ctx2 = _mix_ffn(ctx2, (att_c, conv_c), (0, 0), mod3, ctx_row, norm_ffn_g, w_out, ffn_w, tmc)
    return x2, ctx2


def _odd_layer(x2, ctx2, mod3, lat_row, ctx_row, batch, t, n, norm_mix_g, norm_ffn_g, ffn_w, w_in, w_vt, cw, cb, gb,
               norm_gb, w_out):
    tm = min(TILE_ROWS, t)
    tmc = min(TILE_ROWS, n)
    cos_l, sin_l = _rope_tables(t, True)
    cos_c, sin_c = _rope_tables(n, False)
    _, kc, vtc, _, gtc = _ml_in(ctx2, mod3, lambda b: ctx_row(b), norm_mix_g, w_in, w_vt, cw, cb, gb, cos_c,
                                sin_c, batch, n, tmc)
    s_zero = jnp.zeros((batch, ML_HEADS // 2, VT_ROWS, LANES), F32)
    m_zero = jnp.zeros((batch, ML_HEADS, LANES), F32)
    _, _, sf, mf = _ml_state_scan(kc, vtc, gtc, s_zero, m_zero, batch, n, False, cps=SCAN_CHUNKS_PER_STEP)
    _, _, sb, mb = _ml_state_scan(kc, vtc, gtc, s_zero, m_zero, batch, n, True, cps=SCAN_CHUNKS_PER_STEP)
    ql, kl, vtl, sgl, gtl = _ml_in(x2, mod3, lambda b: b, norm_mix_g, w_in, w_vt, cw, cb, gb, cos_l, sin_l,
                                   batch, t, tm)
    s_pre, m_pre, _, _ = _ml_state_scan(kl, vtl, gtl, sb, mb, batch, t, True, cps=SCAN_CHUNKS_PER_STEP)
    gated = _ml_out(ql, kl, vtl, gtl, sgl, s_pre, m_pre, sf, mf, norm_gb, batch, t, cps=OUT_CHUNKS_PER_STEP)
    tmf = min(FFN_BLOCK_ROWS, t)
    return _mix_ffn(x2, (gated, gated), (0, 1), mod3, lat_row(t // tmf), norm_ffn_g, w_out, ffn_w, tmf)


def kernel(x, c, ctx, c_ctx, ada_w, ada_b, norm_mix_g, norm_ffn_g, ffn_w_gate, ffn_w_up, ffn_w_down, ab_w_in,
           na_q_norm_g, na_k_norm_g, na_rpb, conv_w, conv_b, conv_ln_g, conv_ln_b, ab_w_out, ml_w_in, ml_conv_w,
           ml_conv_b, ml_gate_b, ml_norm_g, ml_w_out):
    batch, t, d = x.shape
    n = ctx.shape[1]
    depth = ada_w.shape[0]
    assert batch + 1 <= MOD_ROWS and depth % 2 == 0, "odd layers are only implemented as the last-layer form"
    s_rows = jnp.zeros((MOD_ROWS, d), F32).at[:batch].set(c).at[batch].set(c_ctx)
    mod = _ada_mod(s_rows, ada_w, ada_b)
    x2 = x.reshape(batch * t, d)
    ctx2 = ctx.reshape(batch * n, d)
    lat_row = lambda tiles: (lambda i: i // tiles)
    ctx_row = lambda *_: batch
    ffn_bf16 = (ffn_w_gate.astype(BF16), ffn_w_up.astype(BF16), ffn_w_down.astype(BF16))
    rep8 = lambda v: jnp.broadcast_to(v[None, :], (SUBLANES, v.shape[0]))
    for l in range(depth):
        j = l // 2
        last = l == depth - 1
        mod3 = mod[l].reshape(MOD_ROWS, 1, 6 * d)
        ffn_w = (l,) + ffn_bf16
        nmg = norm_mix_g[l].reshape(1, d)
        nfg = norm_ffn_g[l].reshape(1, d)
        if l % 2 == 0:
            qg = (jnp.tile(na_q_norm_g[j], NA_HEADS) * (NA_HEAD_DIM ** -0.5 * LOG2E)).reshape(1, D_A)
            kg = jnp.tile(na_k_norm_g[j], NA_HEADS).reshape(1, D_A)
            x2, ctx2 = _even_layer(x2, ctx2, mod3, lat_row, ctx_row, batch, t, n, nmg, nfg, ffn_w,
                                   ab_w_in[j].astype(BF16), qg, kg, _na_bias_table(na_rpb[j] * LOG2E),
                                   jnp.broadcast_to(conv_w[j][:, None, :], (CONV_WIDTH, SUBLANES, D_B)),
                                   rep8(conv_b[j]), rep8(conv_ln_g[j]), rep8(conv_ln_b[j]),
                                   ab_w_out[j].astype(BF16), not last)
        else:
            assert last
            nqk = 2 * D_CQK
            w = ml_w_in[j]
            w_gates = jnp.pad(w[:, nqk + 2 * D_CV:], ((0, 0), (0, LANES - 4 * ML_HEADS)))
            w_in = jnp.concatenate([w[:, :nqk], w[:, nqk + D_CV:nqk + 2 * D_CV], w_gates], axis=1).astype(BF16)
            w_vt = w[:, nqk:nqk + D_CV].T.astype(BF16)
            gb = jnp.pad(ml_gate_b[j], (0, LANES - 4 * ML_HEADS)).reshape(1, LANES)
            norm_gb = jnp.broadcast_to(ml_norm_g[j][:, None], (D_CV, LANES))
            x2 = _odd_layer(x2, ctx2, mod3, lat_row, ctx_row, batch, t, n, nmg, nfg, ffn_w, w_in, w_vt,
                            ml_conv_w[j], ml_conv_b[j].reshape(1, nqk), gb, norm_gb, ml_w_out[j].astype(BF16))
    return x2.reshape(batch, t, d)
```

```python
import functools
import math

import jax
import jax.numpy as jnp
import numpy as np
from jax import lax
from jax.experimental import pallas as pl
from jax.experimental.pallas import tpu as pltpu

F32 = jnp.float32
BF16 = jnp.bfloat16

EPS = 1e-6
GRID_W = 64
NA_HEADS = 8
NA_HEAD_DIM = 64
D_A = NA_HEADS * NA_HEAD_DIM
NA_WIN_H = 8
NA_WIN_W = 16
D_B = 512
CONV_WIDTH = 31
ML_HEADS = 8
ML_QK_DIM = 64
ML_V_DIM = 128
D_CQK = ML_HEADS * ML_QK_DIM
D_CV = ML_HEADS * ML_V_DIM
ML_SHORT_CONV = 5
ML_CHUNK = 128
ROPE_BASE = 10000.0

LANES = 128
SUBLANES = 8
HALO = 16
VT_ROWS = ML_V_DIM + HALO
MOD_ROWS = 16
VMEM_LIMIT = 56 * 1024 * 1024
TILE_ROWS = 512
FFN_BLOCK_ROWS = 1024
FFN_COLS = 256
ML_IN_ROWS = 1024
CONV_ROWS = 32
NA_ROWS_PER_STEP = 8
SCAN_CHUNKS_PER_STEP = 8
OUT_CHUNKS_PER_STEP = 4
NEG_INF = float("-inf")
LOG2E = math.log2(math.e)


def _cparams(sem):
    return pltpu.CompilerParams(dimension_semantics=sem, vmem_limit_bytes=VMEM_LIMIT)


def _const_spec(shape):
    nd = len(shape)
    return pl.BlockSpec(shape, lambda *_: (0,) * nd, pipeline_mode=pl.Buffered(1))


def _sigmoid(x):
    return 1.0 / (1.0 + jnp.exp2(x * (-LOG2E)))


def _silu(x):
    return x * _sigmoid(x)


def _modulate(x, g, shift, scale):
    ms = jnp.mean(x * x, axis=-1, keepdims=True)
    return x * lax.rsqrt(ms + EPS) * (g * (1.0 + scale)) + shift


def _dot(a, b):
    return jnp.dot(a, b, preferred_element_type=F32)


def _dot_nt(a, b):
    return lax.dot_general(a, b, (((1,), (1,)), ((), ())), preferred_element_type=F32)


def _lo_lanes():
    return lax.broadcasted_iota(jnp.int32, (1, LANES), 1) < (LANES // 2)


def _ada_kernel(s_ref, w_ref, b_ref, o_ref):
    s = _silu(s_ref[...]).astype(BF16)
    o_ref[0] = _dot(s, w_ref[0].astype(BF16)) + b_ref[0]


def _ada_mod(s_rows, ada_w, ada_b):
    depth, d, n = ada_w.shape
    tn = n // 4
    return pl.pallas_call(
        _ada_kernel,
        grid=(depth, n // tn),
        in_specs=[pl.BlockSpec((MOD_ROWS, d), lambda l, j: (0, 0)),
                  pl.BlockSpec((1, d, tn), lambda l, j: (l, 0, j)),
                  pl.BlockSpec((1, 1, tn), lambda l, j: (l, 0, j))],
        out_specs=pl.BlockSpec((1, MOD_ROWS, tn), lambda l, j: (l, 0, j)),
        out_shape=jax.ShapeDtypeStruct((depth, MOD_ROWS, n), F32),
        compiler_params=_cparams(("arbitrary", "arbitrary")),
        name="ada_mod",
    )(s_rows, ada_w, ada_b.reshape(depth, 1, n))


def _mod_spec(d, sec, row_fn):
    return pl.BlockSpec((1, 1, d), lambda *idx: (row_fn(*idx), 0, sec))


def _conv_chunk(window, base, cw_ref, cb_ref, lg_ref, lb_ref, rc):
    first = HALO - CONV_WIDTH // 2
    groups = rc // SUBLANES
    accs = [cb_ref[...]] * groups
    for w in range(CONV_WIDTH):
        s, a = (first + w) % SUBLANES, (first + w) // SUBLANES
        wv = cw_ref[w]
        for gi in range(groups):
            accs[gi] = accs[gi] + window(s, base + (a + gi) * SUBLANES, SUBLANES) * wv
    acc = jnp.concatenate(accs, axis=0)
    mu = jnp.mean(acc, axis=-1, keepdims=True)
    xc = acc - mu
    var = jnp.mean(xc * xc, axis=-1, keepdims=True)
    rep = lambda ref: jnp.concatenate([ref[...]] * groups, axis=0)
    return _silu(xc * lax.rsqrt(var + EPS) * rep(lg_ref) + rep(lb_ref))


def _head_norm(r, gain_ref, out_ref, lo):
    for p in range(D_A // LANES):
        sl = slice(p * LANES, (p + 1) * LANES)
        xp = r[:, sl]
        sq = xp * xp
        s_all = jnp.sum(sq, axis=-1, keepdims=True)
        s_lo = jnp.sum(jnp.where(lo, sq, 0.0), axis=-1, keepdims=True)
        ms = jnp.where(lo, s_lo, s_all - s_lo) * (1.0 / NA_HEAD_DIM)
        out_ref[:, sl] = (xp * lax.rsqrt(ms + EPS) * gain_ref[:, sl]).astype(out_ref.dtype)


def _ab_in_kernel(x_ref, sh_ref, sc_ref, g_ref, w_ref, qg_ref, kg_ref, q_ref, k_ref, v_ref, glu_ref):
    h = _modulate(x_ref[...], g_ref[...], sh_ref[0], sc_ref[0]).astype(BF16)
    lo = _lo_lanes()
    _head_norm(_dot(h, w_ref[:, 0:D_A]), qg_ref, q_ref, lo)
    _head_norm(_dot(h, w_ref[:, D_A:2 * D_A]), kg_ref, k_ref, lo)
    v_ref[...] = _dot(h, w_ref[:, 2 * D_A:3 * D_A]).astype(v_ref.dtype)
    u = _dot(h, w_ref[:, 3 * D_A:3 * D_A + D_B])
    gt = _dot(h, w_ref[:, 3 * D_A + D_B:3 * D_A + 2 * D_B])
    glu_ref[...] = u * _sigmoid(gt)


def _ab_in_conv_kernel(x_ref, xn_ref, sh_ref, sc_ref, g_ref, w_ref, qg_ref, kg_ref, cw_ref, cb_ref, lg_ref, lb_ref,
                       q_ref, k_ref, v_ref, conv_ref, win_ref, shift_ref, *, tm, nts, rc):
    i = pl.program_id(0)

    @pl.when(i == 0)
    def _():
        win_ref[...] = jnp.zeros_like(win_ref)

    span = tm + 2 * HALO - SUBLANES
    for s in range(SUBLANES):
        shift_ref[s, 0:span] = win_ref[s:s + span]
    prev_tail = win_ref[tm:tm + HALO]
    window = lambda s, start, size: shift_ref[s, start:start + size, :]
    for c in range(tm // rc):
        conv_ref[c * rc:(c + 1) * rc, :] = _conv_chunk(window, c * rc, cw_ref, cb_ref, lg_ref, lb_ref, rc
                                                       ).astype(conv_ref.dtype)
    g, sh, sc = g_ref[...], sh_ref[0], sc_ref[0]
    h = _modulate(x_ref[...], g, sh, sc).astype(BF16)
    h_ext = jnp.concatenate([h, _modulate(xn_ref[...], g, sh, sc).astype(BF16)], axis=0)
    lo = _lo_lanes()
    _head_norm(_dot(h, w_ref[:, 0:D_A]), qg_ref, q_ref, lo)
    _head_norm(_dot(h, w_ref[:, D_A:2 * D_A]), kg_ref, k_ref, lo)
    v_ref[...] = _dot(h, w_ref[:, 2 * D_A:3 * D_A]).astype(v_ref.dtype)
    u = _dot(h_ext, w_ref[:, 3 * D_A:3 * D_A + D_B])
    gt = _dot(h_ext, w_ref[:, 3 * D_A + D_B:3 * D_A + 2 * D_B])
    glu = u * _sigmoid(gt)
    seq_pos = lax.rem(i, nts)
    win_ref[0:HALO] = jnp.where(seq_pos == 0, 0.0, prev_tail)
    win_ref[HALO:HALO + tm] = glu[0:tm]
    win_ref[HALO + tm:2 * HALO + tm] = jnp.where(seq_pos == nts - 1, 0.0, glu[tm:tm + HALO])


def _ab_in_conv(x2, mod3, norm_g, w_in, qg, kg, cw, cb, lg, lb, t, tm):
    m, d = x2.shape
    ntiles = m // tm
    nts = t // tm
    hb = tm // HALO
    cur = lambda i: jnp.minimum(i, ntiles - 1)
    tok = lambda c: pl.BlockSpec((tm, c), lambda i: (cur(i), 0))
    row = lambda i: cur(i) // nts
    rep = lambda: _const_spec((SUBLANES, D_B))
    return pl.pallas_call(
        functools.partial(_ab_in_conv_kernel, tm=tm, nts=nts, rc=CONV_ROWS),
        grid=(ntiles + 1,),
        in_specs=[tok(d),
                  pl.BlockSpec((HALO, d), lambda i: (jnp.minimum((cur(i) + 1) * hb, m // HALO - 1), 0)),
                  _mod_spec(d, 0, row), _mod_spec(d, 1, row), _const_spec((1, d)),
                  _const_spec(w_in.shape), _const_spec((1, D_A)), _const_spec((1, D_A)),
                  _const_spec((CONV_WIDTH, SUBLANES, D_B)), rep(), rep(), rep()],
        out_specs=[tok(D_A), tok(D_A), tok(D_A),
                   pl.BlockSpec((tm, D_B), lambda i: (jnp.maximum(i - 1, 0), 0))],
        out_shape=[jax.ShapeDtypeStruct((m, D_A), BF16), jax.ShapeDtypeStruct((m, D_A), BF16),
                   jax.ShapeDtypeStruct((m, D_A), BF16), jax.ShapeDtypeStruct((m, D_B), BF16)],
        scratch_shapes=[pltpu.VMEM((tm + 2 * HALO, D_B), F32), pltpu.VMEM((SUBLANES, tm + 2 * HALO, D_B), F32)],
        compiler_params=_cparams(("arbitrary",)),
        name="ab_in_conv",
    )(x2, x2, mod3, mod3, norm_g, w_in, qg, kg, cw, cb, lg, lb)


def _ab_in(x2, mod3, row_fn, norm_g, w_in, qg, kg, tm):
    m, d = x2.shape
    tok = lambda c: pl.BlockSpec((tm, c), lambda i: (i, 0))
    return pl.pallas_call(
        _ab_in_kernel,
        grid=(m // tm,),
        in_specs=[tok(d), _mod_spec(d, 0, row_fn), _mod_spec(d, 1, row_fn), _const_spec((1, d)),
                  _const_spec(w_in.shape), _const_spec((1, D_A)), _const_spec((1, D_A))],
        out_specs=[tok(D_A), tok(D_A), tok(D_A), tok(D_B)],
        out_shape=[jax.ShapeDtypeStruct((m, D_A), BF16), jax.ShapeDtypeStruct((m, D_A), BF16),
                   jax.ShapeDtypeStruct((m, D_A), BF16), jax.ShapeDtypeStruct((m, D_B), F32)],
        compiler_params=_cparams(("arbitrary",)),
        name="ab_in",
    )(x2, mod3, mod3, norm_g, w_in, qg, kg)


def _softmax_pv(scores, values):
    m = functools.reduce(jnp.maximum, [jnp.max(s, axis=-1, keepdims=True) for s in scores])
    ps = [jnp.exp2(s - m) for s in scores]
    l = functools.reduce(jnp.add, [jnp.sum(p, axis=-1, keepdims=True) for p in ps])
    o = functools.reduce(jnp.add, [_dot(p.astype(BF16), v) for p, v in zip(ps, values)])
    return o / l


def _pair_rows(qp, lo):
    zero = jnp.zeros_like(qp)
    return jnp.concatenate([jnp.where(lo, qp, zero), jnp.where(lo, zero, qp)], axis=0)


def _na_kernel(q_ref, k_ref, v_ref, kc_ref, vc_ref, tb_ref, o_ref, *, rows, rps):
    nk = NA_WIN_H * GRID_W
    lo = _lo_lanes()
    npair = D_A // LANES
    sls = [slice(p * LANES, (p + 1) * LANES) for p in range(npair)]
    starts, rhos = [], []
    for i in range(rps):
        r = pl.program_id(1) * rps + i
        r0 = jnp.clip(r - NA_WIN_H // 2, 0, rows - NA_WIN_H)
        starts.append(pl.multiple_of(r0 * GRID_W, GRID_W))
        rhos.append(r0 - r + (NA_WIN_H - 1))
    scores = []
    for i in range(rps):
        for p in range(npair):
            q2 = _pair_rows(q_ref[i * GRID_W:(i + 1) * GRID_W, sls[p]], lo)
            bias = jnp.concatenate([tb_ref[rhos[i] + 2 * a, p] for a in range(NA_WIN_H // 2)], axis=1)
            scores.append((_dot_nt(q2, k_ref[pl.ds(starts[i], nk), sls[p]]) + bias,
                           _dot_nt(q2, kc_ref[:, sls[p]])))
    probs = []
    for s_loc, s_ctx in scores:
        m = jnp.maximum(jnp.max(s_loc, axis=-1, keepdims=True), jnp.max(s_ctx, axis=-1, keepdims=True))
        p_loc = jnp.exp2(s_loc - m)
        p_ctx = jnp.exp2(s_ctx - m)
        l = jnp.sum(p_loc, axis=-1, keepdims=True) + jnp.sum(p_ctx, axis=-1, keepdims=True)
        probs.append((p_loc.astype(BF16), p_ctx.astype(BF16), l))
    for i in range(rps):
        for p in range(npair):
            p_loc, p_ctx, l = probs[i * npair + p]
            o2 = (_dot(p_loc, v_ref[pl.ds(starts[i], nk), sls[p]]) + _dot(p_ctx, vc_ref[:, sls[p]])) / l
            o_ref[i * GRID_W:(i + 1) * GRID_W, sls[p]] = jnp.where(
                lo, o2[0:GRID_W], o2[GRID_W:2 * GRID_W]).astype(o_ref.dtype)


def _na_attention(q, k, v, kc, vc, table, batch, t, n, rps):
    rows = t // GRID_W
    assert rows >= NA_WIN_H and t % GRID_W == 0 and rows % rps == 0
    steps = rows // rps
    return pl.pallas_call(
        functools.partial(_na_kernel, rows=rows, rps=rps),
        grid=(batch, steps),
        in_specs=[pl.BlockSpec((rps * GRID_W, D_A), lambda b, s: (b * steps + s, 0)),
                  pl.BlockSpec((t, D_A), lambda b, s: (b, 0)),
                  pl.BlockSpec((t, D_A), lambda b, s: (b, 0)),
                  pl.BlockSpec((n, D_A), lambda b, s: (b, 0)),
                  pl.BlockSpec((n, D_A), lambda b, s: (b, 0)), _const_spec(table.shape)],
        out_specs=pl.BlockSpec((rps * GRID_W, D_A), lambda b, s: (b * steps + s, 0)),
        out_shape=jax.ShapeDtypeStruct((batch * t, D_A), BF16),
        compiler_params=_cparams(("arbitrary", "arbitrary")),
        name="na_attention",
    )(q, k, v, kc, vc, table)


def _ctx_attn_kernel(q_ref, k_ref, v_ref, o_ref):
    lo = _lo_lanes()
    for p in range(D_A // LANES):
        sl = slice(p * LANES, (p + 1) * LANES)
        qp = q_ref[:, sl]
        kp = k_ref[:, sl]
        vp = v_ref[:, sl]
        outs = []
        for e in range(2):
            qm = jnp.where(lo if e == 0 else jnp.logical_not(lo), qp, jnp.zeros_like(qp))
            outs.append(_softmax_pv([_dot_nt(qm, kp)], [vp]))
        o_ref[:, sl] = jnp.where(lo, outs[0], outs[1]).astype(o_ref.dtype)


def _ctx_attention(q, k, v, batch, n):
    spec = pl.BlockSpec((n, D_A), lambda b: (b, 0))
    return pl.pallas_call(
        _ctx_attn_kernel,
        grid=(batch,),
        in_specs=[spec, spec, spec],
        out_specs=spec,
        out_shape=jax.ShapeDtypeStruct((batch * n, D_A), BF16),
        compiler_params=_cparams(("arbitrary",)),
        name="ctx_attention",
    )(q, k, v)


def _na_bias_table(rpb):
    n_rel = 2 * NA_WIN_W - 1
    lead = GRID_W - NA_WIN_W
    ext = jnp.pad(rpb.astype(F32), ((0, 0), (0, 0), (lead, 2 * GRID_W - 1 - lead - n_rel)), constant_values=NEG_INF)
    band = jnp.stack([ext[:, :, GRID_W - 1 - c:2 * GRID_W - 1 - c] for c in range(GRID_W)], axis=2)
    col = np.arange(GRID_W)
    c0 = np.clip(col - NA_WIN_W // 2, 0, GRID_W - NA_WIN_W)
    inside = (col[None, :] >= c0[:, None]) & (col[None, :] < c0[:, None] + NA_WIN_W)
    band = jnp.where(inside[None, None], band, NEG_INF)
    two = jnp.concatenate([band[:, :-1], band[:, 1:]], axis=-1)
    two = two.reshape(NA_HEADS // 2, 2, 2 * NA_WIN_H - 2, GRID_W, 2 * GRID_W).transpose(2, 0, 1, 3, 4)
    return two.reshape(2 * NA_WIN_H - 2, NA_HEADS // 2, 2 * GRID_W, 2 * GRID_W)


def _conv_kernel(prev_ref, x_ref, next_ref, cw_ref, cb_ref, lg_ref, lb_ref, o_ref, xs_ref, *, tt, nt, rc):
    i = pl.program_id(1)
    xs_ref[0, 0:HALO] = jnp.where(i > 0, prev_ref[...], 0.0)
    xs_ref[0, HALO:HALO + tt] = x_ref[...]
    xs_ref[0, HALO + tt:2 * HALO + tt] = jnp.where(i < nt - 1, next_ref[...], 0.0)
    span = tt + 2 * HALO - SUBLANES
    for s in range(1, SUBLANES):
        xs_ref[s, 0:span] = xs_ref[0, s:s + span]
    window = lambda s, start, size: xs_ref[s, pl.ds(start, size), :]

    def chunk(c, carry):
        base = pl.multiple_of(c * rc, rc)
        o_ref[pl.ds(base, rc), :] = _conv_chunk(window, base, cw_ref, cb_ref, lg_ref, lb_ref, rc).astype(o_ref.dtype)
        return carry

    lax.fori_loop(0, tt // rc, chunk, 0, unroll=4)


def _conv_module(glu, cw, cb, lg, lb, batch, t):
    tt = min(t, TILE_ROWS)
    nt = t // tt
    hb = tt // HALO
    nhb = t // HALO
    return pl.pallas_call(
        functools.partial(_conv_kernel, tt=tt, nt=nt, rc=CONV_ROWS),
        grid=(batch, nt),
        in_specs=[pl.BlockSpec((HALO, D_B), lambda b, i: (b * nhb + jnp.maximum(i * hb - 1, 0), 0)),
                  pl.BlockSpec((tt, D_B), lambda b, i: (b * nt + i, 0)),
                  pl.BlockSpec((HALO, D_B), lambda b, i: (b * nhb + jnp.minimum((i + 1) * hb, nhb - 1), 0)),
                  _const_spec((CONV_WIDTH, SUBLANES, D_B)), _const_spec((SUBLANES, D_B)),
                  _const_spec((SUBLANES, D_B)), _const_spec((SUBLANES, D_B))],
        out_specs=pl.BlockSpec((tt, D_B), lambda b, i: (b * nt + i, 0)),
        out_shape=jax.ShapeDtypeStruct((batch * t, D_B), BF16),
        scratch_shapes=[pltpu.VMEM((SUBLANES, tt + 2 * HALO, D_B), F32)],
        compiler_params=_cparams(("arbitrary", "arbitrary")),
        name="conv_module",
    )(glu, glu, glu, cw, cb, lg, lb)


def _mix_ffn_kernel(x_ref, a1_ref, a2_ref, g1_ref, sh_ref, sc_ref, g2_ref, ng_ref,
                    wo_ref, wg_ref, wu_ref, wd_ref, o_ref, acc_ref, *, tf, sub):
    ka = a1_ref.shape[1]
    d_ff = wg_ref.shape[1]
    tiles = [slice(s * sub, (s + 1) * sub) for s in range(x_ref.shape[0] // sub)]
    hs = []
    for rs in tiles:
        mix = _dot(a1_ref[rs, :], wo_ref[0:ka, :]) + _dot(a2_ref[rs, :], wo_ref[ka:2 * ka, :])
        x1 = x_ref[rs, :] + g1_ref[0] * mix
        o_ref[rs, :] = x1
        hs.append(_modulate(x1, ng_ref[...], sh_ref[0], sc_ref[0]).astype(BF16))
    for rs, h in zip(tiles, hs):
        for c in range(d_ff // tf):
            sl = slice(c * tf, (c + 1) * tf)
            act = (_silu(_dot(h, wg_ref[:, sl])) * _dot(h, wu_ref[:, sl])).astype(BF16)
            part = _dot(act, wd_ref[sl, :])
            if c == 0:
                acc_ref[rs, :] = part
            else:
                acc_ref[rs, :] += part
        o_ref[rs, :] = o_ref[rs, :] + g2_ref[0] * acc_ref[rs, :]


def _mix_ffn(x2, a, a_cols, mod3, row_fn, norm_g, w_out, ffn_w, tm):
    m, d = x2.shape
    a1, a2 = a
    layer, w_gate, w_up, w_down = ffn_w
    ka = w_out.shape[0] // 2
    tok = lambda c, j=0: pl.BlockSpec((tm, c), lambda i: (i, j))
    layer_spec = lambda w: pl.BlockSpec((None,) + w.shape[1:], lambda i: (layer, 0, 0), pipeline_mode=pl.Buffered(1))
    return pl.pallas_call(
        functools.partial(_mix_ffn_kernel, tf=FFN_COLS, sub=min(tm, TILE_ROWS)),
        grid=(m // tm,),
        in_specs=[tok(d), tok(ka, a_cols[0]), tok(ka, a_cols[1]),
                  _mod_spec(d, 2, row_fn), _mod_spec(d, 3, row_fn), _mod_spec(d, 4, row_fn),
                  _mod_spec(d, 5, row_fn), _const_spec((1, d)),
                  _const_spec(w_out.shape), layer_spec(w_gate), layer_spec(w_up), layer_spec(w_down)],
        out_specs=tok(d),
        out_shape=jax.ShapeDtypeStruct((m, d), F32),
        scratch_shapes=[pltpu.VMEM((tm, d), F32)],
        compiler_params=_cparams(("arbitrary",)),
        name="mix_ffn",
    )(x2, a1, a2, mod3, mod3, mod3, mod3, norm_g, w_out, w_gate, w_up, w_down)


def _ml_in_kernel(xp_ref, x_ref, xn_ref, sh_ref, sc_ref, g_ref, w_ref, wvt_ref, cw_ref, cb_ref, gb_ref, cos_ref,
                  sin_ref, q_ref, k_ref, vt_ref, sg_ref, gatest_ref, hs_ref, r_ref, *, tm, nt):
    i = pl.program_id(1)
    g, sh, sc = g_ref[...], sh_ref[0], sc_ref[0]
    hs_ref[0:HALO] = jnp.where(i > 0, _modulate(xp_ref[...], g, sh, sc), 0.0).astype(BF16)
    hs_ref[HALO:HALO + tm] = _modulate(x_ref[...], g, sh, sc).astype(BF16)
    hs_ref[HALO + tm:2 * HALO + tm] = jnp.where(i < nt - 1, _modulate(xn_ref[...], g, sh, sc), 0.0).astype(BF16)
    nqk = 2 * D_CQK
    lane = lax.broadcasted_iota(jnp.int32, (1, LANES), 1)
    r_ref[...] = _dot(hs_ref[...], w_ref[:, 0:nqk])
    hm = hs_ref[HALO:HALO + tm]
    vt = _dot_nt(wvt_ref[...], hm).astype(vt_ref.dtype)
    ones_rows = (lax.broadcasted_iota(jnp.int32, (VT_ROWS - ML_V_DIM, tm), 0) == 0).astype(vt_ref.dtype)
    for h in range(ML_HEADS):
        vt_ref[h * VT_ROWS:h * VT_ROWS + ML_V_DIM, :] = vt[h * ML_V_DIM:(h + 1) * ML_V_DIM]
        vt_ref[h * VT_ROWS + ML_V_DIM:(h + 1) * VT_ROWS, :] = ones_rows
    o_pre = _dot(hm, w_ref[:, nqk:nqk + D_CV])
    gt = _dot(hm, w_ref[:, nqk + D_CV:nqk + D_CV + LANES]) + gb_ref[...]
    sg_ref[...] = _sigmoid(o_pre).astype(sg_ref.dtype)
    log_sig = jnp.minimum(gt, 0.0) - jnp.log(1.0 + jnp.exp(-jnp.abs(gt)))
    gates = jnp.where(lane < 2 * ML_HEADS, gt, log_sig)
    gatest_ref[...] = gates.T
    pad = ML_SHORT_CONV // 2
    rows = tm + 2 * HALO
    qk_parts = []
    for j in range(nqk // LANES):
        rj = r_ref[:, j * LANES:(j + 1) * LANES]
        acc = jnp.broadcast_to(cb_ref[:, j * LANES:(j + 1) * LANES], (tm, LANES))
        for w in range(ML_SHORT_CONV):
            shifted = rj if w == pad else pltpu.roll(rj, (pad - w) % rows, 0)
            acc = acc + shifted[HALO:HALO + tm] * cw_ref[w:w + 1, j * LANES:(j + 1) * LANES]
        qk_parts.append(_silu(acc))
    cos = cos_ref[...]
    sin = sin_ref[...]
    first = (lane & 16) == 0
    for j in range(nqk // LANES):
        xg = qk_parts[j]
        sw = jnp.where(first, pltpu.roll(xg, LANES - 16, 1), pltpu.roll(xg, 16, 1))
        y = xg * cos + sw * sin
        if j < D_CQK // LANES:
            q_ref[:, j * LANES:(j + 1) * LANES] = (y * ML_QK_DIM ** -0.5).astype(q_ref.dtype)
        else:
            jj = j - D_CQK // LANES
            k_ref[:, jj * LANES:(jj + 1) * LANES] = y.astype(k_ref.dtype)


def _ml_in(x2, mod3, row_fn, norm_g, w_in, w_vt, cw, cb, gb, cos_t, sin_t, batch, t, tm):
    m, d = x2.shape
    nt = t // tm
    hb = tm // HALO
    nhb = t // HALO
    nqk = 2 * D_CQK
    tok = lambda c: pl.BlockSpec((tm, c), lambda b, i: (b * nt + i, 0))
    rf = lambda b, i: row_fn(b)
    return pl.pallas_call(
        functools.partial(_ml_in_kernel, tm=tm, nt=nt),
        grid=(batch, nt),
        in_specs=[pl.BlockSpec((HALO, d), lambda b, i: (b * nhb + jnp.maximum(i * hb - 1, 0), 0)),
                  tok(d),
                  pl.BlockSpec((HALO, d), lambda b, i: (b * nhb + jnp.minimum((i + 1) * hb, nhb - 1), 0)),
                  _mod_spec(d, 0, rf), _mod_spec(d, 1, rf), _const_spec((1, d)),
                  _const_spec(w_in.shape), _const_spec(w_vt.shape), _const_spec((ML_SHORT_CONV, nqk)),
                  _const_spec((1, nqk)), _const_spec((1, LANES)),
                  pl.BlockSpec((tm, LANES), lambda b, i: (i, 0)),
                  pl.BlockSpec((tm, LANES), lambda b, i: (i, 0))],
        out_specs=[tok(D_CQK), tok(D_CQK), pl.BlockSpec((ML_HEADS * VT_ROWS, tm), lambda b, i: (b, i)),
                   tok(D_CV), pl.BlockSpec((LANES, tm), lambda b, i: (b, i))],
        out_shape=[jax.ShapeDtypeStruct((m, D_CQK), BF16), jax.ShapeDtypeStruct((m, D_CQK), BF16),
                   jax.ShapeDtypeStruct((batch * ML_HEADS * VT_ROWS, t), BF16),
                   jax.ShapeDtypeStruct((m, D_CV), BF16),
                   jax.ShapeDtypeStruct((batch * LANES, t), F32)],
        scratch_shapes=[pltpu.VMEM((tm + 2 * HALO, d), BF16), pltpu.VMEM((tm + 2 * HALO, nqk), F32)],
        compiler_params=_cparams(("arbitrary", "arbitrary")),
        name="ml_in",
    )(x2, x2, x2, mod3, mod3, norm_g, w_in, w_vt, cw, cb, gb, cos_t, sin_t)


def _rope_tables(t, use_rope):
    if not use_rope:
        return jnp.ones((t, LANES), F32), jnp.zeros((t, LANES), F32)
    in_head = np.arange(LANES) % ML_QK_DIM
    nf = ML_QK_DIM // 4
    inv = ROPE_BASE ** (-np.arange(nf, dtype=np.float64) / nf)
    freq = inv[in_head % nf]
    sign = np.where((in_head // nf) % 2 == 0, -1.0, 1.0)
    tok = np.arange(t)
    pos = np.where((in_head // (2 * nf))[None, :] == 0, (tok // GRID_W)[:, None], (tok % GRID_W)[:, None])
    ang = pos.astype(np.float64) * freq[None, :]
    return jnp.asarray(np.cos(ang), F32), jnp.asarray(np.sin(ang) * sign[None, :], F32)


N_GATES = 4 * ML_HEADS


def _log2_gates(gt_ref, cs):
    return gt_ref[0:N_GATES, cs] * LOG2E


def _cum_rows(x, backward):
    L = x.shape[1]
    hi = x.astype(BF16)
    r1 = x - hi.astype(F32)
    mid = r1.astype(BF16)
    lo = (r1 - mid.astype(F32)).astype(BF16)
    ri = lax.broadcasted_iota(jnp.int32, (L, L), 0)
    ci = lax.broadcasted_iota(jnp.int32, (L, L), 1)
    mat = ((ri >= ci) if backward else (ri <= ci)).astype(F32).astype(BF16)
    out = _dot(jnp.concatenate([hi, mid, lo], axis=0), mat)
    return out[0:N_GATES] + out[N_GATES:2 * N_GATES] + out[2 * N_GATES:3 * N_GATES]


def _gate_rows(gt, cumt, h, backward):
    L = gt.shape[1]
    ig = (ML_HEADS if backward else 0) + h
    lf = (3 * ML_HEADS if backward else 2 * ML_HEADS) + h
    end = 0 if backward else L - 1
    return gt[ig:ig + 1, :], cumt[lf:lf + 1, :], cumt[lf:lf + 1, end:end + 1]


def _stage_weighted_values(lhs_ref, row0, vt_ref, cs, gt, cumt, backward):
    L = gt.shape[1]
    H = ML_HEADS
    ig0, lf0, end = (H, 3 * H, 0) if backward else (0, 2 * H, L - 1)
    b_rows = cumt[lf0:lf0 + H]
    b_end = b_rows[:, end:end + 1]
    w_end = b_end - b_rows + gt[ig0:ig0 + H]
    m_loc = jnp.max(w_end, axis=1, keepdims=True)
    wk = jnp.exp2(w_end - m_loc)
    for h in range(H):
        lhs_ref[row0 + h * VT_ROWS:row0 + (h + 1) * VT_ROWS, :] = (
            vt_ref[h * VT_ROWS:(h + 1) * VT_ROWS, cs].astype(F32) * wk[h:h + 1]).astype(lhs_ref.dtype)
    return m_loc, b_end


def _pair_update(lhs_ref, row0, k_pair, p, m_loc, b_end, lo):
    r0 = row0 + 2 * p * VT_ROWS
    out = _dot(lhs_ref[r0:r0 + 2 * VT_ROWS, :], k_pair)
    hs = (2 * p, 2 * p + 1)
    return (jnp.where(lo, out[0:VT_ROWS], out[VT_ROWS:2 * VT_ROWS]),
            [m_loc[h:h + 1] for h in hs], [b_end[h:h + 1] for h in hs])


def _apply_update(s_pair, upd, m_locs, b_ends, ms, lo):
    decays, gains, m_news = [], [], []
    for e in range(2):
        m_new = jnp.maximum(b_ends[e] + ms[e], m_locs[e])
        decays.append(jnp.exp2(b_ends[e] + ms[e] - m_new))
        gains.append(jnp.exp2(m_locs[e] - m_new))
        m_news.append(m_new)
    return jnp.where(lo, decays[0], decays[1]) * s_pair + jnp.where(lo, gains[0], gains[1]) * upd, m_news


def _absorb_chunks(k_ref, vt_ref, gt_ref, s_ref, m_ref, spre_ref, mpre_ref, lhs_ref, backward, cps):
    L = ML_CHUNK
    H = ML_HEADS
    lo = _lo_lanes()
    order = list(range(cps - 1, -1, -1)) if backward else list(range(cps))
    npair = H // 2
    scal = {}
    for u in order:
        cs = slice(u * L, (u + 1) * L)
        gt = _log2_gates(gt_ref, cs)
        scal[u] = _stage_weighted_values(lhs_ref, u * H * VT_ROWS, vt_ref, cs, gt, _cum_rows(gt, backward), backward)
    updates = {}
    for u in order:
        cs = slice(u * L, (u + 1) * L)
        for p in range(npair):
            updates[u, p] = _pair_update(lhs_ref, u * H * VT_ROWS, k_ref[cs, p * LANES:(p + 1) * LANES], p,
                                         *scal[u], lo)
    for u in order:
        spre_ref[u] = s_ref[...]
        mpre_ref[u] = m_ref[...]
        for p in range(npair):
            hs = (2 * p, 2 * p + 1)
            ms = [m_ref[h:h + 1, 0:1] for h in hs]
            s_new, m_new = _apply_update(s_ref[p], *updates[u, p], ms, lo)
            s_ref[p] = s_new
            for e in range(2):
                m_ref[hs[e]:hs[e] + 1, :] = jnp.broadcast_to(m_new[e], (1, LANES))


def _ml_state_kernel(k_ref, vt_ref, gt_ref, s0_ref, m0_ref, spre_ref, mpre_ref, sfin_ref, mfin_ref, lhs_ref, *,
                     backward, cps):
    @pl.when(pl.program_id(1) == 0)
    def _():
        sfin_ref[...] = s0_ref[...]
        mfin_ref[...] = m0_ref[...]

    _absorb_chunks(k_ref, vt_ref, gt_ref, sfin_ref.at[0], mfin_ref.at[0], spre_ref.at[0], mpre_ref.at[0],
                   lhs_ref, backward, cps)


def _ml_state_scan(k, vt, gt, s0, m0, batch, t, backward, cps):
    L = ML_CHUNK
    nc = t // L
    cps = min(cps, nc)
    assert t % L == 0 and nc % cps == 0
    steps = nc // cps
    np_ = ML_HEADS // 2
    si = (lambda s: steps - 1 - s) if backward else (lambda s: s)
    st = pl.BlockSpec((1, np_, VT_ROWS, LANES), lambda b, s: (b, 0, 0, 0))
    mx = pl.BlockSpec((1, ML_HEADS, LANES), lambda b, s: (b, 0, 0))
    return pl.pallas_call(
        functools.partial(_ml_state_kernel, backward=backward, cps=cps),
        grid=(batch, steps),
        in_specs=[pl.BlockSpec((cps * L, D_CQK), lambda b, s: (b * steps + si(s), 0)),
                  pl.BlockSpec((ML_HEADS * VT_ROWS, cps * L), lambda b, s: (b, si(s))),
                  pl.BlockSpec((LANES, cps * L), lambda b, s: (b, si(s))), st, mx],
        out_specs=[pl.BlockSpec((1, cps, np_, VT_ROWS, LANES), lambda b, s: (b, si(s), 0, 0, 0)),
                   pl.BlockSpec((1, cps, ML_HEADS, LANES), lambda b, s: (b, si(s), 0, 0)), st, mx],
        out_shape=[jax.ShapeDtypeStruct((batch, nc, np_, VT_ROWS, LANES), F32),
                   jax.ShapeDtypeStruct((batch, nc, ML_HEADS, LANES), F32),
                   jax.ShapeDtypeStruct((batch, np_, VT_ROWS, LANES), F32),
                   jax.ShapeDtypeStruct((batch, ML_HEADS, LANES), F32)],
        scratch_shapes=[pltpu.VMEM((cps * ML_HEADS * VT_ROWS, L), BF16)],
        compiler_params=_cparams(("arbitrary", "arbitrary")),
        name="ml_state_bwd" if backward else "ml_state_fwd",
    )(k, vt, gt, s0, m0)


def _ml_out_kernel(q_ref, k_ref, vt_ref, gt_ref, sg_ref, sb_ref, mb_ref, s0_ref, m0_ref, ngb_ref, o_ref,
                   s_ref, m_ref, lhs_ref, *, cps):
    @pl.when(pl.program_id(1) == 0)
    def _():
        s_ref[...] = s0_ref[0]
        m_ref[...] = m0_ref[0]

    L = ML_CHUNK
    lo = _lo_lanes()
    ri = lax.broadcasted_iota(jnp.int32, (L, L), 0)
    ci = lax.broadcasted_iota(jnp.int32, (L, L), 1)
    masks = (ri <= ci, ri >= ci)
    npair = ML_HEADS // 2
    for u in range(cps):
        cs = slice(u * L, (u + 1) * L)
        gt = _log2_gates(gt_ref, cs)
        cum_f = _cum_rows(gt, False)
        cumt = (cum_f, cum_f[:, L - 1:L] - cum_f + gt)
        r_rows = jnp.concatenate([gt[0:ML_HEADS] - cumt[0][2 * ML_HEADS:3 * ML_HEADS],
                                  gt[ML_HEADS:2 * ML_HEADS] - cumt[1][3 * ML_HEADS:4 * ML_HEADS],
                                  jnp.zeros((LANES - 2 * ML_HEADS, L), F32)], axis=0)
        r_cols = r_rows.T
        lhs_row0 = u * ML_HEADS * VT_ROWS
        upd_scal = _stage_weighted_values(lhs_ref, lhs_row0, vt_ref, cs, gt, cum_f, False)
        prods = []
        for p in range(npair):
            sl = slice(p * LANES, (p + 1) * LANES)
            stack = jnp.concatenate([k_ref[cs, sl], s_ref[p].astype(BF16), sb_ref[0, u, p].astype(BF16)], axis=0)
            prods.append(_dot_nt(stack, _pair_rows(q_ref[cs, sl], lo)))
        pts, inters, mqs = [], [], []
        for h in range(ML_HEADS):
            qs = slice((h % 2) * L, (h % 2 + 1) * L)
            for d in range(2):
                ig_row, b_row, _ = _gate_rows(gt, cumt[d], h, d == 1)
                r_col = r_cols[:, d * ML_HEADS + h:d * ML_HEADS + h + 1]
                m = mb_ref[0, u, h:h + 1, 0:1] if d else m_ref[h:h + 1, 0:1]
                g_row = b_row + m
                dt = jnp.where(masks[d], b_row + r_col, NEG_INF)
                m_q = jnp.maximum(g_row, jnp.max(dt, axis=0, keepdims=True))
                pts.append((jnp.exp2(dt - m_q) * prods[h // 2][0:L, qs]).astype(BF16))
                inters.append(jnp.exp2(g_row - m_q))
                mqs.append(m_q)
        nums = [_dot(vt_ref[h * VT_ROWS:(h + 1) * VT_ROWS, cs], jnp.concatenate(pts[2 * h:2 * h + 2], axis=1))
                for h in range(ML_HEADS)]
        for h in range(ML_HEADS):
            qs = slice((h % 2) * L, (h % 2 + 1) * L)
            hsum = None
            for d in range(2):
                i = 2 * h + d
                tot = (inters[i] * prods[h // 2][L + d * VT_ROWS:L + (d + 1) * VT_ROWS, qs]
                       + nums[h][:, d * L:(d + 1) * L])
                den = tot[ML_V_DIM:ML_V_DIM + 1]
                hd = tot[0:ML_V_DIM] / jnp.maximum(jnp.abs(den), jnp.exp2(-mqs[i]))
                hsum = hd if hsum is None else hsum + hd
            hs = slice(h * ML_V_DIM, (h + 1) * ML_V_DIM)
            ms = jnp.mean(hsum * hsum, axis=0, keepdims=True)
            y = hsum * lax.rsqrt(ms + EPS) * ngb_ref[hs, :]
            o_ref[cs, hs] = (y.T.astype(BF16) * sg_ref[cs, hs]).astype(o_ref.dtype)
        for p in range(npair):
            hp = (2 * p, 2 * p + 1)
            ms = [m_ref[h:h + 1, 0:1] for h in hp]
            upd = _pair_update(lhs_ref, lhs_row0, k_ref[cs, p * LANES:(p + 1) * LANES], p, *upd_scal, lo)
            s_new, m_new = _apply_update(s_ref[p], *upd, ms, lo)
            s_ref[p] = s_new
            for e in range(2):
                m_ref[hp[e]:hp[e] + 1, :] = jnp.broadcast_to(m_new[e], (1, LANES))


def _ml_out(q, k, vt, gt, sg, s_bwd, m_bwd, s0, m0, norm_gb, batch, t, cps):
    L = ML_CHUNK
    nc = t // L
    assert t % L == 0 and nc % cps == 0
    steps = nc // cps
    np_ = ML_HEADS // 2
    tok = lambda cols: pl.BlockSpec((cps * L, cols), lambda b, s: (b * steps + s, 0))
    return pl.pallas_call(
        functools.partial(_ml_out_kernel, cps=cps),
        grid=(batch, steps),
        in_specs=[tok(D_CQK), tok(D_CQK),
                  pl.BlockSpec((ML_HEADS * VT_ROWS, cps * L), lambda b, s: (b, s)),
                  pl.BlockSpec((LANES, cps * L), lambda b, s: (b, s)), tok(D_CV),
                  pl.BlockSpec((1, cps, np_, VT_ROWS, LANES), lambda b, s: (b, s, 0, 0, 0)),
                  pl.BlockSpec((1, cps, ML_HEADS, LANES), lambda b, s: (b, s, 0, 0)),
                  pl.BlockSpec((1, np_, VT_ROWS, LANES), lambda b, s: (b, 0, 0, 0)),
                  pl.BlockSpec((1, ML_HEADS, LANES), lambda b, s: (b, 0, 0)),
                  _const_spec((D_CV, LANES))],
        out_specs=tok(D_CV),
        out_shape=jax.ShapeDtypeStruct((batch * t, D_CV), BF16),
        scratch_shapes=[pltpu.VMEM((np_, VT_ROWS, LANES), F32), pltpu.VMEM((ML_HEADS, LANES), F32),
                        pltpu.VMEM((cps * ML_HEADS * VT_ROWS, L), BF16)],
        compiler_params=_cparams(("arbitrary", "arbitrary")),
        name="ml_out",
    )(q, k, vt, gt, sg, s_bwd, m_bwd, s0, m0, norm_gb)


def _even_layer(x2, ctx2, mod3, lat_row, ctx_row, batch, t, n, norm_mix_g, norm_ffn_g, ffn_w, w_in, qg, kg,
                table, cw, cb, lg, lb, w_out, ctx_out):
    tm = min(TILE_ROWS, t)
    tmc = min(TILE_ROWS, batch * n)
    ql, kl, vl, conv_l = _ab_in_conv(x2, mod3, norm_mix_g, w_in, qg, kg, cw, cb, lg, lb, t, min(2 * TILE_ROWS, t))
    qc, kc, vc, glu_c = _ab_in(ctx2, mod3, ctx_row, norm_mix_g, w_in, qg, kg, tmc)
    att_l = _na_attention(ql, kl, vl, kc, vc, table, batch, t, n, rps=NA_ROWS_PER_STEP)
    tmf = min(FFN_BLOCK_ROWS, t)
    x2 = _mix_ffn(x2, (att_l, conv_l), (0, 0), mod3, lat_row(t // tmf), norm_ffn_g, w_out, ffn_w, tmf)
    if ctx_out:
        att_c = _ctx_attention(qc, kc, vc, batch, n)
        conv_c = _conv_module(glu_c, cw, cb, lg, lb, batch, n)
        ctx2 = _mix_ffn(ctx2, (att_c, conv_c), (0, 0), mod3, ctx_row, norm_ffn_g, w_out, ffn_w, tmc)
    return x2, ctx2


def _odd_layer(x2, ctx2, mod3, lat_row, ctx_row, batch, t, n, norm_mix_g, norm_ffn_g, ffn_w, w_in, w_vt, cw, cb, gb,
               norm_gb, w_out):
    tm = min(TILE_ROWS, t)
    tmc = min(TILE_ROWS, n)
    cos_l, sin_l = _rope_tables(t, True)
    cos_c, sin_c = _rope_tables(n, False)
    _, kc, vtc, _, gtc = _ml_in(ctx2, mod3, lambda b: ctx_row(b), norm_mix_g, w_in, w_vt, cw, cb, gb, cos_c,
                                sin_c, batch, n, tmc)
    s_zero = jnp.zeros((batch, ML_HEADS // 2, VT_ROWS, LANES), F32)
    m_zero = jnp.zeros((batch, ML_HEADS, LANES), F32)
    _, _, sf, mf = _ml_state_scan(kc, vtc, gtc, s_zero, m_zero, batch, n, False, cps=SCAN_CHUNKS_PER_STEP)
    _, _, sb, mb = _ml_state_scan(kc, vtc, gtc, s_zero, m_zero, batch, n, True, cps=SCAN_CHUNKS_PER_STEP)
    ql, kl, vtl, sgl, gtl = _ml_in(x2, mod3, lambda b: b, norm_mix_g, w_in, w_vt, cw, cb, gb, cos_l, sin_l,
                                   batch, t, min(ML_IN_ROWS, t))
    s_pre, m_pre, _, _ = _ml_state_scan(kl, vtl, gtl, sb, mb, batch, t, True, cps=SCAN_CHUNKS_PER_STEP)
    gated = _ml_out(ql, kl, vtl, gtl, sgl, s_pre, m_pre, sf, mf, norm_gb, batch, t, cps=OUT_CHUNKS_PER_STEP)
    tmf = min(FFN_BLOCK_ROWS, t)
    return _mix_ffn(x2, (gated, gated), (0, 1), mod3, lat_row(t // tmf), norm_ffn_g, w_out, ffn_w, tmf)


def kernel(x, c, ctx, c_ctx, ada_w, ada_b, norm_mix_g, norm_ffn_g, ffn_w_gate, ffn_w_up, ffn_w_down, ab_w_in,
           na_q_norm_g, na_k_norm_g, na_rpb, conv_w, conv_b, conv_ln_g, conv_ln_b, ab_w_out, ml_w_in, ml_conv_w,
           ml_conv_b, ml_gate_b, ml_norm_g, ml_w_out):
    batch, t, d = x.shape
    n = ctx.shape[1]
    depth = ada_w.shape[0]
    assert batch + 1 <= MOD_ROWS and depth % 2 == 0, "odd layers are only implemented as the last-layer form"
    s_rows = jnp.zeros((MOD_ROWS, d), F32).at[:batch].set(c).at[batch].set(c_ctx)
    mod = _ada_mod(s_rows, ada_w, ada_b)
    x2 = x.reshape(batch * t, d)
    ctx2 = ctx.reshape(batch * n, d)
    lat_row = lambda tiles: (lambda i: i // tiles)
    ctx_row = lambda *_: batch
    ffn_bf16 = (ffn_w_gate.astype(BF16), ffn_w_up.astype(BF16), ffn_w_down.astype(BF16))
    rep8 = lambda v: jnp.broadcast_to(v[None, :], (SUBLANES, v.shape[0]))
    for l in range(depth):
        j = l // 2
        last = l == depth - 1
        mod3 = mod[l].reshape(MOD_ROWS, 1, 6 * d)
        ffn_w = (l,) + ffn_bf16
        nmg = norm_mix_g[l].reshape(1, d)
        nfg = norm_ffn_g[l].reshape(1, d)
        if l % 2 == 0:
            qg = (jnp.tile(na_q_norm_g[j], NA_HEADS) * (NA_HEAD_DIM ** -0.5 * LOG2E)).reshape(1, D_A)
            kg = jnp.tile(na_k_norm_g[j], NA_HEADS).reshape(1, D_A)
            x2, ctx2 = _even_layer(x2, ctx2, mod3, lat_row, ctx_row, batch, t, n, nmg, nfg, ffn_w,
                                   ab_w_in[j].astype(BF16), qg, kg, _na_bias_table(na_rpb[j] * LOG2E),
                                   jnp.broadcast_to(conv_w[j][:, None, :], (CONV_WIDTH, SUBLANES, D_B)),
                                   rep8(conv_b[j]), rep8(conv_ln_g[j]), rep8(conv_ln_b[j]),
                                   ab_w_out[j].astype(BF16), not last)
        else:
            assert last
            nqk = 2 * D_CQK
            w = ml_w_in[j]
            w_gates = jnp.pad(w[:, nqk + 2 * D_CV:], ((0, 0), (0, LANES - 4 * ML_HEADS)))
            w_in = jnp.concatenate([w[:, :nqk], w[:, nqk + D_CV:nqk + 2 * D_CV], w_gates], axis=1).astype(BF16)
            w_vt = w[:, nqk:nqk + D_CV].T.astype(BF16)
            gb = jnp.pad(ml_gate_b[j], (0, LANES - 4 * ML_HEADS)).reshape(1, LANES)
            norm_gb = jnp.broadcast_to(ml_norm_g[j][:, None], (D_CV, LANES))
            x2 = _odd_layer(x2, ctx2, mod3, lat_row, ctx_row, batch, t, n, nmg, nfg, ffn_w, w_in, w_vt,
                            ml_conv_w[j], ml_conv_b[j].reshape(1, nqk), gb, norm_gb, ml_w_out[j].astype(BF16))
    return x2.reshape(batch, t, d)
```

```python
import functools
import math

import jax
import jax.numpy as jnp
import numpy as np
from jax import lax
from jax.experimental import pallas as pl
from jax.experimental.pallas import tpu as pltpu

F32 = jnp.float32
BF16 = jnp.bfloat16

EPS = 1e-6
GRID_W = 64
NA_HEADS = 8
NA_HEAD_DIM = 64
D_A = NA_HEADS * NA_HEAD_DIM
NA_WIN_H = 8
NA_WIN_W = 16
D_B = 512
CONV_WIDTH = 31
ML_HEADS = 8
ML_QK_DIM = 64
ML_V_DIM = 128
D_CQK = ML_HEADS * ML_QK_DIM
D_CV = ML_HEADS * ML_V_DIM
ML_SHORT_CONV = 5
ML_CHUNK = 128
ROPE_BASE = 10000.0

LANES = 128
SUBLANES = 8
HALO = 16
VT_ROWS = ML_V_DIM + HALO
MOD_ROWS = 16
VMEM_LIMIT = 56 * 1024 * 1024
TILE_ROWS = 512
FFN_BLOCK_ROWS = 1024
FFN_COLS = 256
ML_IN_ROWS = 1024
CONV_ROWS = 32
NA_ROWS_PER_STEP = 8
SCAN_CHUNKS_PER_STEP = 16
OUT_CHUNKS_PER_STEP = 4
NEG_INF = float("-inf")
NA_MASKED = -1e30
LOG2E = math.log2(math.e)


def _cparams(sem):
    return pltpu.CompilerParams(dimension_semantics=sem, vmem_limit_bytes=VMEM_LIMIT)


def _const_spec(shape):
    nd = len(shape)
    return pl.BlockSpec(shape, lambda *_: (0,) * nd, pipeline_mode=pl.Buffered(1))


def _sigmoid(x):
    return 1.0 / (1.0 + jnp.exp2(x * (-LOG2E)))


def _silu(x):
    return x * _sigmoid(x)


def _modulate(x, g, shift, scale):
    ms = jnp.mean(x * x, axis=-1, keepdims=True)
    return x * lax.rsqrt(ms + EPS) * (g * (1.0 + scale)) + shift


def _dot(a, b):
    return jnp.dot(a, b, preferred_element_type=F32)


def _dot_nt(a, b):
    return lax.dot_general(a, b, (((1,), (1,)), ((), ())), preferred_element_type=F32)


def _lo_lanes():
    return lax.broadcasted_iota(jnp.int32, (1, LANES), 1) < (LANES // 2)


def _ada_kernel(s_ref, w_ref, b_ref, o_ref):
    s = _silu(s_ref[...]).astype(BF16)
    o_ref[0] = _dot(s, w_ref[0].astype(BF16)) + b_ref[0]


def _ada_mod(s_rows, ada_w, ada_b):
    depth, d, n = ada_w.shape
    tn = n // 4
    return pl.pallas_call(
        _ada_kernel,
        grid=(depth, n // tn),
        in_specs=[pl.BlockSpec((MOD_ROWS, d), lambda l, j: (0, 0)),
                  pl.BlockSpec((1, d, tn), lambda l, j: (l, 0, j)),
                  pl.BlockSpec((1, 1, tn), lambda l, j: (l, 0, j))],
        out_specs=pl.BlockSpec((1, MOD_ROWS, tn), lambda l, j: (l, 0, j)),
        out_shape=jax.ShapeDtypeStruct((depth, MOD_ROWS, n), F32),
        compiler_params=_cparams(("arbitrary", "arbitrary")),
        name="ada_mod",
    )(s_rows, ada_w, ada_b.reshape(depth, 1, n))


def _mod_spec(d, sec, row_fn):
    return pl.BlockSpec((1, 1, d), lambda *idx: (row_fn(*idx), 0, sec))


def _conv_chunk(window, base, cw_ref, cb_ref, lg_ref, lb_ref, rc):
    first = HALO - CONV_WIDTH // 2
    groups = rc // SUBLANES
    accs = [cb_ref[...]] * groups
    for w in range(CONV_WIDTH):
        s, a = (first + w) % SUBLANES, (first + w) // SUBLANES
        wv = cw_ref[w]
        for gi in range(groups):
            accs[gi] = accs[gi] + window(s, base + (a + gi) * SUBLANES, SUBLANES) * wv
    acc = jnp.concatenate(accs, axis=0)
    mu = jnp.mean(acc, axis=-1, keepdims=True)
    xc = acc - mu
    var = jnp.mean(xc * xc, axis=-1, keepdims=True)
    rep = lambda ref: jnp.concatenate([ref[...]] * groups, axis=0)
    return _silu(xc * lax.rsqrt(var + EPS) * rep(lg_ref) + rep(lb_ref))


def _head_norm(r, gain_ref, out_ref, lo):
    for p in range(D_A // LANES):
        sl = slice(p * LANES, (p + 1) * LANES)
        xp = r[:, sl]
        sq = xp * xp
        s_all = jnp.sum(sq, axis=-1, keepdims=True)
        s_lo = jnp.sum(jnp.where(lo, sq, 0.0), axis=-1, keepdims=True)
        ms = jnp.where(lo, s_lo, s_all - s_lo) * (1.0 / NA_HEAD_DIM)
        out_ref[:, sl] = (xp * lax.rsqrt(ms + EPS) * gain_ref[:, sl]).astype(out_ref.dtype)


def _ab_in_kernel(x_ref, sh_ref, sc_ref, g_ref, w_ref, qg_ref, kg_ref, q_ref, k_ref, v_ref, glu_ref):
    h = _modulate(x_ref[...], g_ref[...], sh_ref[0], sc_ref[0]).astype(BF16)
    lo = _lo_lanes()
    _head_norm(_dot(h, w_ref[:, 0:D_A]), qg_ref, q_ref, lo)
    _head_norm(_dot(h, w_ref[:, D_A:2 * D_A]), kg_ref, k_ref, lo)
    v_ref[...] = _dot(h, w_ref[:, 2 * D_A:3 * D_A]).astype(v_ref.dtype)
    u = _dot(h, w_ref[:, 3 * D_A:3 * D_A + D_B])
    gt = _dot(h, w_ref[:, 3 * D_A + D_B:3 * D_A + 2 * D_B])
    glu_ref[...] = u * _sigmoid(gt)


def _ab_in_conv_kernel(x_ref, xn_ref, sh_ref, sc_ref, g_ref, w_ref, qg_ref, kg_ref, cw_ref, cb_ref, lg_ref, lb_ref,
                       q_ref, k_ref, v_ref, conv_ref, win_ref, shift_ref, *, tm, nts, rc):
    i = pl.program_id(0)

    @pl.when(i == 0)
    def _():
        win_ref[...] = jnp.zeros_like(win_ref)

    span = tm + 2 * HALO - SUBLANES
    for s in range(SUBLANES):
        shift_ref[s, 0:span] = win_ref[s:s + span]
    prev_tail = win_ref[tm:tm + HALO]
    window = lambda s, start, size: shift_ref[s, start:start + size, :]
    for c in range(tm // rc):
        conv_ref[c * rc:(c + 1) * rc, :] = _conv_chunk(window, c * rc, cw_ref, cb_ref, lg_ref, lb_ref, rc
                                                       ).astype(conv_ref.dtype)
    g, sh, sc = g_ref[...], sh_ref[0], sc_ref[0]
    h = _modulate(x_ref[...], g, sh, sc).astype(BF16)
    h_ext = jnp.concatenate([h, _modulate(xn_ref[...], g, sh, sc).astype(BF16)], axis=0)
    lo = _lo_lanes()
    _head_norm(_dot(h, w_ref[:, 0:D_A]), qg_ref, q_ref, lo)
    _head_norm(_dot(h, w_ref[:, D_A:2 * D_A]), kg_ref, k_ref, lo)
    v_ref[...] = _dot(h, w_ref[:, 2 * D_A:3 * D_A]).astype(v_ref.dtype)
    u = _dot(h_ext, w_ref[:, 3 * D_A:3 * D_A + D_B])
    gt = _dot(h_ext, w_ref[:, 3 * D_A + D_B:3 * D_A + 2 * D_B])
    glu = u * _sigmoid(gt)
    seq_pos = lax.rem(i, nts)
    win_ref[0:HALO] = jnp.where(seq_pos == 0, 0.0, prev_tail)
    win_ref[HALO:HALO + tm] = glu[0:tm]
    win_ref[HALO + tm:2 * HALO + tm] = jnp.where(seq_pos == nts - 1, 0.0, glu[tm:tm + HALO])


def _ab_in_conv(x2, mod3, norm_g, w_in, qg, kg, cw, cb, lg, lb, t, tm):
    m, d = x2.shape
    ntiles = m // tm
    nts = t // tm
    hb = tm // HALO
    cur = lambda i: jnp.minimum(i, ntiles - 1)
    tok = lambda c: pl.BlockSpec((tm, c), lambda i: (cur(i), 0))
    row = lambda i: cur(i) // nts
    rep = lambda: _const_spec((SUBLANES, D_B))
    return pl.pallas_call(
        functools.partial(_ab_in_conv_kernel, tm=tm, nts=nts, rc=CONV_ROWS),
        grid=(ntiles + 1,),
        in_specs=[tok(d),
                  pl.BlockSpec((HALO, d), lambda i: (jnp.minimum((cur(i) + 1) * hb, m // HALO - 1), 0)),
                  _mod_spec(d, 0, row), _mod_spec(d, 1, row), _const_spec((1, d)),
                  _const_spec(w_in.shape), _const_spec((1, D_A)), _const_spec((1, D_A)),
                  _const_spec((CONV_WIDTH, SUBLANES, D_B)), rep(), rep(), rep()],
        out_specs=[tok(D_A), tok(D_A), tok(D_A),
                   pl.BlockSpec((tm, D_B), lambda i: (jnp.maximum(i - 1, 0), 0))],
        out_shape=[jax.ShapeDtypeStruct((m, D_A), BF16), jax.ShapeDtypeStruct((m, D_A), BF16),
                   jax.ShapeDtypeStruct((m, D_A), BF16), jax.ShapeDtypeStruct((m, D_B), BF16)],
        scratch_shapes=[pltpu.VMEM((tm + 2 * HALO, D_B), F32), pltpu.VMEM((SUBLANES, tm + 2 * HALO, D_B), F32)],
        compiler_params=_cparams(("arbitrary",)),
        name="ab_in_conv",
    )(x2, x2, mod3, mod3, norm_g, w_in, qg, kg, cw, cb, lg, lb)


def _ab_in(x2, mod3, row_fn, norm_g, w_in, qg, kg, tm):
    m, d = x2.shape
    tok = lambda c: pl.BlockSpec((tm, c), lambda i: (i, 0))
    return pl.pallas_call(
        _ab_in_kernel,
        grid=(m // tm,),
        in_specs=[tok(d), _mod_spec(d, 0, row_fn), _mod_spec(d, 1, row_fn), _const_spec((1, d)),
                  _const_spec(w_in.shape), _const_spec((1, D_A)), _const_spec((1, D_A))],
        out_specs=[tok(D_A), tok(D_A), tok(D_A), tok(D_B)],
        out_shape=[jax.ShapeDtypeStruct((m, D_A), BF16), jax.ShapeDtypeStruct((m, D_A), BF16),
                   jax.ShapeDtypeStruct((m, D_A), BF16), jax.ShapeDtypeStruct((m, D_B), F32)],
        compiler_params=_cparams(("arbitrary",)),
        name="ab_in",
    )(x2, mod3, mod3, norm_g, w_in, qg, kg)


def _softmax_pv(scores, values):
    m = functools.reduce(jnp.maximum, [jnp.max(s, axis=-1, keepdims=True) for s in scores])
    ps = [jnp.exp2(s - m) for s in scores]
    l = functools.reduce(jnp.add, [jnp.sum(p, axis=-1, keepdims=True) for p in ps])
    o = functools.reduce(jnp.add, [_dot(p.astype(BF16), v) for p, v in zip(ps, values)])
    return o / l


def _pair_rows(qp, lo):
    zero = jnp.zeros_like(qp)
    return jnp.concatenate([jnp.where(lo, qp, zero), jnp.where(lo, zero, qp)], axis=0)


def _na_kernel(q_ref, k_ref, v_ref, kc_ref, vc_ref, tb_ref, o_ref, *, rows, rps):
    nk = NA_WIN_H * GRID_W
    lo = _lo_lanes()
    npair = D_A // LANES
    sls = [slice(p * LANES, (p + 1) * LANES) for p in range(npair)]
    starts, rhos = [], []
    for i in range(rps):
        r = pl.program_id(1) * rps + i
        r0 = jnp.clip(r - NA_WIN_H // 2, 0, rows - NA_WIN_H)
        starts.append(pl.multiple_of(r0 * GRID_W, GRID_W))
        rhos.append(r0 - r + (NA_WIN_H - 1))
    eye = (lax.broadcasted_iota(jnp.int32, (LANES, LANES), 0)
           == lax.broadcasted_iota(jnp.int32, (LANES, LANES), 1)).astype(F32).astype(BF16)
    scores = []
    for i in range(rps):
        for p in range(npair):
            q2 = _pair_rows(q_ref[i * GRID_W:(i + 1) * GRID_W, sls[p]], lo)
            bias_t = jnp.concatenate([tb_ref[rhos[i] + 2 * a, p] for a in range(NA_WIN_H // 2)], axis=0)
            keys = jnp.concatenate([k_ref[pl.ds(starts[i], nk), sls[p]], bias_t], axis=1)
            scores.append((_dot_nt(jnp.concatenate([q2, eye], axis=1), keys), _dot_nt(q2, kc_ref[:, sls[p]])))
    probs = []
    for s_loc, s_ctx in scores:
        m = jnp.maximum(jnp.max(s_loc, axis=-1, keepdims=True), jnp.max(s_ctx, axis=-1, keepdims=True))
        p_loc = jnp.exp2(s_loc - m)
        p_ctx = jnp.exp2(s_ctx - m)
        l = jnp.sum(p_loc, axis=-1, keepdims=True) + jnp.sum(p_ctx, axis=-1, keepdims=True)
        probs.append((p_loc.astype(BF16), p_ctx.astype(BF16), l))
    for i in range(rps):
        for p in range(npair):
            p_loc, p_ctx, l = probs[i * npair + p]
            o2 = (_dot(p_loc, v_ref[pl.ds(starts[i], nk), sls[p]]) + _dot(p_ctx, vc_ref[:, sls[p]])) / l
            o_ref[i * GRID_W:(i + 1) * GRID_W, sls[p]] = jnp.where(
                lo, o2[0:GRID_W], o2[GRID_W:2 * GRID_W]).astype(o_ref.dtype)


def _na_attention(q, k, v, kc, vc, table, batch, t, n, rps):
    rows = t // GRID_W
    assert rows >= NA_WIN_H and t % GRID_W == 0 and rows % rps == 0
    steps = rows // rps
    return pl.pallas_call(
        functools.partial(_na_kernel, rows=rows, rps=rps),
        grid=(batch, steps),
        in_specs=[pl.BlockSpec((rps * GRID_W, D_A), lambda b, s: (b * steps + s, 0)),
                  pl.BlockSpec((t, D_A), lambda b, s: (b, 0)),
                  pl.BlockSpec((t, D_A), lambda b, s: (b, 0)),
                  pl.BlockSpec((n, D_A), lambda b, s: (b, 0)),
                  pl.BlockSpec((n, D_A), lambda b, s: (b, 0)), _const_spec(table.shape)],
        out_specs=pl.BlockSpec((rps * GRID_W, D_A), lambda b, s: (b * steps + s, 0)),
        out_shape=jax.ShapeDtypeStruct((batch * t, D_A), BF16),
        compiler_params=_cparams(("arbitrary", "arbitrary")),
        name="na_attention",
    )(q, k, v, kc, vc, table)


def _ctx_attn_kernel(q_ref, k_ref, v_ref, o_ref):
    lo = _lo_lanes()
    for p in range(D_A // LANES):
        sl = slice(p * LANES, (p + 1) * LANES)
        qp = q_ref[:, sl]
        kp = k_ref[:, sl]
        vp = v_ref[:, sl]
        outs = []
        for e in range(2):
            qm = jnp.where(lo if e == 0 else jnp.logical_not(lo), qp, jnp.zeros_like(qp))
            outs.append(_softmax_pv([_dot_nt(qm, kp)], [vp]))
        o_ref[:, sl] = jnp.where(lo, outs[0], outs[1]).astype(o_ref.dtype)


def _ctx_attention(q, k, v, batch, n):
    spec = pl.BlockSpec((n, D_A), lambda b: (b, 0))
    return pl.pallas_call(
        _ctx_attn_kernel,
        grid=(batch,),
        in_specs=[spec, spec, spec],
        out_specs=spec,
        out_shape=jax.ShapeDtypeStruct((batch * n, D_A), BF16),
        compiler_params=_cparams(("arbitrary",)),
        name="ctx_attention",
    )(q, k, v)


def _na_bias_table(rpb):
    n_rel = 2 * NA_WIN_W - 1
    lead = GRID_W - NA_WIN_W
    ext = jnp.pad(rpb.astype(F32), ((0, 0), (0, 0), (lead, 2 * GRID_W - 1 - lead - n_rel)), constant_values=NA_MASKED)
    band = jnp.stack([ext[:, :, GRID_W - 1 - c:2 * GRID_W - 1 - c] for c in range(GRID_W)], axis=2)
    col = np.arange(GRID_W)
    c0 = np.clip(col - NA_WIN_W // 2, 0, GRID_W - NA_WIN_W)
    inside = (col[None, :] >= c0[:, None]) & (col[None, :] < c0[:, None] + NA_WIN_W)
    band = jnp.where(inside[None, None], band, NA_MASKED)
    two = jnp.concatenate([band[:, :-1], band[:, 1:]], axis=-1)
    two = two.reshape(NA_HEADS // 2, 2, 2 * NA_WIN_H - 2, GRID_W, 2 * GRID_W).transpose(2, 0, 1, 3, 4)
    two = two.reshape(2 * NA_WIN_H - 2, NA_HEADS // 2, 2 * GRID_W, 2 * GRID_W)
    return jnp.swapaxes(two, -1, -2).astype(BF16)


def _conv_kernel(prev_ref, x_ref, next_ref, cw_ref, cb_ref, lg_ref, lb_ref, o_ref, xs_ref, *, tt, nt, rc):
    i = pl.program_id(1)
    xs_ref[0, 0:HALO] = jnp.where(i > 0, prev_ref[...], 0.0)
    xs_ref[0, HALO:HALO + tt] = x_ref[...]
    xs_ref[0, HALO + tt:2 * HALO + tt] = jnp.where(i < nt - 1, next_ref[...], 0.0)
    span = tt + 2 * HALO - SUBLANES
    for s in range(1, SUBLANES):
        xs_ref[s, 0:span] = xs_ref[0, s:s + span]
    window = lambda s, start, size: xs_ref[s, pl.ds(start, size), :]

    def chunk(c, carry):
        base = pl.multiple_of(c * rc, rc)
        o_ref[pl.ds(base, rc), :] = _conv_chunk(window, base, cw_ref, cb_ref, lg_ref, lb_ref, rc).astype(o_ref.dtype)
        return carry

    lax.fori_loop(0, tt // rc, chunk, 0, unroll=4)


def _conv_module(glu, cw, cb, lg, lb, batch, t):
    tt = min(t, TILE_ROWS)
    nt = t // tt
    hb = tt // HALO
    nhb = t // HALO
    return pl.pallas_call(
        functools.partial(_conv_kernel, tt=tt, nt=nt, rc=CONV_ROWS),
        grid=(batch, nt),
        in_specs=[pl.BlockSpec((HALO, D_B), lambda b, i: (b * nhb + jnp.maximum(i * hb - 1, 0), 0)),
                  pl.BlockSpec((tt, D_B), lambda b, i: (b * nt + i, 0)),
                  pl.BlockSpec((HALO, D_B), lambda b, i: (b * nhb + jnp.minimum((i + 1) * hb, nhb - 1), 0)),
                  _const_spec((CONV_WIDTH, SUBLANES, D_B)), _const_spec((SUBLANES, D_B)),
                  _const_spec((SUBLANES, D_B)), _const_spec((SUBLANES, D_B))],
        out_specs=pl.BlockSpec((tt, D_B), lambda b, i: (b * nt + i, 0)),
        out_shape=jax.ShapeDtypeStruct((batch * t, D_B), BF16),
        scratch_shapes=[pltpu.VMEM((SUBLANES, tt + 2 * HALO, D_B), F32)],
        compiler_params=_cparams(("arbitrary", "arbitrary")),
        name="conv_module",
    )(glu, glu, glu, cw, cb, lg, lb)


def _mix_ffn_kernel(x_ref, a1_ref, a2_ref, g1_ref, sh_ref, sc_ref, g2_ref, ng_ref,
                    wo_ref, wg_ref, wu_ref, wd_ref, o_ref, acc_ref, *, tf, sub):
    ka = a1_ref.shape[1]
    d_ff = wg_ref.shape[1]
    tiles = [slice(s * sub, (s + 1) * sub) for s in range(x_ref.shape[0] // sub)]
    hs = []
    for rs in tiles:
        mix = _dot(a1_ref[rs, :], wo_ref[0:ka, :]) + _dot(a2_ref[rs, :], wo_ref[ka:2 * ka, :])
        x1 = x_ref[rs, :] + g1_ref[0] * mix
        o_ref[rs, :] = x1
        hs.append(_modulate(x1, ng_ref[...], sh_ref[0], sc_ref[0]).astype(BF16))
    for rs, h in zip(tiles, hs):
        for c in range(d_ff // tf):
            sl = slice(c * tf, (c + 1) * tf)
            act = (_silu(_dot(h, wg_ref[:, sl])) * _dot(h, wu_ref[:, sl])).astype(BF16)
            part = _dot(act, wd_ref[sl, :])
            if c == 0:
                acc_ref[rs, :] = part
            else:
                acc_ref[rs, :] += part
        o_ref[rs, :] = o_ref[rs, :] + g2_ref[0] * acc_ref[rs, :]


def _mix_ffn(x2, a, a_cols, mod3, row_fn, norm_g, w_out, ffn_w, tm):
    m, d = x2.shape
    a1, a2 = a
    layer, w_gate, w_up, w_down = ffn_w
    ka = w_out.shape[0] // 2
    tok = lambda c, j=0: pl.BlockSpec((tm, c), lambda i: (i, j))
    layer_spec = lambda w: pl.BlockSpec((None,) + w.shape[1:], lambda i: (layer, 0, 0), pipeline_mode=pl.Buffered(1))
    return pl.pallas_call(
        functools.partial(_mix_ffn_kernel, tf=FFN_COLS, sub=min(tm, TILE_ROWS)),
        grid=(m // tm,),
        in_specs=[tok(d), tok(ka, a_cols[0]), tok(ka, a_cols[1]),
                  _mod_spec(d, 2, row_fn), _mod_spec(d, 3, row_fn), _mod_spec(d, 4, row_fn),
                  _mod_spec(d, 5, row_fn), _const_spec((1, d)),
                  _const_spec(w_out.shape), layer_spec(w_gate), layer_spec(w_up), layer_spec(w_down)],
        out_specs=tok(d),
        out_shape=jax.ShapeDtypeStruct((m, d), F32),
        scratch_shapes=[pltpu.VMEM((tm, d), F32)],
        compiler_params=_cparams(("arbitrary",)),
        name="mix_ffn",
    )(x2, a1, a2, mod3, mod3, mod3, mod3, norm_g, w_out, w_gate, w_up, w_down)


def _ml_in_kernel(xp_ref, x_ref, xn_ref, sh_ref, sc_ref, g_ref, w_ref, wvt_ref, cw_ref, cb_ref, gb_ref, cos_ref,
                  sin_ref, q_ref, k_ref, vt_ref, sg_ref, gatest_ref, hs_ref, r_ref, *, tm, nt):
    i = pl.program_id(1)
    g, sh, sc = g_ref[...], sh_ref[0], sc_ref[0]
    hs_ref[0:HALO] = jnp.where(i > 0, _modulate(xp_ref[...], g, sh, sc), 0.0).astype(BF16)
    hs_ref[HALO:HALO + tm] = _modulate(x_ref[...], g, sh, sc).astype(BF16)
    hs_ref[HALO + tm:2 * HALO + tm] = jnp.where(i < nt - 1, _modulate(xn_ref[...], g, sh, sc), 0.0).astype(BF16)
    nqk = 2 * D_CQK
    lane = lax.broadcasted_iota(jnp.int32, (1, LANES), 1)
    r_ref[...] = _dot(hs_ref[...], w_ref[:, 0:nqk])
    hm = hs_ref[HALO:HALO + tm]
    vt = _dot_nt(wvt_ref[...], hm).astype(vt_ref.dtype)
    ones_rows = (lax.broadcasted_iota(jnp.int32, (VT_ROWS - ML_V_DIM, tm), 0) == 0).astype(vt_ref.dtype)
    for h in range(ML_HEADS):
        vt_ref[h * VT_ROWS:h * VT_ROWS + ML_V_DIM, :] = vt[h * ML_V_DIM:(h + 1) * ML_V_DIM]
        vt_ref[h * VT_ROWS + ML_V_DIM:(h + 1) * VT_ROWS, :] = ones_rows
    o_pre = _dot(hm, w_ref[:, nqk:nqk + D_CV])
    gt = _dot(hm, w_ref[:, nqk + D_CV:nqk + D_CV + LANES]) + gb_ref[...]
    sg_ref[...] = _sigmoid(o_pre).astype(sg_ref.dtype)
    log_sig = jnp.minimum(gt, 0.0) - jnp.log(1.0 + jnp.exp(-jnp.abs(gt)))
    gates = jnp.where(lane < 2 * ML_HEADS, gt, log_sig)
    gatest_ref[...] = gates.T
    pad = ML_SHORT_CONV // 2
    rows = tm + 2 * HALO
    qk_parts = []
    for j in range(nqk // LANES):
        rj = r_ref[:, j * LANES:(j + 1) * LANES]
        acc = jnp.broadcast_to(cb_ref[:, j * LANES:(j + 1) * LANES], (tm, LANES))
        for w in range(ML_SHORT_CONV):
            shifted = rj if w == pad else pltpu.roll(rj, (pad - w) % rows, 0)
            acc = acc + shifted[HALO:HALO + tm] * cw_ref[w:w + 1, j * LANES:(j + 1) * LANES]
        qk_parts.append(_silu(acc))
    cos = cos_ref[...]
    sin = sin_ref[...]
    first = (lane & 16) == 0
    for j in range(nqk // LANES):
        xg = qk_parts[j]
        sw = jnp.where(first, pltpu.roll(xg, LANES - 16, 1), pltpu.roll(xg, 16, 1))
        y = xg * cos + sw * sin
        if j < D_CQK // LANES:
            q_ref[:, j * LANES:(j + 1) * LANES] = (y * ML_QK_DIM ** -0.5).astype(q_ref.dtype)
        else:
            jj = j - D_CQK // LANES
            k_ref[:, jj * LANES:(jj + 1) * LANES] = y.astype(k_ref.dtype)


def _ml_in(x2, mod3, row_fn, norm_g, w_in, w_vt, cw, cb, gb, cos_t, sin_t, batch, t, tm):
    m, d = x2.shape
    nt = t // tm
    hb = tm // HALO
    nhb = t // HALO
    nqk = 2 * D_CQK
    tok = lambda c: pl.BlockSpec((tm, c), lambda b, i: (b * nt + i, 0))
    rf = lambda b, i: row_fn(b)
    return pl.pallas_call(
        functools.partial(_ml_in_kernel, tm=tm, nt=nt),
        grid=(batch, nt),
        in_specs=[pl.BlockSpec((HALO, d), lambda b, i: (b * nhb + jnp.maximum(i * hb - 1, 0), 0)),
                  tok(d),
                  pl.BlockSpec((HALO, d), lambda b, i: (b * nhb + jnp.minimum((i + 1) * hb, nhb - 1), 0)),
                  _mod_spec(d, 0, rf), _mod_spec(d, 1, rf), _const_spec((1, d)),
                  _const_spec(w_in.shape), _const_spec(w_vt.shape), _const_spec((ML_SHORT_CONV, nqk)),
                  _const_spec((1, nqk)), _const_spec((1, LANES)),
                  pl.BlockSpec((tm, LANES), lambda b, i: (i, 0)),
                  pl.BlockSpec((tm, LANES), lambda b, i: (i, 0))],
        out_specs=[tok(D_CQK), tok(D_CQK), pl.BlockSpec((ML_HEADS * VT_ROWS, tm), lambda b, i: (b, i)),
                   tok(D_CV), pl.BlockSpec((LANES, tm), lambda b, i: (b, i))],
        out_shape=[jax.ShapeDtypeStruct((m, D_CQK), BF16), jax.ShapeDtypeStruct((m, D_CQK), BF16),
                   jax.ShapeDtypeStruct((batch * ML_HEADS * VT_ROWS, t), BF16),
                   jax.ShapeDtypeStruct((m, D_CV), BF16),
                   jax.ShapeDtypeStruct((batch * LANES, t), F32)],
        scratch_shapes=[pltpu.VMEM((tm + 2 * HALO, d), BF16), pltpu.VMEM((tm + 2 * HALO, nqk), F32)],
        compiler_params=_cparams(("arbitrary", "arbitrary")),
        name="ml_in",
    )(x2, x2, x2, mod3, mod3, norm_g, w_in, w_vt, cw, cb, gb, cos_t, sin_t)


def _rope_tables(t, use_rope):
    if not use_rope:
        return jnp.ones((t, LANES), F32), jnp.zeros((t, LANES), F32)
    in_head = np.arange(LANES) % ML_QK_DIM
    nf = ML_QK_DIM // 4
    inv = ROPE_BASE ** (-np.arange(nf, dtype=np.float64) / nf)
    freq = inv[in_head % nf]
    sign = np.where((in_head // nf) % 2 == 0, -1.0, 1.0)
    tok = np.arange(t)
    pos = np.where((in_head // (2 * nf))[None, :] == 0, (tok // GRID_W)[:, None], (tok % GRID_W)[:, None])
    ang = pos.astype(np.float64) * freq[None, :]
    return jnp.asarray(np.cos(ang), F32), jnp.asarray(np.sin(ang) * sign[None, :], F32)


N_GATES = 4 * ML_HEADS


def _log2_gates(gt_ref, cs):
    return gt_ref[0:N_GATES, cs] * LOG2E


def _cum_rows(x, backward):
    L = x.shape[1]
    hi = x.astype(BF16)
    r1 = x - hi.astype(F32)
    mid = r1.astype(BF16)
    lo = (r1 - mid.astype(F32)).astype(BF16)
    ri = lax.broadcasted_iota(jnp.int32, (L, L), 0)
    ci = lax.broadcasted_iota(jnp.int32, (L, L), 1)
    mat = ((ri >= ci) if backward else (ri <= ci)).astype(F32).astype(BF16)
    out = _dot(jnp.concatenate([hi, mid, lo], axis=0), mat)
    return out[0:N_GATES] + out[N_GATES:2 * N_GATES] + out[2 * N_GATES:3 * N_GATES]


def _gate_rows(gt, cumt, h, backward):
    L = gt.shape[1]
    ig = (ML_HEADS if backward else 0) + h
    lf = (3 * ML_HEADS if backward else 2 * ML_HEADS) + h
    end = 0 if backward else L - 1
    return gt[ig:ig + 1, :], cumt[lf:lf + 1, :], cumt[lf:lf + 1, end:end + 1]


def _stage_weighted_values(lhs_ref, row0, vt_ref, cs, gt, cumt, backward):
    L = gt.shape[1]
    H = ML_HEADS
    ig0, lf0, end = (H, 3 * H, 0) if backward else (0, 2 * H, L - 1)
    b_rows = cumt[lf0:lf0 + H]
    b_end = b_rows[:, end:end + 1]
    w_end = b_end - b_rows + gt[ig0:ig0 + H]
    m_loc = jnp.max(w_end, axis=1, keepdims=True)
    wk = jnp.exp2(w_end - m_loc)
    for h in range(H):
        lhs_ref[row0 + h * VT_ROWS:row0 + (h + 1) * VT_ROWS, :] = (
            vt_ref[h * VT_ROWS:(h + 1) * VT_ROWS, cs].astype(F32) * wk[h:h + 1]).astype(lhs_ref.dtype)
    return m_loc, b_end


def _pair_update(lhs_ref, row0, k_pair, p, m_loc, b_end, lo):
    r0 = row0 + 2 * p * VT_ROWS
    out = _dot(lhs_ref[r0:r0 + 2 * VT_ROWS, :], k_pair)
    hs = (2 * p, 2 * p + 1)
    return (jnp.where(lo, out[0:VT_ROWS], out[VT_ROWS:2 * VT_ROWS]),
            [m_loc[h:h + 1] for h in hs], [b_end[h:h + 1] for h in hs])


def _apply_update(s_pair, upd, m_locs, b_ends, ms, lo):
    decays, gains, m_news = [], [], []
    for e in range(2):
        m_new = jnp.maximum(b_ends[e] + ms[e], m_locs[e])
        decays.append(jnp.exp2(b_ends[e] + ms[e] - m_new))
        gains.append(jnp.exp2(m_locs[e] - m_new))
        m_news.append(m_new)
    return jnp.where(lo, decays[0], decays[1]) * s_pair + jnp.where(lo, gains[0], gains[1]) * upd, m_news


def _absorb_chunks(k_ref, vt_ref, gt_ref, s_ref, m_ref, spre_ref, mpre_ref, lhs_ref, backward, cps):
    L = ML_CHUNK
    H = ML_HEADS
    lo = _lo_lanes()
    order = list(range(cps - 1, -1, -1)) if backward else list(range(cps))
    npair = H // 2
    scal = {}
    for u in order:
        cs = slice(u * L, (u + 1) * L)
        gt = _log2_gates(gt_ref, cs)
        scal[u] = _stage_weighted_values(lhs_ref, u * H * VT_ROWS, vt_ref, cs, gt, _cum_rows(gt, backward), backward)
    updates = {}
    for u in order:
        cs = slice(u * L, (u + 1) * L)
        for p in range(npair):
            updates[u, p] = _pair_update(lhs_ref, u * H * VT_ROWS, k_ref[cs, p * LANES:(p + 1) * LANES], p,
                                         *scal[u], lo)
    for u in order:
        spre_ref[u] = s_ref[...]
        mpre_ref[u] = m_ref[...]
        for p in range(npair):
            hs = (2 * p, 2 * p + 1)
            ms = [m_ref[h:h + 1, 0:1] for h in hs]
            s_new, m_new = _apply_update(s_ref[p], *updates[u, p], ms, lo)
            s_ref[p] = s_new
            for e in range(2):
                m_ref[hs[e]:hs[e] + 1, :] = jnp.broadcast_to(m_new[e], (1, LANES))


def _ml_state_kernel(k_ref, vt_ref, gt_ref, s0_ref, m0_ref, spre_ref, mpre_ref, sfin_ref, mfin_ref, lhs_ref, *,
                     backward, cps):
    @pl.when(pl.program_id(1) == 0)
    def _():
        sfin_ref[...] = s0_ref[...]
        mfin_ref[...] = m0_ref[...]

    _absorb_chunks(k_ref, vt_ref, gt_ref, sfin_ref.at[0], mfin_ref.at[0], spre_ref.at[0], mpre_ref.at[0],
                   lhs_ref, backward, cps)


def _ml_state_scan(k, vt, gt, s0, m0, batch, t, backward, cps):
    L = ML_CHUNK
    nc = t // L
    cps = min(cps, nc)
    assert t % L == 0 and nc % cps == 0
    steps = nc // cps
    np_ = ML_HEADS // 2
    si = (lambda s: steps - 1 - s) if backward else (lambda s: s)
    st = pl.BlockSpec((1, np_, VT_ROWS, LANES), lambda b, s: (b, 0, 0, 0))
    mx = pl.BlockSpec((1, ML_HEADS, LANES), lambda b, s: (b, 0, 0))
    return pl.pallas_call(
        functools.partial(_ml_state_kernel, backward=backward, cps=cps),
        grid=(batch, steps),
        in_specs=[pl.BlockSpec((cps * L, D_CQK), lambda b, s: (b * steps + si(s), 0)),
                  pl.BlockSpec((ML_HEADS * VT_ROWS, cps * L), lambda b, s: (b, si(s))),
                  pl.BlockSpec((LANES, cps * L), lambda b, s: (b, si(s))), st, mx],
        out_specs=[pl.BlockSpec((1, cps, np_, VT_ROWS, LANES), lambda b, s: (b, si(s), 0, 0, 0)),
                   pl.BlockSpec((1, cps, ML_HEADS, LANES), lambda b, s: (b, si(s), 0, 0)), st, mx],
        out_shape=[jax.ShapeDtypeStruct((batch, nc, np_, VT_ROWS, LANES), F32),
                   jax.ShapeDtypeStruct((batch, nc, ML_HEADS, LANES), F32),
                   jax.ShapeDtypeStruct((batch, np_, VT_ROWS, LANES), F32),
                   jax.ShapeDtypeStruct((batch, ML_HEADS, LANES), F32)],
        scratch_shapes=[pltpu.VMEM((cps * ML_HEADS * VT_ROWS, L), BF16)],
        compiler_params=_cparams(("arbitrary", "arbitrary")),
        name="ml_state_bwd" if backward else "ml_state_fwd",
    )(k, vt, gt, s0, m0)


def _ml_out_kernel(q_ref, k_ref, vt_ref, gt_ref, sg_ref, sb_ref, mb_ref, s0_ref, m0_ref, ngb_ref, o_ref,
                   s_ref, m_ref, lhs_ref, *, cps):
    @pl.when(pl.program_id(1) == 0)
    def _():
        s_ref[...] = s0_ref[0]
        m_ref[...] = m0_ref[0]

    L = ML_CHUNK
    lo = _lo_lanes()
    ri = lax.broadcasted_iota(jnp.int32, (L, L), 0)
    ci = lax.broadcasted_iota(jnp.int32, (L, L), 1)
    masks = (ri <= ci, ri >= ci)
    npair = ML_HEADS // 2
    for u in range(cps):
        cs = slice(u * L, (u + 1) * L)
        gt = _log2_gates(gt_ref, cs)
        cum_f = _cum_rows(gt, False)
        cumt = (cum_f, cum_f[:, L - 1:L] - cum_f + gt)
        r_rows = jnp.concatenate([gt[0:ML_HEADS] - cumt[0][2 * ML_HEADS:3 * ML_HEADS],
                                  gt[ML_HEADS:2 * ML_HEADS] - cumt[1][3 * ML_HEADS:4 * ML_HEADS],
                                  jnp.zeros((LANES - 2 * ML_HEADS, L), F32)], axis=0)
        r_cols = r_rows.T
        lhs_row0 = u * ML_HEADS * VT_ROWS
        upd_scal = _stage_weighted_values(lhs_ref, lhs_row0, vt_ref, cs, gt, cum_f, False)
        prods = []
        for p in range(npair):
            sl = slice(p * LANES, (p + 1) * LANES)
            stack = jnp.concatenate([k_ref[cs, sl], s_ref[p].astype(BF16), sb_ref[0, u, p].astype(BF16)], axis=0)
            prods.append(_dot_nt(stack, _pair_rows(q_ref[cs, sl], lo)))
        pts, inters, mqs = [], [], []
        for h in range(ML_HEADS):
            qs = slice((h % 2) * L, (h % 2 + 1) * L)
            for d in range(2):
                ig_row, b_row, _ = _gate_rows(gt, cumt[d], h, d == 1)
                r_col = r_cols[:, d * ML_HEADS + h:d * ML_HEADS + h + 1]
                m = mb_ref[0, u, h:h + 1, 0:1] if d else m_ref[h:h + 1, 0:1]
                g_row = b_row + m
                dt = jnp.where(masks[d], b_row + r_col, NEG_INF)
                m_q = jnp.maximum(g_row, jnp.max(dt, axis=0, keepdims=True))
                pts.append((jnp.exp2(dt - m_q) * prods[h // 2][0:L, qs]).astype(BF16))
                inters.append(jnp.exp2(g_row - m_q))
                mqs.append(m_q)
        nums = [_dot(vt_ref[h * VT_ROWS:(h + 1) * VT_ROWS, cs], jnp.concatenate(pts[2 * h:2 * h + 2], axis=1))
                for h in range(ML_HEADS)]
        for h in range(ML_HEADS):
            qs = slice((h % 2) * L, (h % 2 + 1) * L)
            hsum = None
            for d in range(2):
                i = 2 * h + d
                tot = (inters[i] * prods[h // 2][L + d * VT_ROWS:L + (d + 1) * VT_ROWS, qs]
                       + nums[h][:, d * L:(d + 1) * L])
                den = tot[ML_V_DIM:ML_V_DIM + 1]
                hd = tot[0:ML_V_DIM] / jnp.maximum(jnp.abs(den), jnp.exp2(-mqs[i]))
                hsum = hd if hsum is None else hsum + hd
            hs = slice(h * ML_V_DIM, (h + 1) * ML_V_DIM)
            ms = jnp.mean(hsum * hsum, axis=0, keepdims=True)
            y = hsum * lax.rsqrt(ms + EPS) * ngb_ref[hs, :]
            o_ref[cs, hs] = (y.T.astype(BF16) * sg_ref[cs, hs]).astype(o_ref.dtype)
        for p in range(npair):
            hp = (2 * p, 2 * p + 1)
            ms = [m_ref[h:h + 1, 0:1] for h in hp]
            upd = _pair_update(lhs_ref, lhs_row0, k_ref[cs, p * LANES:(p + 1) * LANES], p, *upd_scal, lo)
            s_new, m_new = _apply_update(s_ref[p], *upd, ms, lo)
            s_ref[p] = s_new
            for e in range(2):
                m_ref[hp[e]:hp[e] + 1, :] = jnp.broadcast_to(m_new[e], (1, LANES))


def _ml_out(q, k, vt, gt, sg, s_bwd, m_bwd, s0, m0, norm_gb, batch, t, cps):
    L = ML_CHUNK
    nc = t // L
    assert t % L == 0 and nc % cps == 0
    steps = nc // cps
    np_ = ML_HEADS // 2
    tok = lambda cols: pl.BlockSpec((cps * L, cols), lambda b, s: (b * steps + s, 0))
    return pl.pallas_call(
        functools.partial(_ml_out_kernel, cps=cps),
        grid=(batch, steps),
        in_specs=[tok(D_CQK), tok(D_CQK),
                  pl.BlockSpec((ML_HEADS * VT_ROWS, cps * L), lambda b, s: (b, s)),
                  pl.BlockSpec((LANES, cps * L), lambda b, s: (b, s)), tok(D_CV),
                  pl.BlockSpec((1, cps, np_, VT_ROWS, LANES), lambda b, s: (b, s, 0, 0, 0)),
                  pl.BlockSpec((1, cps, ML_HEADS, LANES), lambda b, s: (b, s, 0, 0)),
                  pl.BlockSpec((1, np_, VT_ROWS, LANES), lambda b, s: (b, 0, 0, 0)),
                  pl.BlockSpec((1, ML_HEADS, LANES), lambda b, s: (b, 0, 0)),
                  _const_spec((D_CV, LANES))],
        out_specs=tok(D_CV),
        out_shape=jax.ShapeDtypeStruct((batch * t, D_CV), BF16),
        scratch_shapes=[pltpu.VMEM((np_, VT_ROWS, LANES), F32), pltpu.VMEM((ML_HEADS, LANES), F32),
                        pltpu.VMEM((cps * ML_HEADS * VT_ROWS, L), BF16)],
        compiler_params=_cparams(("arbitrary", "arbitrary")),
        name="ml_out",
    )(q, k, vt, gt, sg, s_bwd, m_bwd, s0, m0, norm_gb)


def _even_layer(x2, ctx2, mod3, lat_row, ctx_row, batch, t, n, norm_mix_g, norm_ffn_g, ffn_w, w_in, qg, kg,
                table, cw, cb, lg, lb, w_out, ctx_out):
    tm = min(TILE_ROWS, t)
    tmc = min(TILE_ROWS, batch * n)
    ql, kl, vl, conv_l = _ab_in_conv(x2, mod3, norm_mix_g, w_in, qg, kg, cw, cb, lg, lb, t, min(2 * TILE_ROWS, t))
    qc, kc, vc, glu_c = _ab_in(ctx2, mod3, ctx_row, norm_mix_g, w_in, qg, kg, tmc)
    att_l = _na_attention(ql, kl, vl, kc, vc, table, batch, t, n, rps=NA_ROWS_PER_STEP)
    tmf = min(FFN_BLOCK_ROWS, t)
    x2 = _mix_ffn(x2, (att_l, conv_l), (0, 0), mod3, lat_row(t // tmf), norm_ffn_g, w_out, ffn_w, tmf)
    if ctx_out:
        att_c = _ctx_attention(qc, kc, vc, batch, n)
        conv_c = _conv_module(glu_c, cw, cb, lg, lb, batch, n)
        ctx2 = _mix_ffn(ctx2, (att_c, conv_c), (0, 0), mod3, ctx_row, norm_ffn_g, w_out, ffn_w, tmc)
    return x2, ctx2


def _odd_layer(x2, ctx2, mod3, lat_row, ctx_row, batch, t, n, norm_mix_g, norm_ffn_g, ffn_w, w_in, w_vt, cw, cb, gb,
               norm_gb, w_out):
    tm = min(TILE_ROWS, t)
    tmc = min(TILE_ROWS, n)
    cos_l, sin_l = _rope_tables(t, True)
    cos_c, sin_c = _rope_tables(n, False)
    _, kc, vtc, _, gtc = _ml_in(ctx2, mod3, lambda b: ctx_row(b), norm_mix_g, w_in, w_vt, cw, cb, gb, cos_c,
                                sin_c, batch, n, tmc)
    s_zero = jnp.zeros((batch, ML_HEADS // 2, VT_ROWS, LANES), F32)
    m_zero = jnp.zeros((batch, ML_HEADS, LANES), F32)
    _, _, sf, mf = _ml_state_scan(kc, vtc, gtc, s_zero, m_zero, batch, n, False, cps=SCAN_CHUNKS_PER_STEP)
    _, _, sb, mb = _ml_state_scan(kc, vtc, gtc, s_zero, m_zero, batch, n, True, cps=SCAN_CHUNKS_PER_STEP)
    ql, kl, vtl, sgl, gtl = _ml_in(x2, mod3, lambda b: b, norm_mix_g, w_in, w_vt, cw, cb, gb, cos_l, sin_l,
                                   batch, t, min(ML_IN_ROWS, t))
    s_pre, m_pre, _, _ = _ml_state_scan(kl, vtl, gtl, sb, mb, batch, t, True, cps=SCAN_CHUNKS_PER_STEP)
    gated = _ml_out(ql, kl, vtl, gtl, sgl, s_pre, m_pre, sf, mf, norm_gb, batch, t, cps=OUT_CHUNKS_PER_STEP)
    tmf = min(FFN_BLOCK_ROWS, t)
    return _mix_ffn(x2, (gated, gated), (0, 1), mod3, lat_row(t // tmf), norm_ffn_g, w_out, ffn_w, tmf)


def kernel(x, c, ctx, c_ctx, ada_w, ada_b, norm_mix_g, norm_ffn_g, ffn_w_gate, ffn_w_up, ffn_w_down, ab_w_in,
           na_q_norm_g, na_k_norm_g, na_rpb, conv_w, conv_b, conv_ln_g, conv_ln_b, ab_w_out, ml_w_in, ml_conv_w,
           ml_conv_b, ml_gate_b, ml_norm_g, ml_w_out):
    batch, t, d = x.shape
    n = ctx.shape[1]
    depth = ada_w.shape[0]
    assert batch + 1 <= MOD_ROWS and depth % 2 == 0, "odd layers are only implemented as the last-layer form"
    s_rows = jnp.zeros((MOD_ROWS, d), F32).at[:batch].set(c).at[batch].set(c_ctx)
    mod = _ada_mod(s_rows, ada_w, ada_b)
    x2 = x.reshape(batch * t, d)
    ctx2 = ctx.reshape(batch * n, d)
    lat_row = lambda tiles: (lambda i: i // tiles)
    ctx_row = lambda *_: batch
    ffn_bf16 = (ffn_w_gate.astype(BF16), ffn_w_up.astype(BF16), ffn_w_down.astype(BF16))
    rep8 = lambda v: jnp.broadcast_to(v[None, :], (SUBLANES, v.shape[0]))
    for l in range(depth):
        j = l // 2
        last = l == depth - 1
        mod3 = mod[l].reshape(MOD_ROWS, 1, 6 * d)
        ffn_w = (l,) + ffn_bf16
        nmg = norm_mix_g[l].reshape(1, d)
        nfg = norm_ffn_g[l].reshape(1, d)
        if l % 2 == 0:
            qg = (jnp.tile(na_q_norm_g[j], NA_HEADS) * (NA_HEAD_DIM ** -0.5 * LOG2E)).reshape(1, D_A)
            kg = jnp.tile(na_k_norm_g[j], NA_HEADS).reshape(1, D_A)
            x2, ctx2 = _even_layer(x2, ctx2, mod3, lat_row, ctx_row, batch, t, n, nmg, nfg, ffn_w,
                                   ab_w_in[j].astype(BF16), qg, kg, _na_bias_table(na_rpb[j] * LOG2E),
                                   jnp.broadcast_to(conv_w[j][:, None, :], (CONV_WIDTH, SUBLANES, D_B)),
                                   rep8(conv_b[j]), rep8(conv_ln_g[j]), rep8(conv_ln_b[j]),
                                   ab_w_out[j].astype(BF16), not last)
        else:
            assert last
            nqk = 2 * D_CQK
            w = ml_w_in[j]
            w_gates = jnp.pad(w[:, nqk + 2 * D_CV:], ((0, 0), (0, LANES - 4 * ML_HEADS)))
            w_in = jnp.concatenate([w[:, :nqk], w[:, nqk + D_CV:nqk + 2 * D_CV], w_gates], axis=1).astype(BF16)
            w_vt = w[:, nqk:nqk + D_CV].T.astype(BF16)
            gb = jnp.pad(ml_gate_b[j], (0, LANES - 4 * ML_HEADS)).reshape(1, LANES)
            norm_gb = jnp.broadcast_to(ml_norm_g[j][:, None], (D_CV, LANES))
            x2 = _odd_layer(x2, ctx2, mod3, lat_row, ctx_row, batch, t, n, nmg, nfg, ffn_w, w_in, w_vt,
                            ml_conv_w[j], ml_conv_b[j].reshape(1, nqk), gb, norm_gb, ml_w_out[j].astype(BF16))
    return x2.reshape(batch, t, d)
```
